```python
import math
import jax, jax.numpy as jnp
from jax import lax
import numpy as np

D_MODEL = 2048
BATCH = 4
SEQ = 4096
DEPTH = 2

S5_WIDTH = D_MODEL // 2
S5_GROUP = 16
S5_GROUPS = S5_WIDTH // S5_GROUP
S5_STATE = 64
S5_DT_MIN = 1e-3
S5_DT_MAX = 1e-1

ATT_HEAD_DIM = 64
ATT_Q_HEADS = (D_MODEL - S5_WIDTH) // ATT_HEAD_DIM
ATT_KV_HEADS = max(1, ATT_Q_HEADS // 8)
WINDOW = 128
ROPE_THETA = 500000.0
ROPE_DIM = ATT_HEAD_DIM // 4
MAX_POS_OFFSET = 1024

SSD_INNER = 2 * D_MODEL
SSD_HEAD_DIM = 64
SSD_HEADS = SSD_INNER // SSD_HEAD_DIM
SSD_GROUPS = 8
SSD_STATE = 128
SSD_CONV = 4
SSD_CHUNK = 128

MOE_GROUPS = 4
MOE_EXPERTS_PER_GROUP = 8
MOE_EXPERTS = MOE_GROUPS * MOE_EXPERTS_PER_GROUP
MOE_TOPK = 2
MOE_FF = D_MODEL // 8

EPS = 1e-6

kernel_name = 'hybrid_s5_swa_ssd_hmoe_trunk'


def rmsnorm(x, g):
    xf = x.astype(jnp.float32)
    y = xf * lax.rsqrt(jnp.mean(xf * xf, axis=-1, keepdims=True) + EPS)
    return (y * g.astype(jnp.float32)).astype(x.dtype)


def modulate(h, shift, scale):
    return h * (1.0 + scale[:, None, :]) + shift[:, None, :]


def rope_tables(positions):
    inv = 1.0 / (ROPE_THETA ** (jnp.arange(0, ROPE_DIM, 2, dtype=jnp.float32) / ROPE_DIM))
    ang = positions.astype(jnp.float32)[..., None] * inv
    return jnp.cos(ang), jnp.sin(ang)


def apply_partial_rope(x, cos, sin):
    half = ROPE_DIM // 2
    xr = x[..., :ROPE_DIM].astype(jnp.float32)
    x1, x2 = xr[..., :half], xr[..., half:]
    cs, sn = cos[:, :, None, :], sin[:, :, None, :]
    rot = jnp.concatenate([x1 * cs - x2 * sn, x2 * cs + x1 * sn], axis=-1).astype(x.dtype)
    return jnp.concatenate([rot, x[..., ROPE_DIM:]], axis=-1)


def s5_mixer(u, lam_re, lam_im, log_dt, b_re, b_im, c_re, c_im, d_skip, w_glu):
    bsz, L, _ = u.shape
    f32 = jnp.float32
    lam = lax.complex(lam_re.astype(f32), lam_im.astype(f32))
    dt = jnp.exp(log_dt.astype(f32))[:, None]
    lam_bar = jnp.exp(lam * dt)
    bmat = lax.complex(b_re.astype(f32), b_im.astype(f32))
    b_bar = ((lam_bar - 1.0) / lam)[..., None] * bmat
    ug = u.astype(f32).reshape(bsz, L, S5_GROUPS, S5_GROUP)
    bu = jnp.einsum('gnp,blgp->lbgn', b_bar, ug)
    a = jnp.broadcast_to(lam_bar, (L, 1, S5_GROUPS, S5_STATE))

    def combine(e1, e2):
        a1, b1 = e1
        a2, b2 = e2
        return a1 * a2, a2 * b1 + b2

    _, states = lax.associative_scan(combine, (a, bu), axis=0)
    cmat = lax.complex(c_re.astype(f32), c_im.astype(f32))
    y = jnp.real(jnp.einsum('gpn,lbgn->blgp', cmat, states)).reshape(bsz, L, S5_WIDTH)
    y = y + d_skip.astype(f32) * u.astype(f32)
    y = jax.nn.gelu(y).astype(u.dtype)
    val, gate = jnp.split(y @ w_glu, 2, axis=-1)
    return val * jax.nn.sigmoid(gate)


def swa_sink_attention(q, k, v, sinks):
    bsz, L, hq, dh = q.shape
    nb = L // WINDOW
    grp = hq // ATT_KV_HEADS
    qb = q.reshape(bsz, nb, WINDOW, ATT_KV_HEADS, grp, dh)

    def with_prev(t):
        tb = t.reshape(bsz, nb, WINDOW, ATT_KV_HEADS, dh)
        prev = jnp.pad(tb, ((0, 0), (1, 0), (0, 0), (0, 0), (0, 0)))[:, :-1]
        return jnp.concatenate([prev, tb], axis=2)

    kk, vv = with_prev(k), with_prev(v)
    s = jnp.einsum('bnqhgd,bnkhd->bnhgqk', qb, kk,
                   preferred_element_type=jnp.float32) * (ATT_HEAD_DIM ** -0.5)
    blk = jnp.arange(nb)[:, None, None] * WINDOW
    qpos = blk + jnp.arange(WINDOW)[None, :, None]
    kpos = blk - WINDOW + jnp.arange(2 * WINDOW)[None, None, :]
    rel = qpos - kpos
    mask = (rel >= 0) & (rel < WINDOW) & (kpos >= 0)
    s = jnp.where(mask[None, :, None, None], s, -jnp.inf)
    sink = sinks.astype(jnp.float32).reshape(ATT_KV_HEADS, grp)[None, None, :, :, None, None]
    sink = jnp.broadcast_to(sink, s.shape[:-1] + (1,))
    p = jax.nn.softmax(jnp.concatenate([s, sink], axis=-1), axis=-1)[..., :-1]
    o = jnp.einsum('bnhgqk,bnkhd->bnqhgd', p.astype(v.dtype), vv)
    return o.reshape(bsz, L, hq * dh)


def s5_swa_mixer(h, cos, sin, w_in, lam_re, lam_im, log_dt, b_re, b_im, c_re, c_im,
                 d_skip, w_glu, sinks, w_out):
    bsz, L, _ = h.shape
    hd = ATT_HEAD_DIM
    proj = h @ w_in
    q_end = S5_WIDTH + ATT_Q_HEADS * hd
    k_end = q_end + ATT_KV_HEADS * hd
    u, q, k, v = jnp.split(proj, [S5_WIDTH, q_end, k_end], axis=-1)
    q = apply_partial_rope(q.reshape(bsz, L, ATT_Q_HEADS, hd), cos, sin)
    k = apply_partial_rope(k.reshape(bsz, L, ATT_KV_HEADS, hd), cos, sin)
    v = v.reshape(bsz, L, ATT_KV_HEADS, hd)
    y_s5 = s5_mixer(u, lam_re, lam_im, log_dt, b_re, b_im, c_re, c_im, d_skip, w_glu)
    y_att = swa_sink_attention(q, k, v, sinks)
    return jnp.concatenate([y_s5, y_att.astype(y_s5.dtype)], axis=-1) @ w_out


def causal_depthwise_conv(x, w, b):
    out = lax.conv_general_dilated(
        x, w[:, None, :].astype(x.dtype), window_strides=(1,),
        padding=[(SSD_CONV - 1, 0)], dimension_numbers=('NWC', 'WIO', 'NWC'),
        feature_group_count=x.shape[-1])
    return out + b.astype(x.dtype)


def segsum_exp(a_cs):
    l = a_cs.shape[-1]
    diff = a_cs[..., :, None] - a_cs[..., None, :]
    mask = jnp.tril(jnp.ones((l, l), dtype=bool))
    return jnp.exp(jnp.where(mask, diff, -jnp.inf))


def ssd_scan(x, dt, a, bm, cm):
    bsz, L, H, P = x.shape
    nc = L // SSD_CHUNK
    J = H // SSD_GROUPS
    xc = (x * dt[..., None]).reshape(bsz, nc, SSD_CHUNK, SSD_GROUPS, J, P)
    ac = jnp.moveaxis((dt * a).reshape(bsz, nc, SSD_CHUNK, SSD_GROUPS, J), 2, -1)
    bc = bm.reshape(bsz, nc, SSD_CHUNK, SSD_GROUPS, SSD_STATE)
    cc = cm.reshape(bsz, nc, SSD_CHUNK, SSD_GROUPS, SSD_STATE)
    a_cs = jnp.cumsum(ac, axis=-1)
    lmat = segsum_exp(a_cs)
    cb = jnp.einsum('bclgn,bcsgn->bcgls', cc, bc)
    y_diag = jnp.einsum('bcgls,bcgjls,bcsgjp->bclgjp', cb, lmat, xc)
    decay_states = jnp.exp(a_cs[..., -1:] - a_cs)
    states = jnp.einsum('bclgn,bcgjl,bclgjp->bcgjpn', bc, decay_states, xc)
    chunk_decay = jnp.exp(a_cs[..., -1])

    def step(hs, inp):
        dec, st = inp
        return hs * dec[..., None, None] + st, hs

    h0 = jnp.zeros((bsz, SSD_GROUPS, J, P, SSD_STATE), jnp.float32)
    _, prev = lax.scan(step, h0, (jnp.moveaxis(chunk_decay, 1, 0), jnp.moveaxis(states, 1, 0)))
    prev = jnp.moveaxis(prev, 0, 1)
    y_off = jnp.einsum('bclgn,bcgjpn,bcgjl->bclgjp', cc, prev, jnp.exp(a_cs))
    return (y_diag + y_off).reshape(bsz, L, H, P)


def mamba2_mixer(h, w_in, conv_w, conv_b, dt_bias, a_log, d_skip, norm_w, w_out):
    bsz, L, _ = h.shape
    f32 = jnp.float32
    bc_width = SSD_GROUPS * SSD_STATE
    proj = h @ w_in
    z, xbc, dt = jnp.split(proj, [SSD_INNER, 2 * SSD_INNER + 2 * bc_width], axis=-1)
    xbc = jax.nn.silu(causal_depthwise_conv(xbc, conv_w, conv_b))
    xs, bm, cm = jnp.split(xbc, [SSD_INNER, SSD_INNER + bc_width], axis=-1)
    dt = jax.nn.softplus(dt.astype(f32) + dt_bias.astype(f32))
    a = -jnp.exp(a_log.astype(f32))
    xh = xs.astype(f32).reshape(bsz, L, SSD_HEADS, SSD_HEAD_DIM)
    y = ssd_scan(xh, dt, a,
                 bm.astype(f32).reshape(bsz, L, SSD_GROUPS, SSD_STATE),
                 cm.astype(f32).reshape(bsz, L, SSD_GROUPS, SSD_STATE))
    y = y + d_skip.astype(f32)[:, None] * xh
    y = y.reshape(bsz, L, SSD_INNER) * jax.nn.silu(z.astype(f32))
    yg = y.reshape(bsz, L, SSD_GROUPS, SSD_INNER // SSD_GROUPS)
    yg = yg * lax.rsqrt(jnp.mean(yg * yg, axis=-1, keepdims=True) + EPS)
    y = yg.reshape(bsz, L, SSD_INNER) * norm_w.astype(f32)
    return y.astype(h.dtype) @ w_out


def hier_moe(h, r1_w, r1_b, r2_w, r2_b, w_gate, w_up, w_down):
    bsz, L, D = h.shape
    f32 = jnp.float32
    t = h.reshape(-1, D)
    g_logits = (t @ r1_w).astype(f32) + r1_b.astype(f32)
    g_prob = jax.nn.softmax(g_logits, axis=-1)
    g_val, g_idx = lax.top_k(g_prob, 1)
    e_logits = ((t @ r2_w).astype(f32) + r2_b.astype(f32)).reshape(-1, MOE_GROUPS, MOE_EXPERTS_PER_GROUP)
    e_logits = jnp.take_along_axis(e_logits, g_idx[:, :, None], axis=1)[:, 0]
    top_v, top_i = lax.top_k(e_logits, MOE_TOPK)
    top_w = jax.nn.softmax(top_v, axis=-1) * g_val
    local = jnp.sum(jax.nn.one_hot(top_i, MOE_EXPERTS_PER_GROUP, dtype=f32) * top_w[..., None], axis=1)
    combine = (jax.nn.one_hot(g_idx[:, 0], MOE_GROUPS, dtype=f32)[:, :, None]
               * local[:, None, :]).reshape(-1, MOE_EXPERTS)
    hid = jax.nn.silu(jnp.einsum('td,edf->tef', t, w_gate)) * jnp.einsum('td,edf->tef', t, w_up)
    out = jnp.einsum('tef,te,efd->td', hid, combine.astype(hid.dtype), w_down)
    return out.reshape(bsz, L, D)


def setup_inputs(seed: int = 0) -> dict:
    key = jax.random.key(seed)
    keys = iter(jax.random.split(key, 48))
    f32 = jnp.float32
    D = D_MODEL
    ne = (DEPTH + 1) // 2
    no = DEPTH // 2

    def normal(shape, scale):
        return jax.random.normal(next(keys), shape, f32) * scale

    def gain(shape):
        return 1.0 + normal(shape, 0.02)

    mix_a_cols = S5_WIDTH + (ATT_Q_HEADS + 2 * ATT_KV_HEADS) * ATT_HEAD_DIM
    conv_ch = SSD_INNER + 2 * SSD_GROUPS * SSD_STATE
    ssd_cols = SSD_INNER + conv_ch + SSD_HEADS

    x = normal((BATCH, SEQ, D), 1.0)
    c = normal((BATCH, D), 1.0)
    offset = jax.random.randint(next(keys), (BATCH, 1), 0, MAX_POS_OFFSET, jnp.int32)
    positions = offset + jnp.arange(SEQ, dtype=jnp.int32)[None, :]

    ada_w = normal((DEPTH, D, 6 * D), 0.5 * D ** -0.5)
    ada_b = normal((DEPTH, 6 * D), 0.02)
    norm_mix = gain((DEPTH, D))
    norm_ffn = gain((DEPTH, D))

    mix_a_w_in = normal((ne, D, mix_a_cols), D ** -0.5)
    s5_lam_re = -0.5 + normal((ne, S5_GROUPS, S5_STATE), 0.005)
    s5_lam_im = jnp.pi * jnp.arange(S5_STATE, dtype=f32) + normal((ne, S5_GROUPS, S5_STATE), 0.01)
    s5_log_dt = jax.random.uniform(next(keys), (ne, S5_GROUPS), f32,
                                   math.log(S5_DT_MIN), math.log(S5_DT_MAX))
    s5_b_re = normal((ne, S5_GROUPS, S5_STATE, S5_GROUP), (2 * S5_GROUP) ** -0.5)
    s5_b_im = normal((ne, S5_GROUPS, S5_STATE, S5_GROUP), (2 * S5_GROUP) ** -0.5)
    s5_c_re = normal((ne, S5_GROUPS, S5_GROUP, S5_STATE), 4.0 * S5_STATE ** -0.5)
    s5_c_im = normal((ne, S5_GROUPS, S5_GROUP, S5_STATE), 4.0 * S5_STATE ** -0.5)
    s5_d = normal((ne, S5_WIDTH), 1.0)
    s5_w_glu = normal((ne, S5_WIDTH, 2 * S5_WIDTH), S5_WIDTH ** -0.5)
    attn_sinks = normal((ne, ATT_Q_HEADS), 0.5)
    mix_a_w_out = normal((ne, D, D), D ** -0.5)

    ssd_w_in = normal((no, D, ssd_cols), D ** -0.5)
    ssd_conv_w = normal((no, SSD_CONV, conv_ch), 0.5 * SSD_CONV ** -0.5)
    ssd_conv_b = normal((no, conv_ch), 0.02)
    dt0 = jnp.exp(jax.random.uniform(next(keys), (no, SSD_HEADS), f32, math.log(1e-3), math.log(1e-1)))
    ssd_dt_bias = dt0 + jnp.log(-jnp.expm1(-dt0))
    ssd_a_log = jnp.log(jax.random.uniform(next(keys), (no, SSD_HEADS), f32, 1.0, 16.0))
    ssd_d = gain((no, SSD_HEADS))
    ssd_norm_w = gain((no, SSD_INNER))
    ssd_w_out = normal((no, SSD_INNER, D), SSD_INNER ** -0.5)

    moe_r1_w = normal((DEPTH, D, MOE_GROUPS), D ** -0.5)
    moe_r1_b = normal((DEPTH, MOE_GROUPS), 0.01)
    moe_r2_w = normal((DEPTH, D, MOE_EXPERTS), D ** -0.5)
    moe_r2_b = normal((DEPTH, MOE_EXPERTS), 0.01)
    moe_w_gate = normal((DEPTH, MOE_EXPERTS, D, MOE_FF), D ** -0.5)
    moe_w_up = normal((DEPTH, MOE_EXPERTS, D, MOE_FF), D ** -0.5)
    moe_w_down = normal((DEPTH, MOE_EXPERTS, MOE_FF, D), MOE_FF ** -0.5)
    final_norm = gain((D,))

    return {'x': x, 'c': c, 'positions': positions,
            'ada_w': ada_w, 'ada_b': ada_b, 'norm_mix': norm_mix, 'norm_ffn': norm_ffn,
            'mix_a_w_in': mix_a_w_in, 's5_lam_re': s5_lam_re, 's5_lam_im': s5_lam_im,
            's5_log_dt': s5_log_dt, 's5_b_re': s5_b_re, 's5_b_im': s5_b_im,
            's5_c_re': s5_c_re, 's5_c_im': s5_c_im, 's5_d': s5_d, 's5_w_glu': s5_w_glu,
            'attn_sinks': attn_sinks, 'mix_a_w_out': mix_a_w_out,
            'ssd_w_in': ssd_w_in, 'ssd_conv_w': ssd_conv_w, 'ssd_conv_b': ssd_conv_b,
            'ssd_dt_bias': ssd_dt_bias, 'ssd_a_log': ssd_a_log, 'ssd_d': ssd_d,
            'ssd_norm_w': ssd_norm_w, 'ssd_w_out': ssd_w_out,
            'moe_r1_w': moe_r1_w, 'moe_r1_b': moe_r1_b, 'moe_r2_w': moe_r2_w, 'moe_r2_b': moe_r2_b,
            'moe_w_gate': moe_w_gate, 'moe_w_up': moe_w_up, 'moe_w_down': moe_w_down,
            'final_norm': final_norm}


def reference(x, c, positions, ada_w, ada_b, norm_mix, norm_ffn,
              mix_a_w_in, s5_lam_re, s5_lam_im, s5_log_dt, s5_b_re, s5_b_im,
              s5_c_re, s5_c_im, s5_d, s5_w_glu, attn_sinks, mix_a_w_out,
              ssd_w_in, ssd_conv_w, ssd_conv_b, ssd_dt_bias, ssd_a_log, ssd_d,
              ssd_norm_w, ssd_w_out,
              moe_r1_w, moe_r1_b, moe_r2_w, moe_r2_b, moe_w_gate, moe_w_up, moe_w_down,
              final_norm):
    cos, sin = rope_tables(positions)
    c_act = jax.nn.silu(c)
    for i in range(DEPTH):
        mod = c_act @ ada_w[i] + ada_b[i]
        sh1, sc1, g1, sh2, sc2, g2 = jnp.split(mod, 6, axis=-1)
        h = modulate(rmsnorm(x, norm_mix[i]), sh1, sc1)
        j = i // 2
        if i % 2 == 0:
            y = s5_swa_mixer(h, cos, sin, mix_a_w_in[j], s5_lam_re[j], s5_lam_im[j], s5_log_dt[j],
                             s5_b_re[j], s5_b_im[j], s5_c_re[j], s5_c_im[j], s5_d[j], s5_w_glu[j],
                             attn_sinks[j], mix_a_w_out[j])
        else:
            y = mamba2_mixer(h, ssd_w_in[j], ssd_conv_w[j], ssd_conv_b[j], ssd_dt_bias[j],
                             ssd_a_log[j], ssd_d[j], ssd_norm_w[j], ssd_w_out[j])
        x = x + g1[:, None, :] * y.astype(x.dtype)
        h = modulate(rmsnorm(x, norm_ffn[i]), sh2, sc2)
        y = hier_moe(h, moe_r1_w[i], moe_r1_b[i], moe_r2_w[i], moe_r2_b[i],
                     moe_w_gate[i], moe_w_up[i], moe_w_down[i])
        x = x + g2[:, None, :] * y.astype(x.dtype)
    return rmsnorm(x, final_norm)
```

```python
import functools
import math

import jax
import jax.numpy as jnp
from jax import lax
from jax.experimental import pallas as pl
from jax.experimental.pallas import tpu as pltpu

F32 = jnp.float32
BF16 = jnp.bfloat16
HIGHEST = lax.Precision.HIGHEST

EPS = 1e-6
S5_GROUP = 16
S5_STATE = 64
S5_CHUNK = 16
HEAD_DIM = 64
KV_HEADS = 2
WINDOW = 128
ROPE_DIM = 16
ROPE_THETA = 500000.0
SSD_HEAD_DIM = 64
SSD_GROUPS = 8
SSD_STATE = 128
SSD_CONV = 4
SSD_CHUNK = 128
MOE_GROUPS = 4
MOE_EPG = 8
MOE_EXPERTS = MOE_GROUPS * MOE_EPG
EXPERT_TILE = 256
NEG_BIG = -1e30
VMEM_LIMIT = 56 * 1024 * 1024


def _cparams(*sem):
    return pltpu.CompilerParams(dimension_semantics=sem, vmem_limit_bytes=VMEM_LIMIT)


def _modnorm(x, g, sh, sc):
    ms = jnp.mean(x * x, axis=-1, keepdims=True)
    return (x * lax.rsqrt(ms + EPS) * g) * (1.0 + sc) + sh


def _adaln_kernel(c_ref, w_ref, b_ref, o_ref):
    c = c_ref[...]
    a = c * jax.nn.sigmoid(c)
    o_ref[0] = jnp.dot(a, w_ref[0], precision=HIGHEST, preferred_element_type=F32) + b_ref[0]


def _adaln(c, ada_w, ada_b):
    depth, d, n = ada_w.shape
    bsz = c.shape[0]
    rows = 8
    cp = jnp.zeros((rows, d), F32).at[:bsz].set(c)
    tn = 1024
    out = pl.pallas_call(
        _adaln_kernel,
        out_shape=jax.ShapeDtypeStruct((depth, rows, n), F32),
        grid=(depth, n // tn),
        in_specs=[pl.BlockSpec((rows, d), lambda l, j: (0, 0)),
                  pl.BlockSpec((1, d, tn), lambda l, j: (l, 0, j)),
                  pl.BlockSpec((1, 1, tn), lambda l, j: (l, 0, j))],
        out_specs=pl.BlockSpec((1, rows, tn), lambda l, j: (l, 0, j)),
        compiler_params=_cparams("arbitrary", "arbitrary"),
        name="adaln",
    )(cp, ada_w, ada_b.reshape(depth, 1, n))
    return out[:, :bsz]


def _nmm_kernel(x_ref, g_ref, sh_ref, sc_ref, w_ref, o_ref, h_ref):
    @pl.when(pl.program_id(1) == 0)
    def _():
        h_ref[...] = _modnorm(x_ref[...], g_ref[...], sh_ref[0], sc_ref[0]).astype(BF16)

    o_ref[...] = jnp.dot(h_ref[...], w_ref[...], preferred_element_type=F32).astype(o_ref.dtype)


def _nmm2_kernel(x_ref, g_ref, sh_ref, sc_ref, w_ref, w2_ref, o_ref, o2_ref, h_ref):
    @pl.when(pl.program_id(1) == 0)
    def _():
        h = _modnorm(x_ref[...], g_ref[...], sh_ref[0], sc_ref[0]).astype(BF16)
        h_ref[...] = h
        o2_ref[...] = jnp.dot(h, w2_ref[...], preferred_element_type=F32)

    o_ref[...] = jnp.dot(h_ref[...], w_ref[...], preferred_element_type=F32).astype(o_ref.dtype)


def _norm_mod_matmul(x2d, seq, g, sh, sc, w, tm, tn, w2=None):
    t, d = x2d.shape
    n = w.shape[1]
    tm = min(tm, seq)
    bmap = lambda i, j: ((i * tm) // seq, 0, 0)
    in_specs = [pl.BlockSpec((tm, d), lambda i, j: (i, 0)),
                pl.BlockSpec((1, d), lambda i, j: (0, 0)),
                pl.BlockSpec((1, 1, d), bmap),
                pl.BlockSpec((1, 1, d), bmap),
                pl.BlockSpec((d, tn), lambda i, j: (0, j))]
    args = [x2d, g.reshape(1, d), sh[:, None, :], sc[:, None, :], w]
    out_shape = jax.ShapeDtypeStruct((t, n), BF16)
    out_specs = pl.BlockSpec((tm, tn), lambda i, j: (i, j))
    kern = _nmm_kernel
    if w2 is not None:
        n2 = w2.shape[1]
        in_specs.append(pl.BlockSpec((d, n2), lambda i, j: (0, 0)))
        args.append(w2)
        out_shape = (out_shape, jax.ShapeDtypeStruct((t, n2), F32))
        out_specs = (out_specs, pl.BlockSpec((tm, n2), lambda i, j: (i, 0)))
        kern = _nmm2_kernel
    return pl.pallas_call(
        kern, out_shape=out_shape, grid=(t // tm, n // tn),
        in_specs=in_specs, out_specs=out_specs,
        scratch_shapes=[pltpu.VMEM((tm, d), BF16)],
        compiler_params=_cparams("arbitrary", "arbitrary"),
        name="norm_mod_matmul",
    )(*args)


def _mmres1_kernel(a_ref, w_ref, x_ref, gt_ref, o_ref):
    acc = jnp.dot(a_ref[...], w_ref[...], preferred_element_type=F32)
    o_ref[...] = x_ref[...] + gt_ref[0] * acc


def _mmres2_kernel(a_ref, a2_ref, w_ref, w2_ref, x_ref, gt_ref, o_ref):
    acc = jnp.dot(a_ref[...], w_ref[...], preferred_element_type=F32)
    acc = acc + jnp.dot(a2_ref[...], w2_ref[...], preferred_element_type=F32)
    o_ref[...] = x_ref[...] + gt_ref[0] * acc


def _matmul_residual(a_list, w_list, x2d, seq, gate, tm, tn):
    t, n = x2d.shape
    tm = min(tm, seq)
    in_specs, args = [], []
    for a in a_list:
        in_specs.append(pl.BlockSpec((tm, a.shape[1]), lambda i, j: (i, 0)))
        args.append(a)
    for w in w_list:
        in_specs.append(pl.BlockSpec((w.shape[0], tn), lambda i, j: (0, j)))
        args.append(w)
    in_specs.append(pl.BlockSpec((tm, tn), lambda i, j: (i, j)))
    in_specs.append(pl.BlockSpec((1, 1, tn), lambda i, j: ((i * tm) // seq, 0, j)))
    args += [x2d, gate[:, None, :]]
    kern = _mmres1_kernel if len(a_list) == 1 else _mmres2_kernel
    return pl.pallas_call(
        kern, out_shape=jax.ShapeDtypeStruct((t, n), F32), grid=(t // tm, n // tn),
        in_specs=in_specs, out_specs=pl.BlockSpec((tm, tn), lambda i, j: (i, j)),
        compiler_params=_cparams("arbitrary", "arbitrary"),
        name="matmul_residual",
    )(*args)


def _s5_params(lam_re, lam_im, log_dt, b_re, b_im, c_re, c_im, d_skip):
    g, n = lam_re.shape
    p = S5_GROUP
    lc = S5_CHUNK
    lam = lax.complex(lam_re.astype(F32), lam_im.astype(F32))
    dt = jnp.exp(log_dt.astype(F32))[:, None]
    ld = lam * dt
    lam_bar = jnp.exp(ld)
    bmat = lax.complex(b_re.astype(F32), b_im.astype(F32))
    b_bar = ((lam_bar - 1.0) / lam)[..., None] * bmat
    cmat = lax.complex(c_re.astype(F32), c_im.astype(F32))
    ks = jnp.arange(lc, dtype=F32)
    pw = jnp.exp(ld[:, None, :] * ks[None, :, None])
    pw1 = jnp.exp(ld[:, None, :] * (ks[None, :, None] + 1.0))
    lam_c = jnp.exp(ld * float(lc))
    kk = jnp.real(jnp.einsum('gpn,gkn,gnq->gkpq', cmat, pw, b_bar, precision=HIGHEST))
    li = jnp.arange(lc)
    lag = li[None, :] - li[:, None]
    m = kk[:, jnp.clip(lag, 0, lc - 1)]
    m = jnp.where((lag >= 0)[None, :, :, None, None], m, 0.0)
    m = m.transpose(0, 1, 4, 2, 3)
    eye = (jnp.eye(lc)[:, None, :, None] * jnp.eye(p)[None, :, None, :])
    m = m + eye[None] * d_skip.astype(F32).reshape(g, 1, 1, 1, p)
    m = m.reshape(g, lc * p, lc * p)
    wb = pw[:, ::-1, :, None] * b_bar[:, None, :, :]
    wb = wb.transpose(0, 1, 3, 2).reshape(g, lc * p, n)
    cl = cmat[:, None, :, :] * pw1[:, :, None, :]
    cl = cl.transpose(0, 3, 1, 2).reshape(g, n, lc * p)
    g2 = g // 2
    z_b = jnp.zeros((g2, lc * p, n), F32)
    wb_re, wb_im = jnp.real(wb).reshape(g2, 2, lc * p, n), jnp.imag(wb).reshape(g2, 2, lc * p, n)
    wb_pair = jnp.concatenate([
        jnp.concatenate([wb_re[:, 0], z_b, wb_im[:, 0], z_b], axis=-1),
        jnp.concatenate([z_b, wb_re[:, 1], z_b, wb_im[:, 1]], axis=-1)], axis=1)
    z_c = jnp.zeros((g2, n, lc * p), F32)
    cl_re, cl_im = jnp.real(cl).reshape(g2, 2, n, lc * p), jnp.imag(cl).reshape(g2, 2, n, lc * p)
    woff_pair = jnp.concatenate([
        jnp.concatenate([cl_re[:, 0], z_c], axis=-1),
        jnp.concatenate([z_c, cl_re[:, 1]], axis=-1),
        jnp.concatenate([-cl_im[:, 0], z_c], axis=-1),
        jnp.concatenate([z_c, -cl_im[:, 1]], axis=-1)], axis=1)
    a_re = jnp.real(lam_c).reshape(g2, 1, 2 * n)
    a_im = jnp.imag(lam_c).reshape(g2, 1, 2 * n)
    return m.astype(BF16), wb_pair.astype(BF16), woff_pair.astype(BF16), a_re, a_im


def _s5_local_kernel(u_ref, wb_ref, o_ref):
    k = u_ref.shape[2]
    o_ref[0] = (jnp.dot(u_ref[0], wb_ref[0, :k], preferred_element_type=F32)
                + jnp.dot(u_ref[1], wb_ref[0, k:], preferred_element_type=F32))


def _s5_scan_kernel(sloc_ref, are_ref, aim_ref, sin_ref, *, bsz):
    pb, ni, w = sloc_ref.shape
    n2 = w // 2
    are = are_ref[...]
    aim = aim_ref[...]
    sub = 8 // bsz

    def body(k, st):
        sre, sim = st
        r0 = pl.multiple_of(k * 8, 8)
        blk = sloc_ref[:, pl.ds(r0, 8), :]
        outs = []
        for j in range(sub):
            outs.append(jnp.concatenate([sre, sim], axis=-1))
            lre = blk[:, j * bsz:(j + 1) * bsz, :n2]
            lim = blk[:, j * bsz:(j + 1) * bsz, n2:]
            sre, sim = are * sre - aim * sim + lre, are * sim + aim * sre + lim
        sin_ref[:, pl.ds(r0, 8), :] = jnp.concatenate(outs, axis=1)
        return sre, sim

    z = jnp.zeros((pb, bsz, n2), F32)
    lax.fori_loop(0, ni // 8, body, (z, z))


def _s5_out_kernel(u_ref, sin_ref, m_ref, woff_ref, y_ref):
    k = u_ref.shape[2]
    s = sin_ref[0].astype(BF16)
    y_ref[0] = (jnp.dot(u_ref[0], m_ref[0], preferred_element_type=F32)
                + jnp.dot(s, woff_ref[0, :, :k], preferred_element_type=F32)).astype(y_ref.dtype)
    y_ref[1] = (jnp.dot(u_ref[1], m_ref[1], preferred_element_type=F32)
                + jnp.dot(s, woff_ref[0, :, k:], preferred_element_type=F32)).astype(y_ref.dtype)


def _s5_mixer(u, bsz, seq, params):
    m, wb_pair, woff_pair, a_re, a_im = params
    g = m.shape[0]
    g2 = g // 2
    lc, p = S5_CHUNK, S5_GROUP
    k = lc * p
    nc = seq // lc
    ni = nc * bsz
    ug = u.reshape(bsz, nc, lc, g, p).transpose(3, 1, 0, 2, 4).reshape(g, ni, k)
    sw = wb_pair.shape[2]
    sloc = pl.pallas_call(
        _s5_local_kernel, out_shape=jax.ShapeDtypeStruct((g2, ni, sw), F32), grid=(g2,),
        in_specs=[pl.BlockSpec((2, ni, k), lambda i: (i, 0, 0)),
                  pl.BlockSpec((1, 2 * k, sw), lambda i: (i, 0, 0))],
        out_specs=pl.BlockSpec((1, ni, sw), lambda i: (i, 0, 0)),
        compiler_params=_cparams("arbitrary"), name="s5_local",
    )(ug, wb_pair)
    pb = 8
    sin = pl.pallas_call(
        functools.partial(_s5_scan_kernel, bsz=bsz),
        out_shape=jax.ShapeDtypeStruct((g2, ni, sw), F32), grid=(g2 // pb,),
        in_specs=[pl.BlockSpec((pb, ni, sw), lambda i: (i, 0, 0)),
                  pl.BlockSpec((pb, 1, sw // 2), lambda i: (i, 0, 0)),
                  pl.BlockSpec((pb, 1, sw // 2), lambda i: (i, 0, 0))],
        out_specs=pl.BlockSpec((pb, ni, sw), lambda i: (i, 0, 0)),
        compiler_params=_cparams("arbitrary"), name="s5_scan",
    )(sloc, a_re, a_im)
    yg = pl.pallas_call(
        _s5_out_kernel, out_shape=jax.ShapeDtypeStruct((g, ni, k), BF16), grid=(g2,),
        in_specs=[pl.BlockSpec((2, ni, k), lambda i: (i, 0, 0)),
                  pl.BlockSpec((1, ni, sw), lambda i: (i, 0, 0)),
                  pl.BlockSpec((2, k, k), lambda i: (i, 0, 0)),
                  pl.BlockSpec((1, sw, 2 * k), lambda i: (i, 0, 0))],
        out_specs=pl.BlockSpec((2, ni, k), lambda i: (i, 0, 0)),
        compiler_params=_cparams("arbitrary"), name="s5_out",
    )(ug, sin, m, woff_pair)
    y = yg.reshape(g, nc, bsz, lc, p).transpose(2, 1, 3, 0, 4).reshape(bsz * seq, g * p)
    return y


def _glu_kernel(y_ref, wv_ref, wg_ref, o_ref, a_ref):
    @pl.when(pl.program_id(1) == 0)
    def _():
        a_ref[...] = jax.nn.gelu(y_ref[...].astype(F32), approximate=True).astype(BF16)

    a = a_ref[...]
    val = jnp.dot(a, wv_ref[...], preferred_element_type=F32)
    gate = jnp.dot(a, wg_ref[...], preferred_element_type=F32)
    o_ref[...] = (val * jax.nn.sigmoid(gate)).astype(o_ref.dtype)


def _gelu_glu(y, wv, wg, tm, tn):
    t, k = y.shape
    n = wv.shape[1]
    tm = min(tm, t)
    return pl.pallas_call(
        _glu_kernel, out_shape=jax.ShapeDtypeStruct((t, n), BF16), grid=(t // tm, n // tn),
        in_specs=[pl.BlockSpec((tm, k), lambda i, j: (i, 0)),
                  pl.BlockSpec((k, tn), lambda i, j: (0, j)),
                  pl.BlockSpec((k, tn), lambda i, j: (0, j))],
        out_specs=pl.BlockSpec((tm, tn), lambda i, j: (i, j)),
        scratch_shapes=[pltpu.VMEM((tm, k), BF16)],
        compiler_params=_cparams("arbitrary", "arbitrary"), name="gelu_glu",
    )(y, wv, wg)


def _rope(x, cos, sin):
    w = x.shape[1]
    half = ROPE_DIM // 2
    d = lax.broadcasted_iota(jnp.int32, x.shape, 1) % HEAD_DIM
    partner = jnp.where(d < half, pltpu.roll(x, w - half, axis=1), pltpu.roll(x, half, axis=1))
    return x * cos + partner * sin


def _attn_kernel(sink_ref, q_ref, kc_ref, kp_ref, vc_ref, vp_ref,
                 cq_ref, sq_ref, cp_ref, sp_ref, o_ref):
    n = pl.program_id(1)
    wq = q_ref.shape[1]
    hq = wq // HEAD_DIM
    grp = hq // KV_HEADS
    cos_c, sin_c = cq_ref[...], sq_ref[...]
    reps = wq // cos_c.shape[1]
    q = _rope(q_ref[...].astype(F32),
              jnp.concatenate([cos_c] * reps, axis=1), jnp.concatenate([sin_c] * reps, axis=1))
    q = (q * (HEAD_DIM ** -0.5)).astype(BF16)
    kc = _rope(kc_ref[...].astype(F32), cos_c, sin_c).astype(BF16)
    kp = _rope(kp_ref[...].astype(F32), cp_ref[...], sp_ref[...]).astype(BF16)
    kk = jnp.concatenate([kp, kc], axis=0)
    vv = jnp.concatenate([vp_ref[...], vc_ref[...]], axis=0)
    qi = lax.broadcasted_iota(jnp.int32, (WINDOW, 2 * WINDOW), 0)
    kj = lax.broadcasted_iota(jnp.int32, (WINDOW, 2 * WINDOW), 1)
    first = jnp.where(n > 0, 0, WINDOW)
    valid = (kj > qi) & (kj <= qi + WINDOW) & (kj >= first)
    outs = []
    for h in range(hq):
        kvh = h // grp
        qh = q[:, h * HEAD_DIM:(h + 1) * HEAD_DIM]
        kh = kk[:, kvh * HEAD_DIM:(kvh + 1) * HEAD_DIM]
        vh = vv[:, kvh * HEAD_DIM:(kvh + 1) * HEAD_DIM]
        s = lax.dot_general(qh, kh, (((1,), (1,)), ((), ())), preferred_element_type=F32)
        s = jnp.where(valid, s, NEG_BIG)
        sink = sink_ref[h]
        mx = jnp.maximum(jnp.max(s, axis=-1, keepdims=True), sink)
        p = jnp.exp(s - mx)
        den = jnp.sum(p, axis=-1, keepdims=True) + jnp.exp(sink - mx)
        o = jnp.dot(p.astype(BF16), vh, preferred_element_type=F32)
        outs.append(o / den)
    o_ref[...] = jnp.concatenate(outs, axis=1).astype(o_ref.dtype)


def _attention(proj, bsz, seq, cos_t, sin_t, sinks, q_col, k_col, v_col):
    t = proj.shape[0]
    nb = seq // WINDOW
    wq = sinks.shape[0] * HEAD_DIM
    wk = KV_HEADS * HEAD_DIM
    cur = lambda b, n, s: (b * nb + n, 0)
    prv = lambda b, n, s: (b * nb + jnp.maximum(n - 1, 0), 0)
    return pl.pallas_call(
        _attn_kernel, out_shape=jax.ShapeDtypeStruct((t, wq), BF16),
        grid_spec=pltpu.PrefetchScalarGridSpec(
            num_scalar_prefetch=1, grid=(bsz, nb),
            in_specs=[pl.BlockSpec((WINDOW, wq), lambda b, n, s: (b * nb + n, q_col)),
                      pl.BlockSpec((WINDOW, wk), lambda b, n, s: (b * nb + n, k_col)),
                      pl.BlockSpec((WINDOW, wk), lambda b, n, s: (b * nb + jnp.maximum(n - 1, 0), k_col)),
                      pl.BlockSpec((WINDOW, wk), lambda b, n, s: (b * nb + n, v_col)),
                      pl.BlockSpec((WINDOW, wk), lambda b, n, s: (b * nb + jnp.maximum(n - 1, 0), v_col)),
                      pl.BlockSpec((WINDOW, wk), cur), pl.BlockSpec((WINDOW, wk), cur),
                      pl.BlockSpec((WINDOW, wk), prv), pl.BlockSpec((WINDOW, wk), prv)],
            out_specs=pl.BlockSpec((WINDOW, wq), lambda b, n, s: (b * nb + n, 0))),
        compiler_params=_cparams("arbitrary", "arbitrary"), name="swa_attention",
    )(sinks.astype(F32), proj, proj, proj, proj, proj, cos_t, sin_t, cos_t, sin_t)


def _rope_tables(positions):
    half = ROPE_DIM // 2
    inv = 1.0 / (ROPE_THETA ** (jnp.arange(0, ROPE_DIM, 2, dtype=F32) / ROPE_DIM))
    ang = positions.astype(F32).reshape(-1, 1) * inv[None, :]
    cos, sin = jnp.cos(ang), jnp.sin(ang)
    t = ang.shape[0]
    ones = jnp.ones((t, HEAD_DIM - ROPE_DIM), F32)
    cos_h = jnp.concatenate([cos, cos, ones], axis=1)
    sin_h = jnp.concatenate([-sin, sin, 0.0 * ones], axis=1)
    return jnp.tile(cos_h, (1, KV_HEADS)), jnp.tile(sin_h, (1, KV_HEADS))


def _router_kernel(x_ref, g_ref, sh_ref, sc_ref, rw_ref, rb_ref, o_ref, cnt_ref, carry_ref):
    i = pl.program_id(0)
    tm = x_ref.shape[0]
    ne = MOE_EXPERTS

    @pl.when(i == 0)
    def _():
        carry_ref[...] = jnp.zeros_like(carry_ref)

    h = _modnorm(x_ref[...], g_ref[...], sh_ref[0], sc_ref[0])
    lt = lax.dot_general(rw_ref[...], h, (((1,), (1,)), ((), ())),
                         precision=HIGHEST, preferred_element_type=F32) + rb_ref[...]
    gl = lt[0:MOE_GROUPS]
    gm = jnp.max(gl, axis=0, keepdims=True)
    gsum = jnp.sum(jnp.exp(gl - gm), axis=0, keepdims=True)
    g_val = 1.0 / gsum
    grow = lax.broadcasted_iota(jnp.int32, gl.shape, 0)
    g_idx = jnp.min(jnp.where(gl == gm, grow, MOE_GROUPS), axis=0, keepdims=True)
    sel = jnp.zeros((MOE_EPG, tm), F32)
    for gi in range(MOE_GROUPS):
        sel = jnp.where(g_idx == gi, lt[8 + gi * MOE_EPG: 8 + (gi + 1) * MOE_EPG], sel)
    erow = lax.broadcasted_iota(jnp.int32, sel.shape, 0)
    m1 = jnp.max(sel, axis=0, keepdims=True)
    i1 = jnp.min(jnp.where(sel == m1, erow, MOE_EPG), axis=0, keepdims=True)
    sel2 = jnp.where(erow == i1, -jnp.inf, sel)
    m2 = jnp.max(sel2, axis=0, keepdims=True)
    i2 = jnp.min(jnp.where(sel2 == m2, erow, MOE_EPG), axis=0, keepdims=True)
    e21 = jnp.exp(m2 - m1)
    w1 = g_val / (1.0 + e21)
    w2 = g_val * e21 / (1.0 + e21)
    e1 = g_idx * MOE_EPG + i1
    e2 = g_idx * MOE_EPG + i2
    xrow = lax.broadcasted_iota(jnp.int32, (ne, tm), 0)
    oh1 = xrow == e1
    oh2 = xrow == e2
    cmat = jnp.where(oh1 | oh2, 1.0, 0.0)
    ti = lax.broadcasted_iota(jnp.int32, (tm, tm), 0)
    tj = lax.broadcasted_iota(jnp.int32, (tm, tm), 1)
    su = jnp.where(ti < tj, 1.0, 0.0).astype(BF16)
    before = jnp.dot(cmat.astype(BF16), su, preferred_element_type=F32) + carry_ref[...]
    r1 = jnp.sum(jnp.where(oh1, before, 0.0), axis=0, keepdims=True)
    r2 = jnp.sum(jnp.where(oh2, before, 0.0), axis=0, keepdims=True)
    carry = carry_ref[...] + jnp.sum(cmat, axis=1, keepdims=True)
    carry_ref[...] = carry
    cnt_ref[...] = jnp.broadcast_to(carry, cnt_ref.shape)
    o_ref[...] = jnp.concatenate([e1.astype(F32), e2.astype(F32), w1, w2, r1, r2,
                                  jnp.zeros((2, tm), F32)], axis=0)


def _dispatch_kernel(d1_ref, d2_ref, x_ref, g_ref, sh_ref, sc_ref, xs_ref, h_ref, sem):
    i = pl.program_id(0)
    tm = x_ref.shape[0]
    h_ref[...] = _modnorm(x_ref[...], g_ref[...], sh_ref[0], sc_ref[0])

    def issue(r, c):
        t = i * tm + r
        pltpu.make_async_copy(h_ref.at[pl.ds(r, 1)], xs_ref.at[pl.ds(d1_ref[t], 1)], sem).start()
        pltpu.make_async_copy(h_ref.at[pl.ds(r, 1)], xs_ref.at[pl.ds(d2_ref[t], 1)], sem).start()
        return c

    lax.fori_loop(0, tm, issue, 0)
    pltpu.make_async_copy(h_ref, xs_ref.at[pl.ds(0, tm)], sem).wait()
    pltpu.make_async_copy(h_ref, xs_ref.at[pl.ds(0, tm)], sem).wait()


def _expert_kernel(te_ref, ts_ref, nv_ref, xs_ref, wg_ref, wu_ref, wd_ref, ys_ref):
    i = pl.program_id(0)

    @pl.when(i < nv_ref[0])
    def _():
        x = xs_ref[...].astype(BF16)
        hg = jnp.dot(x, wg_ref[0], preferred_element_type=F32)
        hu = jnp.dot(x, wu_ref[0], preferred_element_type=F32)
        hid = (hg * jax.nn.sigmoid(hg) * hu).astype(BF16)
        ys_ref[...] = jnp.dot(hid, wd_ref[0], preferred_element_type=F32)


def _combine_kernel(d1_ref, d2_ref, x_ref, w_ref, gt_ref, fn_ref, ys_ref, o_ref, b1_ref, b2_ref, sem,
                    *, final_norm):
    i = pl.program_id(0)
    tm = x_ref.shape[0]

    def issue(r, c):
        t = i * tm + r
        pltpu.make_async_copy(ys_ref.at[pl.ds(d1_ref[t], 1)], b1_ref.at[pl.ds(r, 1)], sem).start()
        pltpu.make_async_copy(ys_ref.at[pl.ds(d2_ref[t], 1)], b2_ref.at[pl.ds(r, 1)], sem).start()
        return c

    lax.fori_loop(0, tm, issue, 0)
    pltpu.make_async_copy(ys_ref.at[pl.ds(0, tm)], b1_ref, sem).wait()
    pltpu.make_async_copy(ys_ref.at[pl.ds(0, tm)], b2_ref, sem).wait()
    w = w_ref[...]
    y = w[:, 0:1] * b1_ref[...] + w[:, 1:2] * b2_ref[...]
    out = x_ref[...] + gt_ref[0] * y
    if final_norm:
        ms = jnp.mean(out * out, axis=-1, keepdims=True)
        out = out * lax.rsqrt(ms + EPS) * fn_ref[...]
    o_ref[...] = out


def _hier_moe(x2d, seq, g, sh, sc, gate, r1_w, r1_b, r2_w, r2_b, wg, wu, wd, final_w):
    t, d = x2d.shape
    ne = MOE_EXPERTS
    rw = jnp.zeros((128, d), F32).at[0:MOE_GROUPS].set(r1_w.T).at[8:8 + ne].set(r2_w.T)
    rb = jnp.zeros((128, 1), F32).at[0:MOE_GROUPS, 0].set(r1_b).at[8:8 + ne, 0].set(r2_b)
    tmr = min(512, seq)
    bmap = lambda i: ((i * tmr) // seq, 0, 0)
    route, cnt = pl.pallas_call(
        _router_kernel,
        out_shape=(jax.ShapeDtypeStruct((8, t), F32), jax.ShapeDtypeStruct((ne, 128), F32)),
        grid=(t // tmr,),
        in_specs=[pl.BlockSpec((tmr, d), lambda i: (i, 0)),
                  pl.BlockSpec((1, d), lambda i: (0, 0)),
                  pl.BlockSpec((1, 1, d), bmap), pl.BlockSpec((1, 1, d), bmap),
                  pl.BlockSpec((128, d), lambda i: (0, 0)),
                  pl.BlockSpec((128, 1), lambda i: (0, 0))],
        out_specs=(pl.BlockSpec((8, tmr), lambda i: (0, i)),
                   pl.BlockSpec((ne, 128), lambda i: (0, 0))),
        scratch_shapes=[pltpu.VMEM((ne, 1), F32)],
        compiler_params=_cparams("arbitrary"), name="moe_router",
    )(x2d, g.reshape(1, d), sh[:, None, :], sc[:, None, :], rw, rb)

    e1 = route[0].astype(jnp.int32)
    e2 = route[1].astype(jnp.int32)
    tile = EXPERT_TILE
    counts = cnt[:, 0].astype(jnp.int32)
    padded = ((counts + tile - 1) // tile) * tile
    ends = jnp.cumsum(padded)
    offs = ends - padded
    dest1 = offs[e1] + route[4].astype(jnp.int32)
    dest2 = offs[e2] + route[5].astype(jnp.int32)
    nt = (2 * t) // tile + ne
    n_used = ends[-1] // tile
    tid = jnp.arange(nt, dtype=jnp.int32)
    tsrc = jnp.minimum(tid, n_used - 1)
    texp = jnp.minimum(jnp.searchsorted(ends, tsrc * tile, side='right'), ne - 1).astype(jnp.int32)
    wts = jnp.zeros((t, 8), F32).at[:, 0].set(route[2]).at[:, 1].set(route[3])

    tmd = min(256, seq)
    bmapd = lambda i, a, b: ((i * tmd) // seq, 0, 0)
    xs = pl.pallas_call(
        _dispatch_kernel, out_shape=jax.ShapeDtypeStruct((nt * tile, d), F32),
        grid_spec=pltpu.PrefetchScalarGridSpec(
            num_scalar_prefetch=2, grid=(t // tmd,),
            in_specs=[pl.BlockSpec((tmd, d), lambda i, a, b: (i, 0)),
                      pl.BlockSpec((1, d), lambda i, a, b: (0, 0)),
                      pl.BlockSpec((1, 1, d), bmapd), pl.BlockSpec((1, 1, d), bmapd)],
            out_specs=pl.BlockSpec(memory_space=pl.ANY),
            scratch_shapes=[pltpu.VMEM((tmd, d), F32), pltpu.SemaphoreType.DMA(())]),
        compiler_params=_cparams("arbitrary"), name="moe_dispatch",
    )(dest1, dest2, x2d, g.reshape(1, d), sh[:, None, :], sc[:, None, :])

    ff = wg.shape[2]
    ys = pl.pallas_call(
        _expert_kernel, out_shape=jax.ShapeDtypeStruct((nt * tile, d), F32),
        grid_spec=pltpu.PrefetchScalarGridSpec(
            num_scalar_prefetch=3, grid=(nt,),
            in_specs=[pl.BlockSpec((tile, d), lambda i, te, ts, nv: (ts[i], 0)),
                      pl.BlockSpec((1, d, ff), lambda i, te, ts, nv: (te[i], 0, 0)),
                      pl.BlockSpec((1, d, ff), lambda i, te, ts, nv: (te[i], 0, 0)),
                      pl.BlockSpec((1, ff, d), lambda i, te, ts, nv: (te[i], 0, 0))],
            out_specs=pl.BlockSpec((tile, d), lambda i, te, ts, nv: (ts[i], 0))),
        compiler_params=_cparams("arbitrary"), name="moe_experts",
    )(texp, tsrc, n_used.reshape(1).astype(jnp.int32), xs, wg, wu, wd)

    fin = final_w is not None
    fw = (final_w if fin else jnp.ones((d,), F32)).reshape(1, d)
    bmapc = lambda i, a, b: ((i * tmd) // seq, 0, 0)
    return pl.pallas_call(
        functools.partial(_combine_kernel, final_norm=fin),
        out_shape=jax.ShapeDtypeStruct((t, d), F32),
        grid_spec=pltpu.PrefetchScalarGridSpec(
            num_scalar_prefetch=2, grid=(t // tmd,),
            in_specs=[pl.BlockSpec((tmd, d), lambda i, a, b: (i, 0)),
                      pl.BlockSpec((tmd, 8), lambda i, a, b: (i, 0)),
                      pl.BlockSpec((1, 1, d), bmapc),
                      pl.BlockSpec((1, d), lambda i, a, b: (0, 0)),
                      pl.BlockSpec(memory_space=pl.ANY)],
            out_specs=pl.BlockSpec((tmd, d), lambda i, a, b: (i, 0)),
            scratch_shapes=[pltpu.VMEM((tmd, d), F32), pltpu.VMEM((tmd, d), F32),
                            pltpu.SemaphoreType.DMA(())]),
        compiler_params=_cparams("arbitrary"), name="moe_combine",
    )(dest1, dest2, x2d, wts, gate[:, None, :], fw, ys)


def _ssd_kernel(z_ref, x_ref, b_ref, c_ref, dt_ref, cwx_ref, cwb_ref, cwc_ref, cbx_ref, cbb_ref, cbc_ref,
                dtb_ref, alog_ref, dsk_ref, nw_ref, o_ref,
                xe_ref, be_ref, ce_ref, st_ref, acs_ref, dtg_ref):
    c = pl.program_id(1)
    lq = SSD_CHUNK
    ng = SSD_GROUPS
    gw = x_ref.shape[1] // ng
    hpg = gw // SSD_HEAD_DIM
    halo = 8
    k0 = halo - (SSD_CONV - 1)

    @pl.when(c == 0)
    def _():
        xe_ref[0:halo] = jnp.zeros((halo, xe_ref.shape[1]), F32)
        be_ref[0:halo] = jnp.zeros((halo, be_ref.shape[1]), F32)
        ce_ref[0:halo] = jnp.zeros((halo, ce_ref.shape[1]), F32)
        st_ref[...] = jnp.zeros_like(st_ref)

    xe_ref[halo:halo + lq] = x_ref[...].astype(F32)
    be_ref[halo:halo + lq] = b_ref[...].astype(F32)
    ce_ref[halo:halo + lq] = c_ref[...].astype(F32)

    def conv_silu(e_ref, w_ref, bias_ref, cols):
        acc = bias_ref[:, cols]
        for k in range(SSD_CONV):
            acc = acc + e_ref[k0 + k:k0 + k + lq, cols] * w_ref[k:k + 1, cols]
        return acc * jax.nn.sigmoid(acc)

    dt = jax.nn.softplus(dt_ref[...] + dtb_ref[...])
    adt = dt * (-jnp.exp(alog_ref[...]))
    li = lax.broadcasted_iota(jnp.int32, (lq, lq), 0)
    si = lax.broadcasted_iota(jnp.int32, (lq, lq), 1)
    tri = jnp.where(si <= li, 1.0, 0.0)
    acs = jnp.dot(tri, adt, precision=HIGHEST, preferred_element_type=F32)
    lane = lax.broadcasted_iota(jnp.int32, (lq, 128), 1)
    for gi in range(ng):
        sh = (128 - gi * hpg) % 128
        acs_ref[gi] = jnp.where(lane < hpg, pltpu.roll(acs, sh, axis=1) if sh else acs, 0.0)
        dtg_ref[gi] = jnp.where(lane < hpg, pltpu.roll(dt, sh, axis=1) if sh else dt, 0.0)
    causal = si <= li
    lane_lo = lane < SSD_HEAD_DIM

    def group_body(gi, carry):
        xc0 = pl.multiple_of(gi * gw, gw)
        nc0 = pl.multiple_of(gi * SSD_STATE, SSD_STATE)
        xg = conv_silu(xe_ref, cwx_ref, cbx_ref, pl.ds(xc0, gw))
        bg = conv_silu(be_ref, cwb_ref, cbb_ref, pl.ds(nc0, SSD_STATE))
        cg = conv_silu(ce_ref, cwc_ref, cbc_ref, pl.ds(nc0, SSD_STATE))
        bt = bg.T
        cgb = cg.astype(BF16)
        cb = jnp.dot(cgb, bt.astype(BF16), preferred_element_type=F32)
        acs_g = acs_ref[gi]
        acs_t = acs_g.T
        dt_t = dtg_ref[gi].T
        xgb = xg.astype(BF16)
        ys = []
        for jp in range(hpg // 2):
            lhs_parts, w_parts, ecols, dlast = [], [], [], []
            for j in (2 * jp, 2 * jp + 1):
                e_col = jnp.broadcast_to(acs_g[:, j:j + 1], (lq, lq))
                r_row = jnp.broadcast_to(acs_t[j:j + 1, :], (lq, lq))
                dt_row = dt_t[j:j + 1, :]
                lmat = jnp.where(causal, jnp.exp(e_col - r_row), 0.0) * cb * dt_row
                lhs_parts.append(lmat.astype(BF16))
                a_last = acs_t[j:j + 1, lq - 1:lq]
                wrow = dt_row * jnp.exp(a_last - acs_t[j:j + 1, :])
                w_parts.append((bt * wrow).astype(BF16))
                ecols.append(e_col)
                dlast.append(jnp.exp(a_last))
            lhs = jnp.concatenate([jnp.concatenate(lhs_parts, axis=1),
                                   jnp.concatenate(w_parts, axis=1)], axis=0)
            xp = xgb[:, jp * 128:(jp + 1) * 128]
            zero = jnp.zeros_like(xp)
            rhs = jnp.concatenate([jnp.where(lane_lo, xp, zero), jnp.where(lane_lo, zero, xp)], axis=0)
            res = jnp.dot(lhs, rhs, preferred_element_type=F32)
            st = st_ref[gi, :, jp * 128:(jp + 1) * 128]
            yoff = jnp.dot(cgb, st.astype(BF16), preferred_element_type=F32)
            escale = jnp.exp(jnp.where(lane_lo, ecols[0], ecols[1]))
            ys.append(res[:lq] + yoff * escale)
            dl = jnp.where(lane_lo[:1], dlast[0], dlast[1])
            st_ref[gi, :, jp * 128:(jp + 1) * 128] = st * dl + res[lq:]
        y = jnp.concatenate(ys, axis=1) + dsk_ref[:, pl.ds(xc0, gw)] * xg
        zg = z_ref[:, pl.ds(xc0, gw)].astype(F32)
        y = y * (zg * jax.nn.sigmoid(zg))
        ms = jnp.mean(y * y, axis=-1, keepdims=True)
        y = y * lax.rsqrt(ms + EPS) * nw_ref[:, pl.ds(xc0, gw)]
        o_ref[:, pl.ds(xc0, gw)] = y.astype(o_ref.dtype)
        return carry

    lax.fori_loop(0, ng, group_body, 0)
    xe_ref[0:halo] = xe_ref[lq:lq + halo]
    be_ref[0:halo] = be_ref[lq:lq + halo]
    ce_ref[0:halo] = ce_ref[lq:lq + halo]


def _ssd_mixer(zx, dt_raw, bsz, seq, conv_w, conv_b, dt_bias, a_log, d_skip, norm_w):
    t = zx.shape[0]
    heads = a_log.shape[0]
    inner = heads * SSD_HEAD_DIM
    bcw = SSD_GROUPS * SSD_STATE
    lq = SSD_CHUNK
    nc = seq // lq
    pad = 128 - heads
    cw = conv_w.astype(F32)
    cwp = jnp.zeros((8, cw.shape[1]), F32).at[:SSD_CONV].set(cw)
    cb = conv_b.astype(F32).reshape(1, -1)
    row = lambda b, c: (b * nc + c, 0)
    const = lambda b, c: (0, 0)
    xb = inner // inner
    bb = (2 * inner) // bcw
    return pl.pallas_call(
        _ssd_kernel, out_shape=jax.ShapeDtypeStruct((t, inner), BF16), grid=(bsz, nc),
        in_specs=[pl.BlockSpec((lq, inner), lambda b, c: (b * nc + c, 0)),
                  pl.BlockSpec((lq, inner), lambda b, c: (b * nc + c, xb)),
                  pl.BlockSpec((lq, bcw), lambda b, c: (b * nc + c, bb)),
                  pl.BlockSpec((lq, bcw), lambda b, c: (b * nc + c, bb + 1)),
                  pl.BlockSpec((lq, 128), row),
                  pl.BlockSpec((8, inner), const), pl.BlockSpec((8, bcw), const), pl.BlockSpec((8, bcw), const),
                  pl.BlockSpec((1, inner), const), pl.BlockSpec((1, bcw), const), pl.BlockSpec((1, bcw), const),
                  pl.BlockSpec((1, 128), const), pl.BlockSpec((1, 128), const),
                  pl.BlockSpec((1, inner), const), pl.BlockSpec((1, inner), const)],
        out_specs=pl.BlockSpec((lq, inner), row),
        scratch_shapes=[pltpu.VMEM((lq + 8, inner), F32), pltpu.VMEM((lq + 8, bcw), F32),
                        pltpu.VMEM((lq + 8, bcw), F32),
                        pltpu.VMEM((SSD_GROUPS, SSD_STATE, inner // SSD_GROUPS), F32),
                        pltpu.VMEM((SSD_GROUPS, lq, 128), F32), pltpu.VMEM((SSD_GROUPS, lq, 128), F32)],
        compiler_params=_cparams("arbitrary", "arbitrary"), name="ssd_mixer",
    )(zx, zx, zx, zx, dt_raw,
      cwp[:, :inner], cwp[:, inner:inner + bcw], cwp[:, inner + bcw:],
      cb[:, :inner], cb[:, inner:inner + bcw], cb[:, inner + bcw:],
      jnp.pad(dt_bias.astype(F32), (0, pad)).reshape(1, 128),
      jnp.pad(a_log.astype(F32), (0, pad)).reshape(1, 128),
      jnp.repeat(d_skip.astype(F32), SSD_HEAD_DIM).reshape(1, inner),
      norm_w.astype(F32).reshape(1, inner))


def kernel(x, c, positions, ada_w, ada_b, norm_mix, norm_ffn, mix_a_w_in, s5_lam_re, s5_lam_im, s5_log_dt, s5_b_re, s5_b_im, s5_c_re, s5_c_im, s5_d, s5_w_glu, attn_sinks, mix_a_w_out, ssd_w_in, ssd_conv_w, ssd_conv_b, ssd_dt_bias, ssd_a_log, ssd_d, ssd_norm_w, ssd_w_out, moe_r1_w, moe_r1_b, moe_r2_w, moe_r2_b, moe_w_gate, moe_w_up, moe_w_down, final_norm):
    bsz, seq, d = x.shape
    depth = ada_w.shape[0]
    t = bsz * seq
    xf = x.reshape(t, d).astype(F32)
    mod = _adaln(c.astype(F32), ada_w, ada_b)
    cos_t, sin_t = _rope_tables(positions)

    for i in range(depth):
        sh1, sc1, g1, sh2, sc2, g2 = jnp.split(mod[i], 6, axis=-1)
        j = i // 2
        if i % 2 == 0:
            s5w = s5_lam_re.shape[1] * S5_GROUP
            proj = _norm_mod_matmul(xf, seq, norm_mix[i], sh1, sc1, mix_a_w_in[j].astype(BF16),
                                    tm=1024, tn=768)
            params = _s5_params(s5_lam_re[j], s5_lam_im[j], s5_log_dt[j], s5_b_re[j], s5_b_im[j],
                                s5_c_re[j], s5_c_im[j], s5_d[j])
            y_s5 = _s5_mixer(proj[:, :s5w], bsz, seq, params)
            wglu = s5_w_glu[j].astype(BF16)
            y_glu = _gelu_glu(y_s5, wglu[:, :s5w], wglu[:, s5w:], tm=1024, tn=512)
            wq = attn_sinks.shape[1] * HEAD_DIM
            wk = KV_HEADS * HEAD_DIM
            y_att = _attention(proj, bsz, seq, cos_t, sin_t, attn_sinks[j],
                               q_col=s5w // wq, k_col=(s5w + wq) // wk, v_col=(s5w + wq) // wk + 1)
            wo = mix_a_w_out[j].astype(BF16)
            xf = _matmul_residual([y_glu, y_att], [wo[:s5w], wo[s5w:]], xf, seq, g1, tm=1024, tn=1024)
        else:
            heads = ssd_a_log.shape[1]
            inner = heads * SSD_HEAD_DIM
            main = inner + inner + 2 * SSD_GROUPS * SSD_STATE
            w_in = ssd_w_in[j]
            w_dt = jnp.pad(w_in[:, main:], ((0, 0), (0, 128 - heads))).astype(BF16)
            zx, dt_raw = _norm_mod_matmul(xf, seq, norm_mix[i], sh1, sc1, w_in[:, :main].astype(BF16),
                                          tm=1024, tn=512, w2=w_dt)
            y = _ssd_mixer(zx, dt_raw, bsz, seq, ssd_conv_w[j], ssd_conv_b[j], ssd_dt_bias[j],
                           ssd_a_log[j], ssd_d[j], ssd_norm_w[j])
            xf = _matmul_residual([y], [ssd_w_out[j].astype(BF16)], xf, seq, g1, tm=512, tn=1024)
        xf = _hier_moe(xf, seq, norm_ffn[i], sh2, sc2, g2, moe_r1_w[i], moe_r1_b[i], moe_r2_w[i], moe_r2_b[i],
                       moe_w_gate[i].astype(BF16), moe_w_up[i].astype(BF16), moe_w_down[i].astype(BF16),
                       final_norm if i == depth - 1 else None)
    return xf.reshape(bsz, seq, d).astype(x.dtype)
```

```python
import functools
import math

import jax
import jax.numpy as jnp
from jax import lax
from jax.experimental import pallas as pl
from jax.experimental.pallas import tpu as pltpu

F32 = jnp.float32
BF16 = jnp.bfloat16
HIGHEST = lax.Precision.HIGHEST

EPS = 1e-6
S5_GROUP = 16
S5_STATE = 64
S5_CHUNK = 16
HEAD_DIM = 64
KV_HEADS = 2
WINDOW = 128
ROPE_DIM = 16
ROPE_THETA = 500000.0
SSD_HEAD_DIM = 64
SSD_GROUPS = 8
SSD_STATE = 128
SSD_CONV = 4
SSD_CHUNK = 128
MOE_GROUPS = 4
MOE_EPG = 8
MOE_EXPERTS = MOE_GROUPS * MOE_EPG
EXPERT_TILE = 256
NEG_BIG = -1e30
VMEM_LIMIT = 56 * 1024 * 1024


def _cparams(*sem):
    return pltpu.CompilerParams(dimension_semantics=sem, vmem_limit_bytes=VMEM_LIMIT)


def _modnorm(x, g, sh, sc):
    ms = jnp.mean(x * x, axis=-1, keepdims=True)
    return (x * lax.rsqrt(ms + EPS) * g) * (1.0 + sc) + sh


def _adaln_kernel(c_ref, w_ref, b_ref, o_ref):
    c = c_ref[...]
    a = c * jax.nn.sigmoid(c)
    o_ref[0] = jnp.dot(a, w_ref[0], precision=HIGHEST, preferred_element_type=F32) + b_ref[0]


def _adaln(c, ada_w, ada_b):
    depth, d, n = ada_w.shape
    bsz = c.shape[0]
    rows = 8
    cp = jnp.zeros((rows, d), F32).at[:bsz].set(c)
    tn = 1024
    out = pl.pallas_call(
        _adaln_kernel,
        out_shape=jax.ShapeDtypeStruct((depth, rows, n), F32),
        grid=(depth, n // tn),
        in_specs=[pl.BlockSpec((rows, d), lambda l, j: (0, 0)),
                  pl.BlockSpec((1, d, tn), lambda l, j: (l, 0, j)),
                  pl.BlockSpec((1, 1, tn), lambda l, j: (l, 0, j))],
        out_specs=pl.BlockSpec((1, rows, tn), lambda l, j: (l, 0, j)),
        compiler_params=_cparams("arbitrary", "arbitrary"),
        name="adaln",
    )(cp, ada_w, ada_b.reshape(depth, 1, n))
    return out[:, :bsz]


def _nmm_kernel(x_ref, g_ref, sh_ref, sc_ref, w_ref, o_ref, h_ref):
    @pl.when(pl.program_id(1) == 0)
    def _():
        h_ref[...] = _modnorm(x_ref[...], g_ref[...], sh_ref[0], sc_ref[0]).astype(BF16)

    o_ref[...] = jnp.dot(h_ref[...], w_ref[...], preferred_element_type=F32).astype(o_ref.dtype)


def _nmm2_kernel(x_ref, g_ref, sh_ref, sc_ref, w_ref, w2_ref, o_ref, o2_ref, h_ref):
    @pl.when(pl.program_id(1) == 0)
    def _():
        h = _modnorm(x_ref[...], g_ref[...], sh_ref[0], sc_ref[0]).astype(BF16)
        h_ref[...] = h
        o2_ref[...] = jnp.dot(h, w2_ref[...], preferred_element_type=F32)

    o_ref[...] = jnp.dot(h_ref[...], w_ref[...], preferred_element_type=F32).astype(o_ref.dtype)


def _norm_mod_matmul(x2d, seq, g, sh, sc, w, tm, tn, w2=None):
    t, d = x2d.shape
    n = w.shape[1]
    tm = min(tm, seq)
    bmap = lambda i, j: ((i * tm) // seq, 0, 0)
    in_specs = [pl.BlockSpec((tm, d), lambda i, j: (i, 0)),
                pl.BlockSpec((1, d), lambda i, j: (0, 0)),
                pl.BlockSpec((1, 1, d), bmap),
                pl.BlockSpec((1, 1, d), bmap),
                pl.BlockSpec((d, tn), lambda i, j: (0, j))]
    args = [x2d, g.reshape(1, d), sh[:, None, :], sc[:, None, :], w]
    out_shape = jax.ShapeDtypeStruct((t, n), BF16)
    out_specs = pl.BlockSpec((tm, tn), lambda i, j: (i, j))
    kern = _nmm_kernel
    if w2 is not None:
        n2 = w2.shape[1]
        in_specs.append(pl.BlockSpec((d, n2), lambda i, j: (0, 0)))
        args.append(w2)
        out_shape = (out_shape, jax.ShapeDtypeStruct((t, n2), F32))
        out_specs = (out_specs, pl.BlockSpec((tm, n2), lambda i, j: (i, 0)))
        kern = _nmm2_kernel
    return pl.pallas_call(
        kern, out_shape=out_shape, grid=(t // tm, n // tn),
        in_specs=in_specs, out_specs=out_specs,
        scratch_shapes=[pltpu.VMEM((tm, d), BF16)],
        compiler_params=_cparams("arbitrary", "arbitrary"),
        name="norm_mod_matmul",
    )(*args)


def _mmres1_kernel(a_ref, w_ref, x_ref, gt_ref, o_ref):
    acc = jnp.dot(a_ref[...], w_ref[...], preferred_element_type=F32)
    o_ref[...] = x_ref[...] + gt_ref[0] * acc


def _mmres2_kernel(a_ref, a2_ref, w_ref, w2_ref, x_ref, gt_ref, o_ref):
    acc = jnp.dot(a_ref[...], w_ref[...], preferred_element_type=F32)
    acc = acc + jnp.dot(a2_ref[...], w2_ref[...], preferred_element_type=F32)
    o_ref[...] = x_ref[...] + gt_ref[0] * acc


def _matmul_residual(a_list, w_list, x2d, seq, gate, tm, tn):
    t, n = x2d.shape
    tm = min(tm, seq)
    in_specs, args = [], []
    for a in a_list:
        in_specs.append(pl.BlockSpec((tm, a.shape[1]), lambda i, j: (i, 0)))
        args.append(a)
    for w in w_list:
        in_specs.append(pl.BlockSpec((w.shape[0], tn), lambda i, j: (0, j)))
        args.append(w)
    in_specs.append(pl.BlockSpec((tm, tn), lambda i, j: (i, j)))
    in_specs.append(pl.BlockSpec((1, 1, tn), lambda i, j: ((i * tm) // seq, 0, j)))
    args += [x2d, gate[:, None, :]]
    kern = _mmres1_kernel if len(a_list) == 1 else _mmres2_kernel
    return pl.pallas_call(
        kern, out_shape=jax.ShapeDtypeStruct((t, n), F32), grid=(t // tm, n // tn),
        in_specs=in_specs, out_specs=pl.BlockSpec((tm, tn), lambda i, j: (i, j)),
        compiler_params=_cparams("arbitrary", "arbitrary"),
        name="matmul_residual",
    )(*args)


def _inproj0_kernel(x_ref, g_ref, sh_ref, sc_ref, wu_ref, wa_ref, u_ref, a_ref):
    h = _modnorm(x_ref[...], g_ref[...], sh_ref[0], sc_ref[0]).astype(BF16)
    u_ref[...] = jnp.dot(h, wu_ref[...], preferred_element_type=F32)
    a_ref[...] = jnp.dot(h, wa_ref[...], preferred_element_type=F32).astype(a_ref.dtype)


def _inproj0(x2d, seq, g, sh, sc, w_u, w_a, tm):
    t, d = x2d.shape
    tm = min(tm, seq)
    nu, na = w_u.shape[1], w_a.shape[1]
    bmap = lambda i: ((i * tm) // seq, 0, 0)
    return pl.pallas_call(
        _inproj0_kernel,
        out_shape=(jax.ShapeDtypeStruct((t, nu), F32), jax.ShapeDtypeStruct((t, na), BF16)),
        grid=(t // tm,),
        in_specs=[pl.BlockSpec((tm, d), lambda i: (i, 0)),
                  pl.BlockSpec((1, d), lambda i: (0, 0)),
                  pl.BlockSpec((1, 1, d), bmap), pl.BlockSpec((1, 1, d), bmap),
                  pl.BlockSpec((d, nu), lambda i: (0, 0)),
                  pl.BlockSpec((d, na), lambda i: (0, 0))],
        out_specs=(pl.BlockSpec((tm, nu), lambda i: (i, 0)), pl.BlockSpec((tm, na), lambda i: (i, 0))),
        compiler_params=_cparams("arbitrary"), name="inproj0",
    )(x2d, g.reshape(1, d), sh[:, None, :], sc[:, None, :], w_u, w_a)


S5_SLAB = 8
S5_SEGS = 8


def _s5_params(lam_re, lam_im, log_dt, b_re, b_im, c_re, c_im, d_skip, seq):
    g, n = lam_re.shape
    p = S5_GROUP
    lc = S5_CHUNK
    gs = S5_SLAB
    nj = g // gs
    lam = lax.complex(lam_re.astype(F32), lam_im.astype(F32))
    dt = jnp.exp(log_dt.astype(F32))[:, None]
    ld = lam * dt
    lam_bar = jnp.exp(ld)
    bmat = lax.complex(b_re.astype(F32), b_im.astype(F32))
    b_bar = ((lam_bar - 1.0) / lam)[..., None] * bmat
    cmat = lax.complex(c_re.astype(F32), c_im.astype(F32))
    ks = jnp.arange(lc, dtype=F32)
    pw = jnp.exp(ld[:, None, :] * ks[None, :, None])
    pw1 = jnp.exp(ld[:, None, :] * (ks[None, :, None] + 1.0))
    kk = jnp.real(jnp.einsum('gpn,gkn,gnq->gkpq', cmat, pw, b_bar, precision=HIGHEST))
    li = jnp.arange(lc)
    lag = li[None, :] - li[:, None]
    m = kk[:, jnp.clip(lag, 0, lc - 1)]
    m = jnp.where((lag >= 0)[None, :, :, None, None], m, 0.0)
    m = m.transpose(0, 1, 4, 2, 3)
    eye = (jnp.eye(lc)[:, None, :, None] * jnp.eye(p)[None, :, None, :])
    m = m + eye[None] * d_skip.astype(F32).reshape(g, 1, 1, 1, p)
    eg = jnp.eye(gs, dtype=F32)
    m_slab = jnp.einsum('jasqlp,ab->jsaqlbp', m.reshape(nj, gs, lc, p, lc, p), eg)
    m_slab = m_slab.reshape(nj, lc * gs * p, lc * gs * p).astype(BF16)
    wb = (pw[:, ::-1, :, None] * b_bar[:, None, :, :]).transpose(0, 1, 3, 2)
    wb = wb.reshape(nj, gs, lc, p, n)
    wb_slab = jnp.concatenate([jnp.einsum('jasqn,ab->jsaqbn', part, eg).reshape(nj, lc * gs * p, gs * n)
                               for part in (jnp.real(wb), jnp.imag(wb))], axis=-1).astype(BF16)
    cl = (cmat[:, None, :, :] * pw1[:, :, None, :]).transpose(0, 3, 1, 2)
    cl = cl.reshape(nj, gs, n, lc, p)
    woff_slab = jnp.concatenate([jnp.einsum('janlp,ab->janlbp', part, eg).reshape(nj, gs * n, lc * gs * p)
                                 for part in (jnp.real(cl), -jnp.imag(cl))], axis=1).astype(BF16)
    seglen = (seq // lc) // S5_SEGS
    kq = jnp.arange(seglen + 1, dtype=F32) * float(lc)
    pq = jnp.exp(ld[:, None, :] * kq[None, :, None]).reshape(nj, gs, seglen + 1, n)
    pq = pq.transpose(0, 2, 1, 3).reshape(nj, seglen + 1, gs * n)
    return m_slab, wb_slab, woff_slab, jnp.real(pq), jnp.imag(pq)


def _s5_kernel(u_ref, m_ref, wb_ref, woff_ref, pre_ref, pim_ref, y_ref, sloc_ref, sin_ref):
    lc = S5_CHUNK
    nc = u_ref.shape[0] // lc
    hw = wb_ref.shape[2] // 2
    seglen = nc // S5_SEGS
    u_all = jnp.concatenate([u_ref[pl.ds(s, nc, stride=lc), :] for s in range(lc)], axis=1).astype(BF16)
    w = u_ref.shape[1]
    nq = hw // w
    sloc = jnp.dot(u_all, wb_ref[0], preferred_element_type=F32)
    for q in range(2 * nq):
        sloc_ref[q] = sloc[:, q * w:(q + 1) * w]
    a_re, a_im = pre_ref[0, 1:2], pim_ref[0, 1:2]

    def put(rows, re, im):
        for q in range(nq):
            sin_ref[q, rows, :] = re[:, q * w:(q + 1) * w]
            sin_ref[nq + q, rows, :] = im[:, q * w:(q + 1) * w]

    def get(ref, rows):
        return (jnp.concatenate([ref[q, rows, :] for q in range(nq)], axis=1),
                jnp.concatenate([ref[nq + q, rows, :] for q in range(nq)], axis=1))

    def seg_step(k, st):
        sre, sim = st
        rows = pl.ds(k, S5_SEGS, stride=seglen)
        put(rows, sre, sim)
        lre, lim = get(sloc_ref, rows)
        return a_re * sre - a_im * sim + lre, a_re * sim + a_im * sre + lim

    z = jnp.zeros((S5_SEGS, hw), F32)
    ere, eim = lax.fori_loop(0, seglen, seg_step, (z, z))
    g_re, g_im = pre_ref[0, seglen:seglen + 1], pim_ref[0, seglen:seglen + 1]
    cre, cim = jnp.zeros((1, hw), F32), jnp.zeros((1, hw), F32)
    pre, pim = pre_ref[0, 0:seglen], pim_ref[0, 0:seglen]
    for sg in range(1, S5_SEGS):
        cre, cim = (g_re * cre - g_im * cim + ere[sg - 1:sg], g_re * cim + g_im * cre + eim[sg - 1:sg])
        rows = slice(sg * seglen, (sg + 1) * seglen)
        cur_re, cur_im = get(sin_ref, rows)
        put(rows, cur_re + pre * cre - pim * cim, cur_im + pre * cim + pim * cre)
    s_in = jnp.concatenate([sin_ref[q] for q in range(2 * nq)], axis=1).astype(BF16)
    y_all = (jnp.dot(u_all, m_ref[0], preferred_element_type=F32)
             + jnp.dot(s_in, woff_ref[0], preferred_element_type=F32))
    for s in range(lc):
        y_ref[pl.ds(s, nc, stride=lc), :] = y_all[:, s * w:(s + 1) * w]


def _s5_mixer(u, bsz, seq, params):
    m_slab, wb_slab, woff_slab, pq_re, pq_im = params
    nj, km, _ = m_slab.shape
    sw = wb_slab.shape[2]
    w = km // S5_CHUNK
    nc = seq // S5_CHUNK
    t = u.shape[0]
    const = lambda j, b: (j, 0, 0)
    return pl.pallas_call(
        _s5_kernel, out_shape=jax.ShapeDtypeStruct((t, nj * w), F32), grid=(nj, bsz),
        in_specs=[pl.BlockSpec((seq, w), lambda j, b: (b, j)),
                  pl.BlockSpec((1, km, km), const),
                  pl.BlockSpec((1, km, sw), const),
                  pl.BlockSpec((1, sw, km), const),
                  pl.BlockSpec((1,) + pq_re.shape[1:], const),
                  pl.BlockSpec((1,) + pq_im.shape[1:], const)],
        out_specs=pl.BlockSpec((seq, w), lambda j, b: (b, j)),
        scratch_shapes=[pltpu.VMEM((sw // w, nc, w), F32), pltpu.VMEM((sw // w, nc, w), F32)],
        compiler_params=_cparams("arbitrary", "arbitrary"), name="s5_mixer",
    )(u, m_slab, wb_slab, woff_slab, pq_re, pq_im)


def _glu_kernel(y_ref, wv_ref, wg_ref, o_ref, a_ref):
    @pl.when(pl.program_id(1) == 0)
    def _():
        a_ref[...] = jax.nn.gelu(y_ref[...].astype(F32), approximate=True).astype(BF16)

    a = a_ref[...]
    val = jnp.dot(a, wv_ref[...], preferred_element_type=F32)
    gate = jnp.dot(a, wg_ref[...], preferred_element_type=F32)
    o_ref[...] = (val * jax.nn.sigmoid(gate)).astype(o_ref.dtype)


def _gelu_glu(y, wv, wg, tm, tn):
    t, k = y.shape
    n = wv.shape[1]
    tm = min(tm, t)
    return pl.pallas_call(
        _glu_kernel, out_shape=jax.ShapeDtypeStruct((t, n), BF16), grid=(t // tm, n // tn),
        in_specs=[pl.BlockSpec((tm, k), lambda i, j: (i, 0)),
                  pl.BlockSpec((k, tn), lambda i, j: (0, j)),
                  pl.BlockSpec((k, tn), lambda i, j: (0, j))],
        out_specs=pl.BlockSpec((tm, tn), lambda i, j: (i, j)),
        scratch_shapes=[pltpu.VMEM((tm, k), BF16)],
        compiler_params=_cparams("arbitrary", "arbitrary"), name="gelu_glu",
    )(y, wv, wg)


def _rope(x, cos, sin):
    w = x.shape[1]
    half = ROPE_DIM // 2
    d = lax.broadcasted_iota(jnp.int32, x.shape, 1) % HEAD_DIM
    partner = jnp.where(d < half, pltpu.roll(x, w - half, axis=1), pltpu.roll(x, half, axis=1))
    return x * cos + partner * sin


def _attn_kernel(sink_ref, q_ref, kc_ref, kp_ref, vc_ref, vp_ref,
                 cq_ref, sq_ref, cp_ref, sp_ref, o_ref):
    n = pl.program_id(1)
    wq = q_ref.shape[1]
    hq = wq // HEAD_DIM
    grp = hq // KV_HEADS
    cos_c, sin_c = cq_ref[...], sq_ref[...]
    reps = wq // cos_c.shape[1]
    q = _rope(q_ref[...].astype(F32),
              jnp.concatenate([cos_c] * reps, axis=1), jnp.concatenate([sin_c] * reps, axis=1))
    q = (q * (HEAD_DIM ** -0.5)).astype(BF16)
    kc = _rope(kc_ref[...].astype(F32), cos_c, sin_c).astype(BF16)
    kp = _rope(kp_ref[...].astype(F32), cp_ref[...], sp_ref[...]).astype(BF16)
    kk = jnp.concatenate([kp, kc], axis=0)
    vv = jnp.concatenate([vp_ref[...], vc_ref[...]], axis=0)
    qi = lax.broadcasted_iota(jnp.int32, (WINDOW, 2 * WINDOW), 0)
    kj = lax.broadcasted_iota(jnp.int32, (WINDOW, 2 * WINDOW), 1)
    first = jnp.where(n > 0, 0, WINDOW)
    valid = (kj > qi) & (kj <= qi + WINDOW) & (kj >= first)
    outs = []
    for h in range(hq):
        kvh = h // grp
        qh = q[:, h * HEAD_DIM:(h + 1) * HEAD_DIM]
        kh = kk[:, kvh * HEAD_DIM:(kvh + 1) * HEAD_DIM]
        vh = vv[:, kvh * HEAD_DIM:(kvh + 1) * HEAD_DIM]
        s = lax.dot_general(qh, kh, (((1,), (1,)), ((), ())), preferred_element_type=F32)
        s = jnp.where(valid, s, NEG_BIG)
        sink = sink_ref[h]
        mx = jnp.maximum(jnp.max(s, axis=-1, keepdims=True), sink)
        p = jnp.exp(s - mx)
        den = jnp.sum(p, axis=-1, keepdims=True) + jnp.exp(sink - mx)
        o = jnp.dot(p.astype(BF16), vh, preferred_element_type=F32)
        outs.append(o / den)
    o_ref[...] = jnp.concatenate(outs, axis=1).astype(o_ref.dtype)


def _attention(proj, bsz, seq, cos_t, sin_t, sinks, q_col, k_col, v_col):
    t = proj.shape[0]
    nb = seq // WINDOW
    wq = sinks.shape[0] * HEAD_DIM
    wk = KV_HEADS * HEAD_DIM
    cur = lambda b, n, s: (b * nb + n, 0)
    prv = lambda b, n, s: (b * nb + jnp.maximum(n - 1, 0), 0)
    return pl.pallas_call(
        _attn_kernel, out_shape=jax.ShapeDtypeStruct((t, wq), BF16),
        grid_spec=pltpu.PrefetchScalarGridSpec(
            num_scalar_prefetch=1, grid=(bsz, nb),
            in_specs=[pl.BlockSpec((WINDOW, wq), lambda b, n, s: (b * nb + n, q_col)),
                      pl.BlockSpec((WINDOW, wk), lambda b, n, s: (b * nb + n, k_col)),
                      pl.BlockSpec((WINDOW, wk), lambda b, n, s: (b * nb + jnp.maximum(n - 1, 0), k_col)),
                      pl.BlockSpec((WINDOW, wk), lambda b, n, s: (b * nb + n, v_col)),
                      pl.BlockSpec((WINDOW, wk), lambda b, n, s: (b * nb + jnp.maximum(n - 1, 0), v_col)),
                      pl.BlockSpec((WINDOW, wk), cur), pl.BlockSpec((WINDOW, wk), cur),
                      pl.BlockSpec((WINDOW, wk), prv), pl.BlockSpec((WINDOW, wk), prv)],
            out_specs=pl.BlockSpec((WINDOW, wq), lambda b, n, s: (b * nb + n, 0))),
        compiler_params=_cparams("arbitrary", "arbitrary"), name="swa_attention",
    )(sinks.astype(F32), proj, proj, proj, proj, proj, cos_t, sin_t, cos_t, sin_t)


def _rope_tables(positions):
    half = ROPE_DIM // 2
    inv = 1.0 / (ROPE_THETA ** (jnp.arange(0, ROPE_DIM, 2, dtype=F32) / ROPE_DIM))
    ang = positions.astype(F32).reshape(-1, 1) * inv[None, :]
    cos, sin = jnp.cos(ang), jnp.sin(ang)
    t = ang.shape[0]
    ones = jnp.ones((t, HEAD_DIM - ROPE_DIM), F32)
    cos_h = jnp.concatenate([cos, cos, ones], axis=1)
    sin_h = jnp.concatenate([-sin, sin, 0.0 * ones], axis=1)
    return jnp.tile(cos_h, (1, KV_HEADS)), jnp.tile(sin_h, (1, KV_HEADS))


def _router_kernel(x_ref, g_ref, sh_ref, sc_ref, rw_ref, rb_ref, o_ref, cnt_ref, carry_ref):
    i = pl.program_id(0)
    tm = x_ref.shape[0]
    ne = MOE_EXPERTS

    @pl.when(i == 0)
    def _():
        carry_ref[...] = jnp.zeros_like(carry_ref)

    h = _modnorm(x_ref[...], g_ref[...], sh_ref[0], sc_ref[0])
    lt = lax.dot_general(rw_ref[...], h, (((1,), (1,)), ((), ())),
                         precision=HIGHEST, preferred_element_type=F32) + rb_ref[...]
    gl = lt[0:MOE_GROUPS]
    gm = jnp.max(gl, axis=0, keepdims=True)
    gsum = jnp.sum(jnp.exp(gl - gm), axis=0, keepdims=True)
    g_val = 1.0 / gsum
    grow = lax.broadcasted_iota(jnp.int32, gl.shape, 0)
    g_idx = jnp.min(jnp.where(gl == gm, grow, MOE_GROUPS), axis=0, keepdims=True)
    sel = jnp.zeros((MOE_EPG, tm), F32)
    for gi in range(MOE_GROUPS):
        sel = jnp.where(g_idx == gi, lt[8 + gi * MOE_EPG: 8 + (gi + 1) * MOE_EPG], sel)
    erow = lax.broadcasted_iota(jnp.int32, sel.shape, 0)
    m1 = jnp.max(sel, axis=0, keepdims=True)
    i1 = jnp.min(jnp.where(sel == m1, erow, MOE_EPG), axis=0, keepdims=True)
    sel2 = jnp.where(erow == i1, -jnp.inf, sel)
    m2 = jnp.max(sel2, axis=0, keepdims=True)
    i2 = jnp.min(jnp.where(sel2 == m2, erow, MOE_EPG), axis=0, keepdims=True)
    e21 = jnp.exp(m2 - m1)
    w1 = g_val / (1.0 + e21)
    w2 = g_val * e21 / (1.0 + e21)
    e1 = g_idx * MOE_EPG + i1
    e2 = g_idx * MOE_EPG + i2
    xrow = lax.broadcasted_iota(jnp.int32, (ne, tm), 0)
    oh1 = xrow == e1
    oh2 = xrow == e2
    cmat = jnp.where(oh1 | oh2, 1.0, 0.0)
    ti = lax.broadcasted_iota(jnp.int32, (tm, tm), 0)
    tj = lax.broadcasted_iota(jnp.int32, (tm, tm), 1)
    su = jnp.where(ti < tj, 1.0, 0.0).astype(BF16)
    before = jnp.dot(cmat.astype(BF16), su, preferred_element_type=F32) + carry_ref[...]
    r1 = jnp.sum(jnp.where(oh1, before, 0.0), axis=0, keepdims=True)
    r2 = jnp.sum(jnp.where(oh2, before, 0.0), axis=0, keepdims=True)
    carry = carry_ref[...] + jnp.sum(cmat, axis=1, keepdims=True)
    carry_ref[...] = carry
    cnt_ref[...] = jnp.broadcast_to(carry, cnt_ref.shape)
    o_ref[...] = jnp.concatenate([e1.astype(F32), e2.astype(F32), w1, w2, r1, r2,
                                  jnp.zeros((2, tm), F32)], axis=0)


def _dispatch_kernel(d1_ref, d2_ref, x_ref, g_ref, sh_ref, sc_ref, xs_ref, h_ref, sem):
    i = pl.program_id(0)
    tm = x_ref.shape[0]
    h_ref[...] = _modnorm(x_ref[...], g_ref[...], sh_ref[0], sc_ref[0])

    def issue(r, c):
        t = i * tm + r
        pltpu.make_async_copy(h_ref.at[pl.ds(r, 1)], xs_ref.at[pl.ds(d1_ref[t], 1)], sem).start()
        pltpu.make_async_copy(h_ref.at[pl.ds(r, 1)], xs_ref.at[pl.ds(d2_ref[t], 1)], sem).start()
        return c

    lax.fori_loop(0, tm, issue, 0)
    pltpu.make_async_copy(h_ref, xs_ref.at[pl.ds(0, tm)], sem).wait()
    pltpu.make_async_copy(h_ref, xs_ref.at[pl.ds(0, tm)], sem).wait()


def _expert_kernel(te_ref, ts_ref, nv_ref, xs_ref, wg_ref, wu_ref, wd_ref, ys_ref, wgb_ref, wub_ref, wdb_ref):
    i = pl.program_id(0)
    new_expert = (i == 0) | (te_ref[i] != te_ref[jnp.maximum(i - 1, 0)])

    @pl.when(new_expert)
    def _():
        wgb_ref[...] = wg_ref[0].astype(BF16)
        wub_ref[...] = wu_ref[0].astype(BF16)
        wdb_ref[...] = wd_ref[0].astype(BF16)

    @pl.when(i < nv_ref[0])
    def _():
        x = xs_ref[...].astype(BF16)
        hg = jnp.dot(x, wgb_ref[...], preferred_element_type=F32)
        hu = jnp.dot(x, wub_ref[...], preferred_element_type=F32)
        hid = (hg * jax.nn.sigmoid(hg) * hu).astype(BF16)
        ys_ref[...] = jnp.dot(hid, wdb_ref[...], preferred_element_type=F32)


def _combine_kernel(d1_ref, d2_ref, x_ref, w_ref, gt_ref, fn_ref, ys_ref, o_ref, b1_ref, b2_ref, sem,
                    *, final_norm):
    i = pl.program_id(0)
    tm = x_ref.shape[0]

    def issue(r, c):
        t = i * tm + r
        pltpu.make_async_copy(ys_ref.at[pl.ds(d1_ref[t], 1)], b1_ref.at[pl.ds(r, 1)], sem).start()
        pltpu.make_async_copy(ys_ref.at[pl.ds(d2_ref[t], 1)], b2_ref.at[pl.ds(r, 1)], sem).start()
        return c

    lax.fori_loop(0, tm, issue, 0)
    pltpu.make_async_copy(ys_ref.at[pl.ds(0, tm)], b1_ref, sem).wait()
    pltpu.make_async_copy(ys_ref.at[pl.ds(0, tm)], b2_ref, sem).wait()
    w = w_ref[...]
    y = w[:, 0:1] * b1_ref[...] + w[:, 1:2] * b2_ref[...]
    out = x_ref[...] + gt_ref[0] * y
    if final_norm:
        ms = jnp.mean(out * out, axis=-1, keepdims=True)
        out = out * lax.rsqrt(ms + EPS) * fn_ref[...]
    o_ref[...] = out


def _hier_moe(x2d, seq, g, sh, sc, gate, r1_w, r1_b, r2_w, r2_b, wg, wu, wd, final_w):
    t, d = x2d.shape
    ne = MOE_EXPERTS
    rw = jnp.zeros((128, d), F32).at[0:MOE_GROUPS].set(r1_w.T).at[8:8 + ne].set(r2_w.T)
    rb = jnp.zeros((128, 1), F32).at[0:MOE_GROUPS, 0].set(r1_b).at[8:8 + ne, 0].set(r2_b)
    tmr = min(512, seq)
    bmap = lambda i: ((i * tmr) // seq, 0, 0)
    route, cnt = pl.pallas_call(
        _router_kernel,
        out_shape=(jax.ShapeDtypeStruct((8, t), F32), jax.ShapeDtypeStruct((ne, 128), F32)),
        grid=(t // tmr,),
        in_specs=[pl.BlockSpec((tmr, d), lambda i: (i, 0)),
                  pl.BlockSpec((1, d), lambda i: (0, 0)),
                  pl.BlockSpec((1, 1, d), bmap), pl.BlockSpec((1, 1, d), bmap),
                  pl.BlockSpec((128, d), lambda i: (0, 0)),
                  pl.BlockSpec((128, 1), lambda i: (0, 0))],
        out_specs=(pl.BlockSpec((8, tmr), lambda i: (0, i)),
                   pl.BlockSpec((ne, 128), lambda i: (0, 0))),
        scratch_shapes=[pltpu.VMEM((ne, 1), F32)],
        compiler_params=_cparams("arbitrary"), name="moe_router",
    )(x2d, g.reshape(1, d), sh[:, None, :], sc[:, None, :], rw, rb)

    e1 = route[0].astype(jnp.int32)
    e2 = route[1].astype(jnp.int32)
    tile = EXPERT_TILE
    counts = cnt[:, 0].astype(jnp.int32)
    padded = ((counts + tile - 1) // tile) * tile
    ends = jnp.cumsum(padded)
    offs = ends - padded
    dest1 = offs[e1] + route[4].astype(jnp.int32)
    dest2 = offs[e2] + route[5].astype(jnp.int32)
    nt = (2 * t) // tile + ne
    n_used = ends[-1] // tile
    tid = jnp.arange(nt, dtype=jnp.int32)
    tsrc = jnp.minimum(tid, n_used - 1)
    texp = jnp.minimum(jnp.sum((tsrc[:, None] * tile >= ends[None, :]).astype(jnp.int32), axis=1), ne - 1)
    wts = jnp.concatenate([route[2:4].T, jnp.zeros((t, 6), F32)], axis=1)

    tmd = min(256, seq)
    bmapd = lambda i, a, b: ((i * tmd) // seq, 0, 0)
    xs = pl.pallas_call(
        _dispatch_kernel, out_shape=jax.ShapeDtypeStruct((nt * tile, d), F32),
        grid_spec=pltpu.PrefetchScalarGridSpec(
            num_scalar_prefetch=2, grid=(t // tmd,),
            in_specs=[pl.BlockSpec((tmd, d), lambda i, a, b: (i, 0)),
                      pl.BlockSpec((1, d), lambda i, a, b: (0, 0)),
                      pl.BlockSpec((1, 1, d), bmapd), pl.BlockSpec((1, 1, d), bmapd)],
            out_specs=pl.BlockSpec(memory_space=pl.ANY),
            scratch_shapes=[pltpu.VMEM((tmd, d), F32), pltpu.SemaphoreType.DMA(())]),
        compiler_params=_cparams("arbitrary"), name="moe_dispatch",
    )(dest1, dest2, x2d, g.reshape(1, d), sh[:, None, :], sc[:, None, :])

    ff = wg.shape[2]
    ys = pl.pallas_call(
        _expert_kernel, out_shape=jax.ShapeDtypeStruct((nt * tile, d), F32),
        grid_spec=pltpu.PrefetchScalarGridSpec(
            num_scalar_prefetch=3, grid=(nt,),
            in_specs=[pl.BlockSpec((tile, d), lambda i, te, ts, nv: (ts[i], 0)),
                      pl.BlockSpec((1, d, ff), lambda i, te, ts, nv: (te[i], 0, 0)),
                      pl.BlockSpec((1, d, ff), lambda i, te, ts, nv: (te[i], 0, 0)),
                      pl.BlockSpec((1, ff, d), lambda i, te, ts, nv: (te[i], 0, 0))],
            out_specs=pl.BlockSpec((tile, d), lambda i, te, ts, nv: (ts[i], 0)),
            scratch_shapes=[pltpu.VMEM((d, ff), BF16), pltpu.VMEM((d, ff), BF16), pltpu.VMEM((ff, d), BF16)]),
        compiler_params=_cparams("arbitrary"), name="moe_experts",
    )(texp, tsrc, n_used.reshape(1).astype(jnp.int32), xs, wg, wu, wd)

    fin = final_w is not None
    fw = (final_w if fin else jnp.ones((d,), F32)).reshape(1, d)
    bmapc = lambda i, a, b: ((i * tmd) // seq, 0, 0)
    return pl.pallas_call(
        functools.partial(_combine_kernel, final_norm=fin),
        out_shape=jax.ShapeDtypeStruct((t, d), F32),
        grid_spec=pltpu.PrefetchScalarGridSpec(
            num_scalar_prefetch=2, grid=(t // tmd,),
            in_specs=[pl.BlockSpec((tmd, d), lambda i, a, b: (i, 0)),
                      pl.BlockSpec((tmd, 8), lambda i, a, b: (i, 0)),
                      pl.BlockSpec((1, 1, d), bmapc),
                      pl.BlockSpec((1, d), lambda i, a, b: (0, 0)),
                      pl.BlockSpec(memory_space=pl.ANY)],
            out_specs=pl.BlockSpec((tmd, d), lambda i, a, b: (i, 0)),
            scratch_shapes=[pltpu.VMEM((tmd, d), F32), pltpu.VMEM((tmd, d), F32),
                            pltpu.SemaphoreType.DMA(())]),
        compiler_params=_cparams("arbitrary"), name="moe_combine",
    )(dest1, dest2, x2d, wts, gate[:, None, :], fw, ys)


def _ssd_kernel(z_ref, x_ref, b_ref, c_ref, dt_ref, cwx_ref, cwb_ref, cwc_ref, cbx_ref, cbb_ref, cbc_ref,
                dtb_ref, alog_ref, dsk_ref, nw_ref, o_ref,
                xe_ref, be_ref, ce_ref, st_ref, acs_ref, dtg_ref):
    c = pl.program_id(1)
    lq = SSD_CHUNK
    ng = SSD_GROUPS
    gw = x_ref.shape[1] // ng
    hpg = gw // SSD_HEAD_DIM
    halo = 8
    k0 = halo - (SSD_CONV - 1)

    @pl.when(c == 0)
    def _():
        xe_ref[0:halo] = jnp.zeros((halo, xe_ref.shape[1]), F32)
        be_ref[0:halo] = jnp.zeros((halo, be_ref.shape[1]), F32)
        ce_ref[0:halo] = jnp.zeros((halo, ce_ref.shape[1]), F32)
        st_ref[...] = jnp.zeros_like(st_ref)

    xe_ref[halo:halo + lq] = x_ref[...].astype(F32)
    be_ref[halo:halo + lq] = b_ref[...].astype(F32)
    ce_ref[halo:halo + lq] = c_ref[...].astype(F32)

    def conv_silu(e_ref, w_ref, bias_ref, cols):
        acc = bias_ref[:, cols]
        for k in range(SSD_CONV):
            acc = acc + e_ref[k0 + k:k0 + k + lq, cols] * w_ref[k:k + 1, cols]
        return acc * jax.nn.sigmoid(acc)

    dt = jax.nn.softplus(dt_ref[...] + dtb_ref[...])
    adt = dt * (-jnp.exp(alog_ref[...]))
    li = lax.broadcasted_iota(jnp.int32, (lq, lq), 0)
    si = lax.broadcasted_iota(jnp.int32, (lq, lq), 1)
    tri = jnp.where(si <= li, 1.0, 0.0)
    acs = jnp.dot(tri, adt, precision=HIGHEST, preferred_element_type=F32)
    lane = lax.broadcasted_iota(jnp.int32, (lq, 128), 1)
    for gi in range(ng):
        sh = (128 - gi * hpg) % 128
        acs_ref[gi] = jnp.where(lane < hpg, pltpu.roll(acs, sh, axis=1) if sh else acs, 0.0)
        dtg_ref[gi] = jnp.where(lane < hpg, pltpu.roll(dt, sh, axis=1) if sh else dt, 0.0)
    causal = si <= li
    lane_lo = lane < SSD_HEAD_DIM

    def group_body(gi, carry):
        xc0 = pl.multiple_of(gi * gw, gw)
        nc0 = pl.multiple_of(gi * SSD_STATE, SSD_STATE)
        xg = conv_silu(xe_ref, cwx_ref, cbx_ref, pl.ds(xc0, gw))
        bg = conv_silu(be_ref, cwb_ref, cbb_ref, pl.ds(nc0, SSD_STATE))
        cg = conv_silu(ce_ref, cwc_ref, cbc_ref, pl.ds(nc0, SSD_STATE))
        bt = bg.T
        cgb = cg.astype(BF16)
        cb = jnp.dot(cgb, bt.astype(BF16), preferred_element_type=F32)
        acs_g = acs_ref[gi]
        acs_t = acs_g.T
        dt_t = dtg_ref[gi].T
        xgb = xg.astype(BF16)
        ys = []
        for jp in range(hpg // 2):
            lhs_parts, w_parts, ecols, dlast = [], [], [], []
            for j in (2 * jp, 2 * jp + 1):
                e_col = jnp.broadcast_to(acs_g[:, j:j + 1], (lq, lq))
                r_row = jnp.broadcast_to(acs_t[j:j + 1, :], (lq, lq))
                dt_row = dt_t[j:j + 1, :]
                lmat = jnp.where(causal, jnp.exp(e_col - r_row), 0.0) * cb * dt_row
                lhs_parts.append(lmat.astype(BF16))
                a_last = acs_t[j:j + 1, lq - 1:lq]
                wrow = dt_row * jnp.exp(a_last - acs_t[j:j + 1, :])
                w_parts.append((bt * wrow).astype(BF16))
                ecols.append(e_col)
                dlast.append(jnp.exp(a_last))
            lhs = jnp.concatenate([jnp.concatenate(lhs_parts, axis=1),
                                   jnp.concatenate(w_parts, axis=1)], axis=0)
            xp = xgb[:, jp * 128:(jp + 1) * 128]
            zero = jnp.zeros_like(xp)
            rhs = jnp.concatenate([jnp.where(lane_lo, xp, zero), jnp.where(lane_lo, zero, xp)], axis=0)
            res = jnp.dot(lhs, rhs, preferred_element_type=F32)
            st = st_ref[gi, :, jp * 128:(jp + 1) * 128]
            yoff = jnp.dot(cgb, st.astype(BF16), preferred_element_type=F32)
            escale = jnp.exp(jnp.where(lane_lo, ecols[0], ecols[1]))
            ys.append(res[:lq] + yoff * escale)
            dl = jnp.where(lane_lo[:1], dlast[0], dlast[1])
            st_ref[gi, :, jp * 128:(jp + 1) * 128] = st * dl + res[lq:]
        y = jnp.concatenate(ys, axis=1) + dsk_ref[:, pl.ds(xc0, gw)] * xg
        zg = z_ref[:, pl.ds(xc0, gw)].astype(F32)
        y = y * (zg * jax.nn.sigmoid(zg))
        ms = jnp.mean(y * y, axis=-1, keepdims=True)
        y = y * lax.rsqrt(ms + EPS) * nw_ref[:, pl.ds(xc0, gw)]
        o_ref[:, pl.ds(xc0, gw)] = y.astype(o_ref.dtype)
        return carry

    lax.fori_loop(0, ng, group_body, 0)
    xe_ref[0:halo] = xe_ref[lq:lq + halo]
    be_ref[0:halo] = be_ref[lq:lq + halo]
    ce_ref[0:halo] = ce_ref[lq:lq + halo]


def _ssd_mixer(zx, dt_raw, bsz, seq, conv_w, conv_b, dt_bias, a_log, d_skip, norm_w):
    t = zx.shape[0]
    heads = a_log.shape[0]
    inner = heads * SSD_HEAD_DIM
    bcw = SSD_GROUPS * SSD_STATE
    lq = SSD_CHUNK
    nc = seq // lq
    pad = 128 - heads
    cw = conv_w.astype(F32)
    cwp = jnp.zeros((8, cw.shape[1]), F32).at[:SSD_CONV].set(cw)
    cb = conv_b.astype(F32).reshape(1, -1)
    row = lambda b, c: (b * nc + c, 0)
    const = lambda b, c: (0, 0)
    xb = inner // inner
    bb = (2 * inner) // bcw
    return pl.pallas_call(
        _ssd_kernel, out_shape=jax.ShapeDtypeStruct((t, inner), BF16), grid=(bsz, nc),
        in_specs=[pl.BlockSpec((lq, inner), lambda b, c: (b * nc + c, 0)),
                  pl.BlockSpec((lq, inner), lambda b, c: (b * nc + c, xb)),
                  pl.BlockSpec((lq, bcw), lambda b, c: (b * nc + c, bb)),
                  pl.BlockSpec((lq, bcw), lambda b, c: (b * nc + c, bb + 1)),
                  pl.BlockSpec((lq, 128), row),
                  pl.BlockSpec((8, inner), const), pl.BlockSpec((8, bcw), const), pl.BlockSpec((8, bcw), const),
                  pl.BlockSpec((1, inner), const), pl.BlockSpec((1, bcw), const), pl.BlockSpec((1, bcw), const),
                  pl.BlockSpec((1, 128), const), pl.BlockSpec((1, 128), const),
                  pl.BlockSpec((1, inner), const), pl.BlockSpec((1, inner), const)],
        out_specs=pl.BlockSpec((lq, inner), row),
        scratch_shapes=[pltpu.VMEM((lq + 8, inner), F32), pltpu.VMEM((lq + 8, bcw), F32),
                        pltpu.VMEM((lq + 8, bcw), F32),
                        pltpu.VMEM((SSD_GROUPS, SSD_STATE, inner // SSD_GROUPS), F32),
                        pltpu.VMEM((SSD_GROUPS, lq, 128), F32), pltpu.VMEM((SSD_GROUPS, lq, 128), F32)],
        compiler_params=_cparams("arbitrary", "arbitrary"), name="ssd_mixer",
    )(zx, zx, zx, zx, dt_raw,
      cwp[:, :inner], cwp[:, inner:inner + bcw], cwp[:, inner + bcw:],
      cb[:, :inner], cb[:, inner:inner + bcw], cb[:, inner + bcw:],
      jnp.pad(dt_bias.astype(F32), (0, pad)).reshape(1, 128),
      jnp.pad(a_log.astype(F32), (0, pad)).reshape(1, 128),
      jnp.repeat(d_skip.astype(F32), SSD_HEAD_DIM).reshape(1, inner),
      norm_w.astype(F32).reshape(1, inner))


def kernel(x, c, positions, ada_w, ada_b, norm_mix, norm_ffn, mix_a_w_in, s5_lam_re, s5_lam_im, s5_log_dt, s5_b_re, s5_b_im, s5_c_re, s5_c_im, s5_d, s5_w_glu, attn_sinks, mix_a_w_out, ssd_w_in, ssd_conv_w, ssd_conv_b, ssd_dt_bias, ssd_a_log, ssd_d, ssd_norm_w, ssd_w_out, moe_r1_w, moe_r1_b, moe_r2_w, moe_r2_b, moe_w_gate, moe_w_up, moe_w_down, final_norm):
    bsz, seq, d = x.shape
    depth = ada_w.shape[0]
    t = bsz * seq
    xf = x.reshape(t, d).astype(F32)
    mod = _adaln(c.astype(F32), ada_w, ada_b)
    cos_t, sin_t = _rope_tables(positions)

    for i in range(depth):
        sh1, sc1, g1, sh2, sc2, g2 = jnp.split(mod[i], 6, axis=-1)
        j = i // 2
        if i % 2 == 0:
            s5w = s5_lam_re.shape[1] * S5_GROUP
            w_in = mix_a_w_in[j].astype(BF16)
            u, qkv = _inproj0(xf, seq, norm_mix[i], sh1, sc1, w_in[:, :s5w], w_in[:, s5w:], tm=512)
            params = _s5_params(s5_lam_re[j], s5_lam_im[j], s5_log_dt[j], s5_b_re[j], s5_b_im[j],
                                s5_c_re[j], s5_c_im[j], s5_d[j], seq)
            y_s5 = _s5_mixer(u, bsz, seq, params)
            wglu = s5_w_glu[j].astype(BF16)
            y_glu = _gelu_glu(y_s5, wglu[:, :s5w], wglu[:, s5w:], tm=1024, tn=512)
            wq = attn_sinks.shape[1] * HEAD_DIM
            wk = KV_HEADS * HEAD_DIM
            y_att = _attention(qkv, bsz, seq, cos_t, sin_t, attn_sinks[j],
                               q_col=0, k_col=wq // wk, v_col=wq // wk + 1)
            wo = mix_a_w_out[j].astype(BF16)
            xf = _matmul_residual([y_glu, y_att], [wo[:s5w], wo[s5w:]], xf, seq, g1, tm=1024, tn=1024)
        else:
            heads = ssd_a_log.shape[1]
            inner = heads * SSD_HEAD_DIM
            main = inner + inner + 2 * SSD_GROUPS * SSD_STATE
            w_in = ssd_w_in[j]
            w_dt = jnp.pad(w_in[:, main:], ((0, 0), (0, 128 - heads))).astype(BF16)
            zx, dt_raw = _norm_mod_matmul(xf, seq, norm_mix[i], sh1, sc1, w_in[:, :main].astype(BF16),
                                          tm=1024, tn=512, w2=w_dt)
            y = _ssd_mixer(zx, dt_raw, bsz, seq, ssd_conv_w[j], ssd_conv_b[j], ssd_dt_bias[j],
                           ssd_a_log[j], ssd_d[j], ssd_norm_w[j])
            xf = _matmul_residual([y], [ssd_w_out[j].astype(BF16)], xf, seq, g1, tm=512, tn=1024)
        xf = _hier_moe(xf, seq, norm_ffn[i], sh2, sc2, g2, moe_r1_w[i], moe_r1_b[i], moe_r2_w[i], moe_r2_b[i],
                       moe_w_gate[i], moe_w_up[i], moe_w_down[i],
                       final_norm if i == depth - 1 else None)
    return xf.reshape(bsz, seq, d).astype(x.dtype)
```

```python
import functools
import math

import jax
import jax.numpy as jnp
from jax import lax
from jax.experimental import pallas as pl
from jax.experimental.pallas import tpu as pltpu

F32 = jnp.float32
BF16 = jnp.bfloat16
HIGHEST = lax.Precision.HIGHEST

EPS = 1e-6
S5_GROUP = 16
S5_STATE = 64
S5_CHUNK = 16
HEAD_DIM = 64
KV_HEADS = 2
WINDOW = 128
ROPE_DIM = 16
ROPE_THETA = 500000.0
SSD_HEAD_DIM = 64
SSD_GROUPS = 8
SSD_STATE = 128
SSD_CONV = 4
SSD_CHUNK = 128
MOE_GROUPS = 4
MOE_EPG = 8
MOE_EXPERTS = MOE_GROUPS * MOE_EPG
EXPERT_TILE = 256
NEG_BIG = -1e30
VMEM_LIMIT = 56 * 1024 * 1024


def _cparams(*sem):
    return pltpu.CompilerParams(dimension_semantics=sem, vmem_limit_bytes=VMEM_LIMIT)


def _modnorm(x, g, sh, sc):
    ms = jnp.mean(x * x, axis=-1, keepdims=True)
    return (x * lax.rsqrt(ms + EPS) * g) * (1.0 + sc) + sh


def _adaln_kernel(c_ref, w_ref, b_ref, o_ref):
    c = c_ref[...]
    a = c * jax.nn.sigmoid(c)
    o_ref[0] = jnp.dot(a, w_ref[0], precision=HIGHEST, preferred_element_type=F32) + b_ref[0]


def _adaln(c, ada_w, ada_b):
    depth, d, n = ada_w.shape
    bsz = c.shape[0]
    rows = 8
    cp = jnp.zeros((rows, d), F32).at[:bsz].set(c)
    tn = 1024
    out = pl.pallas_call(
        _adaln_kernel,
        out_shape=jax.ShapeDtypeStruct((depth, rows, n), F32),
        grid=(depth, n // tn),
        in_specs=[pl.BlockSpec((rows, d), lambda l, j: (0, 0)),
                  pl.BlockSpec((1, d, tn), lambda l, j: (l, 0, j)),
                  pl.BlockSpec((1, 1, tn), lambda l, j: (l, 0, j))],
        out_specs=pl.BlockSpec((1, rows, tn), lambda l, j: (l, 0, j)),
        compiler_params=_cparams("arbitrary", "arbitrary"),
        name="adaln",
    )(cp, ada_w, ada_b.reshape(depth, 1, n))
    return out[:, :bsz]


def _inproj1_kernel(x_ref, g_ref, sh_ref, sc_ref, w_ref, w2_ref, o_ref, o2_ref, h_ref):
    @pl.when(pl.program_id(1) == 0)
    def _():
        h = _modnorm(x_ref[...], g_ref[...], sh_ref[0], sc_ref[0]).astype(BF16)
        h_ref[...] = h
        o2_ref[...] = jnp.dot(h, w2_ref[...], preferred_element_type=F32)

    o_ref[...] = jnp.dot(h_ref[...], w_ref[...].astype(BF16), preferred_element_type=F32).astype(o_ref.dtype)


def _inproj1(x2d, seq, g, sh, sc, w, n_main, w2, tm, tn):
    t, d = x2d.shape
    tm = min(tm, seq)
    n2 = w2.shape[1]
    bmap = lambda i, j: ((i * tm) // seq, 0, 0)
    return pl.pallas_call(
        _inproj1_kernel,
        out_shape=(jax.ShapeDtypeStruct((t, n_main), BF16), jax.ShapeDtypeStruct((t, n2), F32)),
        grid=(t // tm, n_main // tn),
        in_specs=[pl.BlockSpec((tm, d), lambda i, j: (i, 0)),
                  pl.BlockSpec((1, d), lambda i, j: (0, 0)),
                  pl.BlockSpec((1, 1, d), bmap), pl.BlockSpec((1, 1, d), bmap),
                  pl.BlockSpec((d, tn), lambda i, j: (0, j)),
                  pl.BlockSpec((d, n2), lambda i, j: (0, 0))],
        out_specs=(pl.BlockSpec((tm, tn), lambda i, j: (i, j)), pl.BlockSpec((tm, n2), lambda i, j: (i, 0))),
        scratch_shapes=[pltpu.VMEM((tm, d), BF16)],
        compiler_params=_cparams("arbitrary", "arbitrary"), name="inproj1",
    )(x2d, g.reshape(1, d), sh[:, None, :], sc[:, None, :], w, w2)


def _mmres_kernel(*refs, n_in):
    a_refs, w_refs = refs[:n_in], refs[n_in:2 * n_in]
    x_ref, gt_ref, o_ref = refs[2 * n_in:]
    acc = jnp.dot(a_refs[0][...], w_refs[0][...].astype(BF16), preferred_element_type=F32)
    for a_ref, w_ref in zip(a_refs[1:], w_refs[1:]):
        acc = acc + jnp.dot(a_ref[...], w_ref[...].astype(BF16), preferred_element_type=F32)
    o_ref[...] = x_ref[...] + gt_ref[0] * acc


def _matmul_residual(a_list, w, x2d, seq, gate, tm, tn):
    t, n = x2d.shape
    tm = min(tm, seq)
    in_specs, row = [], 0
    for a in a_list:
        in_specs.append(pl.BlockSpec((tm, a.shape[1]), lambda i, j: (i, 0)))
    for a in a_list:
        k = a.shape[1]
        in_specs.append(pl.BlockSpec((k, tn), functools.partial(lambda i, j, rb: (rb, j), rb=row // k)))
        row += k
    in_specs.append(pl.BlockSpec((tm, tn), lambda i, j: (i, j)))
    in_specs.append(pl.BlockSpec((1, 1, tn), lambda i, j: ((i * tm) // seq, 0, j)))
    return pl.pallas_call(
        functools.partial(_mmres_kernel, n_in=len(a_list)),
        out_shape=jax.ShapeDtypeStruct((t, n), F32), grid=(t // tm, n // tn),
        in_specs=in_specs, out_specs=pl.BlockSpec((tm, tn), lambda i, j: (i, j)),
        compiler_params=_cparams("arbitrary", "arbitrary"), name="matmul_residual",
    )(*a_list, *([w] * len(a_list)), x2d, gate[:, None, :])


def _inproj0_kernel(x_ref, g_ref, sh_ref, sc_ref, wu_ref, wa_ref, u_ref, a_ref, wub_ref, wab_ref):
    @pl.when(pl.program_id(0) == 0)
    def _():
        wub_ref[...] = wu_ref[...].astype(BF16)
        wab_ref[...] = wa_ref[...].astype(BF16)

    h = _modnorm(x_ref[...], g_ref[...], sh_ref[0], sc_ref[0]).astype(BF16)
    u_ref[...] = jnp.dot(h, wub_ref[...], preferred_element_type=F32)
    a_ref[...] = jnp.dot(h, wab_ref[...], preferred_element_type=F32).astype(a_ref.dtype)


def _inproj0(x2d, seq, g, sh, sc, w_u, w_a, tm):
    t, d = x2d.shape
    tm = min(tm, seq)
    nu, na = w_u.shape[1], w_a.shape[1]
    bmap = lambda i: ((i * tm) // seq, 0, 0)
    return pl.pallas_call(
        _inproj0_kernel,
        out_shape=(jax.ShapeDtypeStruct((t, nu), F32), jax.ShapeDtypeStruct((t, na), BF16)),
        grid=(t // tm,),
        in_specs=[pl.BlockSpec((tm, d), lambda i: (i, 0)),
                  pl.BlockSpec((1, d), lambda i: (0, 0)),
                  pl.BlockSpec((1, 1, d), bmap), pl.BlockSpec((1, 1, d), bmap),
                  pl.BlockSpec((d, nu), lambda i: (0, 0), pipeline_mode=pl.Buffered(1)),
                  pl.BlockSpec((d, na), lambda i: (0, 0), pipeline_mode=pl.Buffered(1))],
        out_specs=(pl.BlockSpec((tm, nu), lambda i: (i, 0)), pl.BlockSpec((tm, na), lambda i: (i, 0))),
        scratch_shapes=[pltpu.VMEM((d, nu), BF16), pltpu.VMEM((d, na), BF16)],
        compiler_params=_cparams("arbitrary"), name="inproj0",
    )(x2d, g.reshape(1, d), sh[:, None, :], sc[:, None, :], w_u, w_a)


S5_SLAB = 8
S5_SEGS = 8


def _s5_params(lam_re, lam_im, log_dt, b_re, b_im, c_re, c_im, d_skip, seq):
    g, n = lam_re.shape
    p = S5_GROUP
    lc = S5_CHUNK
    gs = S5_SLAB
    nj = g // gs
    lam = lax.complex(lam_re.astype(F32), lam_im.astype(F32))
    dt = jnp.exp(log_dt.astype(F32))[:, None]
    ld = lam * dt
    lam_bar = jnp.exp(ld)
    bmat = lax.complex(b_re.astype(F32), b_im.astype(F32))
    b_bar = ((lam_bar - 1.0) / lam)[..., None] * bmat
    cmat = lax.complex(c_re.astype(F32), c_im.astype(F32))
    ks = jnp.arange(lc, dtype=F32)
    pw = jnp.exp(ld[:, None, :] * ks[None, :, None])
    pw1 = jnp.exp(ld[:, None, :] * (ks[None, :, None] + 1.0))
    kk = jnp.real(jnp.einsum('gpn,gkn,gnq->gkpq', cmat, pw, b_bar, precision=HIGHEST))
    eg = jnp.eye(gs, dtype=F32)
    ex_p = jnp.einsum('ab,pr->apbr', eg, jnp.eye(p, dtype=F32)).reshape(gs, p, gs * p)
    ex_n = jnp.einsum('ab,nr->anbr', eg, jnp.eye(n, dtype=F32)).reshape(gs, n, gs * n)
    kq_ = kk.transpose(0, 1, 3, 2).reshape(nj, gs, lc, p, p)
    kexp = jnp.einsum('jakqp,apc->jkaqc', kq_, ex_p, precision=HIGHEST).reshape(nj, lc, gs * p, gs * p)
    kexp = kexp.at[:, 0].add(jnp.eye(gs * p, dtype=F32)[None] * d_skip.astype(F32).reshape(nj, 1, gs * p))
    kexp = kexp.astype(BF16)
    wb = (pw[:, ::-1, :, None] * b_bar[:, None, :, :]).transpose(0, 1, 3, 2)
    wb = wb.reshape(nj, gs, lc, p, n)
    wb_slab = jnp.concatenate(
        [jnp.einsum('jasqn,anc->jsaqc', part, ex_n, precision=HIGHEST).reshape(nj, lc * gs * p, gs * n)
         for part in (jnp.real(wb), jnp.imag(wb))], axis=-1).astype(BF16)
    cl = (cmat[:, None, :, :] * pw1[:, :, None, :]).transpose(0, 3, 1, 2)
    cl = cl.reshape(nj, gs, n, lc, p)
    woff_slab = jnp.concatenate(
        [jnp.einsum('janlp,apc->janlc', part, ex_p, precision=HIGHEST).reshape(nj, gs * n, lc * gs * p)
         for part in (jnp.real(cl), -jnp.imag(cl))], axis=1).astype(BF16)
    seglen = (seq // lc) // S5_SEGS
    kq = jnp.arange(seglen + 1, dtype=F32) * float(lc)
    pq = jnp.exp(ld[:, None, :] * kq[None, :, None]).reshape(nj, gs, seglen + 1, n)
    pq = pq.transpose(0, 2, 1, 3).reshape(nj, seglen + 1, gs * n)
    return kexp, wb_slab, woff_slab, jnp.real(pq), jnp.imag(pq)


def _s5_kernel(u_ref, k_ref, wb_ref, woff_ref, pre_ref, pim_ref, y_ref, sloc_ref, sin_ref, m_ref):
    lc = S5_CHUNK
    w = u_ref.shape[1]

    @pl.when((pl.program_id(0) == 0) & (pl.program_id(1) == 0))
    def _():
        m_ref[...] = jnp.zeros_like(m_ref)

    @pl.when(pl.program_id(1) == 0)
    def _():
        for s in range(lc):
            for l in range(s, lc):
                m_ref[s * w:(s + 1) * w, l * w:(l + 1) * w] = k_ref[0, l - s]

    nc = u_ref.shape[0] // lc
    hw = wb_ref.shape[2] // 2
    seglen = nc // S5_SEGS
    u_all = jnp.concatenate([u_ref[pl.ds(s, nc, stride=lc), :] for s in range(lc)], axis=1).astype(BF16)
    nq = hw // w
    sloc = jnp.dot(u_all, wb_ref[0], preferred_element_type=F32)
    for q in range(2 * nq):
        sloc_ref[q] = sloc[:, q * w:(q + 1) * w]
    a_re, a_im = pre_ref[0, 1:2], pim_ref[0, 1:2]

    def put(rows, re, im):
        for q in range(nq):
            sin_ref[q, rows, :] = re[:, q * w:(q + 1) * w]
            sin_ref[nq + q, rows, :] = im[:, q * w:(q + 1) * w]

    def get(ref, rows):
        return (jnp.concatenate([ref[q, rows, :] for q in range(nq)], axis=1),
                jnp.concatenate([ref[nq + q, rows, :] for q in range(nq)], axis=1))

    def seg_step(k, st):
        sre, sim = st
        rows = pl.ds(k, S5_SEGS, stride=seglen)
        put(rows, sre, sim)
        lre, lim = get(sloc_ref, rows)
        return a_re * sre - a_im * sim + lre, a_re * sim + a_im * sre + lim

    z = jnp.zeros((S5_SEGS, hw), F32)
    ere, eim = lax.fori_loop(0, seglen, seg_step, (z, z))
    g_re, g_im = pre_ref[0, seglen:seglen + 1], pim_ref[0, seglen:seglen + 1]
    cre, cim = jnp.zeros((1, hw), F32), jnp.zeros((1, hw), F32)
    pre, pim = pre_ref[0, 0:seglen], pim_ref[0, 0:seglen]
    for sg in range(1, S5_SEGS):
        cre, cim = (g_re * cre - g_im * cim + ere[sg - 1:sg], g_re * cim + g_im * cre + eim[sg - 1:sg])
        rows = slice(sg * seglen, (sg + 1) * seglen)
        cur_re, cur_im = get(sin_ref, rows)
        put(rows, cur_re + pre * cre - pim * cim, cur_im + pre * cim + pim * cre)
    s_in = jnp.concatenate([sin_ref[q] for q in range(2 * nq)], axis=1).astype(BF16)
    for lb in range(lc // 2):
        c0, c1 = 2 * lb * w, (2 * lb + 2) * w
        y2 = (jnp.dot(u_all[:, :c1], m_ref[:c1, c0:c1], preferred_element_type=F32)
              + jnp.dot(s_in, woff_ref[0, :, c0:c1], preferred_element_type=F32))
        y_ref[pl.ds(2 * lb, nc, stride=lc), :] = y2[:, :w]
        y_ref[pl.ds(2 * lb + 1, nc, stride=lc), :] = y2[:, w:]


def _s5_mixer(u, bsz, seq, params):
    kexp, wb_slab, woff_slab, pq_re, pq_im = params
    nj, km, sw = wb_slab.shape
    w = km // S5_CHUNK
    nc = seq // S5_CHUNK
    t = u.shape[0]
    const = lambda j, b: (j, 0, 0)
    return pl.pallas_call(
        _s5_kernel, out_shape=jax.ShapeDtypeStruct((t, nj * w), F32), grid=(nj, bsz),
        in_specs=[pl.BlockSpec((seq, w), lambda j, b: (b, j)),
                  pl.BlockSpec((1, S5_CHUNK, w, w), lambda j, b: (j, 0, 0, 0)),
                  pl.BlockSpec((1, km, sw), const),
                  pl.BlockSpec((1, sw, km), const),
                  pl.BlockSpec((1,) + pq_re.shape[1:], const),
                  pl.BlockSpec((1,) + pq_im.shape[1:], const)],
        out_specs=pl.BlockSpec((seq, w), lambda j, b: (b, j)),
        scratch_shapes=[pltpu.VMEM((sw // w, nc, w), F32), pltpu.VMEM((sw // w, nc, w), F32),
                        pltpu.VMEM((km, km), BF16)],
        compiler_params=_cparams("arbitrary", "arbitrary"), name="s5_mixer",
    )(u, kexp, wb_slab, woff_slab, pq_re, pq_im)


def _glu_kernel(y_ref, wv_ref, wg_ref, o_ref, a_ref):
    @pl.when(pl.program_id(1) == 0)
    def _():
        a_ref[...] = jax.nn.gelu(y_ref[...].astype(F32), approximate=True).astype(BF16)

    a = a_ref[...]
    val = jnp.dot(a, wv_ref[...].astype(BF16), preferred_element_type=F32)
    gate = jnp.dot(a, wg_ref[...].astype(BF16), preferred_element_type=F32)
    o_ref[...] = (val * jax.nn.sigmoid(gate)).astype(o_ref.dtype)


def _gelu_glu(y, w, tm, tn):
    t, k = y.shape
    n = w.shape[1] // 2
    tm = min(tm, t)
    nb = n // tn
    return pl.pallas_call(
        _glu_kernel, out_shape=jax.ShapeDtypeStruct((t, n), BF16), grid=(t // tm, n // tn),
        in_specs=[pl.BlockSpec((tm, k), lambda i, j: (i, 0)),
                  pl.BlockSpec((k, tn), lambda i, j: (0, j)),
                  pl.BlockSpec((k, tn), lambda i, j: (0, j + nb))],
        out_specs=pl.BlockSpec((tm, tn), lambda i, j: (i, j)),
        scratch_shapes=[pltpu.VMEM((tm, k), BF16)],
        compiler_params=_cparams("arbitrary", "arbitrary"), name="gelu_glu",
    )(y, w, w)


def _rope(x, cos, sin):
    w = x.shape[1]
    half = ROPE_DIM // 2
    d = lax.broadcasted_iota(jnp.int32, x.shape, 1) % HEAD_DIM
    partner = jnp.where(d < half, pltpu.roll(x, w - half, axis=1), pltpu.roll(x, half, axis=1))
    return x * cos + partner * sin


def _attn_kernel(sink_ref, q_ref, kc_ref, kp_ref, vc_ref, vp_ref,
                 cq_ref, sq_ref, cp_ref, sp_ref, o_ref):
    n = pl.program_id(1)
    wq = q_ref.shape[1]
    hq = wq // HEAD_DIM
    grp = hq // KV_HEADS
    cos_c, sin_c = cq_ref[...], sq_ref[...]
    reps = wq // cos_c.shape[1]
    q = _rope(q_ref[...].astype(F32),
              jnp.concatenate([cos_c] * reps, axis=1), jnp.concatenate([sin_c] * reps, axis=1))
    q = (q * (HEAD_DIM ** -0.5)).astype(BF16)
    kc = _rope(kc_ref[...].astype(F32), cos_c, sin_c).astype(BF16)
    kp = _rope(kp_ref[...].astype(F32), cp_ref[...], sp_ref[...]).astype(BF16)
    kk = jnp.concatenate([kp, kc], axis=0)
    vv = jnp.concatenate([vp_ref[...], vc_ref[...]], axis=0)
    qi = lax.broadcasted_iota(jnp.int32, (WINDOW, 2 * WINDOW), 0)
    kj = lax.broadcasted_iota(jnp.int32, (WINDOW, 2 * WINDOW), 1)
    first = jnp.where(n > 0, 0, WINDOW)
    valid = (kj > qi) & (kj <= qi + WINDOW) & (kj >= first)
    outs = []
    for h in range(hq):
        kvh = h // grp
        qh = q[:, h * HEAD_DIM:(h + 1) * HEAD_DIM]
        kh = kk[:, kvh * HEAD_DIM:(kvh + 1) * HEAD_DIM]
        vh = vv[:, kvh * HEAD_DIM:(kvh + 1) * HEAD_DIM]
        s = lax.dot_general(qh, kh, (((1,), (1,)), ((), ())), preferred_element_type=F32)
        s = jnp.where(valid, s, NEG_BIG)
        sink = sink_ref[h]
        mx = jnp.maximum(jnp.max(s, axis=-1, keepdims=True), sink)
        p = jnp.exp(s - mx)
        den = jnp.sum(p, axis=-1, keepdims=True) + jnp.exp(sink - mx)
        o = jnp.dot(p.astype(BF16), vh, preferred_element_type=F32)
        outs.append(o / den)
    o_ref[...] = jnp.concatenate(outs, axis=1).astype(o_ref.dtype)


def _attention(proj, bsz, seq, cos_t, sin_t, sinks, q_col, k_col, v_col):
    t = proj.shape[0]
    nb = seq // WINDOW
    wq = sinks.shape[0] * HEAD_DIM
    wk = KV_HEADS * HEAD_DIM
    cur = lambda b, n, s: (b * nb + n, 0)
    prv = lambda b, n, s: (b * nb + jnp.maximum(n - 1, 0), 0)
    return pl.pallas_call(
        _attn_kernel, out_shape=jax.ShapeDtypeStruct((t, wq), BF16),
        grid_spec=pltpu.PrefetchScalarGridSpec(
            num_scalar_prefetch=1, grid=(bsz, nb),
            in_specs=[pl.BlockSpec((WINDOW, wq), lambda b, n, s: (b * nb + n, q_col)),
                      pl.BlockSpec((WINDOW, wk), lambda b, n, s: (b * nb + n, k_col)),
                      pl.BlockSpec((WINDOW, wk), lambda b, n, s: (b * nb + jnp.maximum(n - 1, 0), k_col)),
                      pl.BlockSpec((WINDOW, wk), lambda b, n, s: (b * nb + n, v_col)),
                      pl.BlockSpec((WINDOW, wk), lambda b, n, s: (b * nb + jnp.maximum(n - 1, 0), v_col)),
                      pl.BlockSpec((WINDOW, wk), cur), pl.BlockSpec((WINDOW, wk), cur),
                      pl.BlockSpec((WINDOW, wk), prv), pl.BlockSpec((WINDOW, wk), prv)],
            out_specs=pl.BlockSpec((WINDOW, wq), lambda b, n, s: (b * nb + n, 0))),
        compiler_params=_cparams("arbitrary", "arbitrary"), name="swa_attention",
    )(sinks.astype(F32), proj, proj, proj, proj, proj, cos_t, sin_t, cos_t, sin_t)


def _rope_tables(positions):
    half = ROPE_DIM // 2
    inv = 1.0 / (ROPE_THETA ** (jnp.arange(0, ROPE_DIM, 2, dtype=F32) / ROPE_DIM))
    ang = positions.astype(F32).reshape(-1, 1) * inv[None, :]
    cos, sin = jnp.cos(ang), jnp.sin(ang)
    t = ang.shape[0]
    ones = jnp.ones((t, HEAD_DIM - ROPE_DIM), F32)
    cos_h = jnp.concatenate([cos, cos, ones], axis=1)
    sin_h = jnp.concatenate([-sin, sin, 0.0 * ones], axis=1)
    return jnp.tile(cos_h, (1, KV_HEADS)), jnp.tile(sin_h, (1, KV_HEADS))


def _router_kernel(x_ref, g_ref, sh_ref, sc_ref, rw_ref, rb_ref, o_ref, cnt_ref, hp_ref, carry_ref):
    i = pl.program_id(0)
    tm = x_ref.shape[0]
    ne = MOE_EXPERTS

    @pl.when(i == 0)
    def _():
        carry_ref[...] = jnp.zeros_like(carry_ref)

    h = _modnorm(x_ref[...], g_ref[...], sh_ref[0], sc_ref[0])
    half = h.shape[1] // 2
    hi = lax.bitcast_convert_type(h[:, :half].astype(BF16).astype(F32), jnp.uint32)
    lo = lax.bitcast_convert_type(h[:, half:].astype(BF16).astype(F32), jnp.uint32)
    hp_ref[...] = hi | (lo >> 16)
    lt = lax.dot_general(rw_ref[...], h, (((1,), (1,)), ((), ())),
                         precision=HIGHEST, preferred_element_type=F32) + rb_ref[...]
    gl = lt[0:MOE_GROUPS]
    gm = jnp.max(gl, axis=0, keepdims=True)
    gsum = jnp.sum(jnp.exp(gl - gm), axis=0, keepdims=True)
    g_val = 1.0 / gsum
    grow = lax.broadcasted_iota(jnp.int32, gl.shape, 0)
    g_idx = jnp.min(jnp.where(gl == gm, grow, MOE_GROUPS), axis=0, keepdims=True)
    sel = jnp.zeros((MOE_EPG, tm), F32)
    for gi in range(MOE_GROUPS):
        sel = jnp.where(g_idx == gi, lt[8 + gi * MOE_EPG: 8 + (gi + 1) * MOE_EPG], sel)
    erow = lax.broadcasted_iota(jnp.int32, sel.shape, 0)
    m1 = jnp.max(sel, axis=0, keepdims=True)
    i1 = jnp.min(jnp.where(sel == m1, erow, MOE_EPG), axis=0, keepdims=True)
    sel2 = jnp.where(erow == i1, -jnp.inf, sel)
    m2 = jnp.max(sel2, axis=0, keepdims=True)
    i2 = jnp.min(jnp.where(sel2 == m2, erow, MOE_EPG), axis=0, keepdims=True)
    e21 = jnp.exp(m2 - m1)
    w1 = g_val / (1.0 + e21)
    w2 = g_val * e21 / (1.0 + e21)
    e1 = g_idx * MOE_EPG + i1
    e2 = g_idx * MOE_EPG + i2
    xrow = lax.broadcasted_iota(jnp.int32, (ne, tm), 0)
    oh1 = xrow == e1
    oh2 = xrow == e2
    cmat = jnp.where(oh1 | oh2, 1.0, 0.0)
    ti = lax.broadcasted_iota(jnp.int32, (tm, tm), 0)
    tj = lax.broadcasted_iota(jnp.int32, (tm, tm), 1)
    su = jnp.where(ti < tj, 1.0, 0.0).astype(BF16)
    before = jnp.dot(cmat.astype(BF16), su, preferred_element_type=F32) + carry_ref[...]
    r1 = jnp.sum(jnp.where(oh1, before, 0.0), axis=0, keepdims=True)
    r2 = jnp.sum(jnp.where(oh2, before, 0.0), axis=0, keepdims=True)
    carry = carry_ref[...] + jnp.sum(cmat, axis=1, keepdims=True)
    carry_ref[...] = carry
    cnt_ref[...] = jnp.broadcast_to(carry, cnt_ref.shape)
    o_ref[...] = jnp.concatenate([e1.astype(F32), e2.astype(F32), w1, w2, r1, r2,
                                  jnp.zeros((2, tm), F32)], axis=0)


def _invert_kernel(d1_ref, d2_ref, cnt_ref, off_ref, end_ref, slot_ref, tok_ref, *, n_tok, unroll):
    ne = cnt_ref.shape[0]

    def pad_expert(e, p):
        lo = off_ref[e] + cnt_ref[e]

        def pad_row(r, p):
            slot_ref[r] = 2 * n_tok + p
            tok_ref[r] = 0
            return p + 1

        return lax.fori_loop(lo, end_ref[e], pad_row, p)

    lax.fori_loop(0, ne, pad_expert, 0)

    def tail_row(r, c):
        slot_ref[r] = 2 * n_tok
        tok_ref[r] = 0
        return c

    lax.fori_loop(end_ref[ne - 1], slot_ref.shape[0], tail_row, 0)

    def tok_block(i, c):
        for j in range(unroll):
            t = i * unroll + j
            r1, r2 = d1_ref[t], d2_ref[t]
            slot_ref[r1] = t
            slot_ref[r2] = n_tok + t
            tok_ref[r1] = t
            tok_ref[r2] = t
        return c

    lax.fori_loop(0, n_tok // unroll, tok_block, 0)


def _expert_kernel(te_ref, nv_ref, slot_ref, tok_ref, hp_ref, wg_ref, wu_ref, wd_ref, ys_ref,
                   xb0_ref, xb1_ref, yb0_ref, yb1_ref, wgb_ref, wub_ref, wdb_ref, gsem, ssem):
    i = pl.program_id(0)
    nv = nv_ref[0]
    tile = xb0_ref.shape[0]

    def gather(tile_idx, xb_ref, sem):
        for r in range(tile):
            tok = tok_ref[tile_idx * tile + r]
            pltpu.make_async_copy(hp_ref.at[pl.ds(tok, 1)], xb_ref.at[pl.ds(r, 1)], sem).start()

    def wait_gather(xb_ref, sem):
        pltpu.make_async_copy(hp_ref.at[pl.ds(0, tile)], xb_ref, sem).wait()

    def scatter(tile_idx, yb_ref, sem):
        for r in range(tile):
            s = slot_ref[tile_idx * tile + r]
            pltpu.make_async_copy(yb_ref.at[pl.ds(r, 1)], ys_ref.at[pl.ds(s, 1)], sem).start()

    def wait_scatter(yb_ref, sem):
        pltpu.make_async_copy(yb_ref, ys_ref.at[pl.ds(0, tile)], sem).wait()

    new_expert = (i == 0) | (te_ref[i] != te_ref[jnp.maximum(i - 1, 0)])

    @pl.when(new_expert & (i < nv))
    def _():
        wgb_ref[...] = wg_ref[0].astype(BF16)
        wub_ref[...] = wu_ref[0].astype(BF16)
        wdb_ref[...] = wd_ref[0].astype(BF16)

    @pl.when(i == 0)
    def _():
        gather(0, xb0_ref, gsem.at[0])

    def step(p, xb_ref, xb_next_ref, yb_ref, yb_other_ref):
        wait_gather(xb_ref, gsem.at[p])

        @pl.when(i >= 2)
        def _():
            wait_scatter(yb_ref, ssem.at[p])

        xp = xb_ref[...]
        x = jnp.concatenate([lax.bitcast_convert_type(xp & jnp.uint32(0xFFFF0000), F32).astype(BF16),
                             lax.bitcast_convert_type(xp << 16, F32).astype(BF16)], axis=1)
        gather(jnp.minimum(i + 1, nv - 1), xb_next_ref, gsem.at[1 - p])
        hg = jnp.dot(x, wgb_ref[...], preferred_element_type=F32)
        hu = jnp.dot(x, wub_ref[...], preferred_element_type=F32)
        hid = (hg * jax.nn.sigmoid(hg) * hu).astype(BF16)
        yb_ref[...] = jnp.dot(hid, wdb_ref[...], preferred_element_type=F32)
        scatter(i, yb_ref, ssem.at[p])

        @pl.when(i == nv - 1)
        def _():
            wait_gather(xb_next_ref, gsem.at[1 - p])
            wait_scatter(yb_ref, ssem.at[p])

            @pl.when(i >= 1)
            def _():
                wait_scatter(yb_other_ref, ssem.at[1 - p])

    odd = lax.rem(i, 2) == 1

    @pl.when((i < nv) & jnp.logical_not(odd))
    def _():
        step(0, xb0_ref, xb1_ref, yb0_ref, yb1_ref)

    @pl.when((i < nv) & odd)
    def _():
        step(1, xb1_ref, xb0_ref, yb1_ref, yb0_ref)


def _combine_kernel(x_ref, w_ref, gt_ref, fn_ref, ya_ref, yb_ref, o_ref, *, final_norm):
    w = w_ref[...]
    y = w[:, 0:1] * ya_ref[...] + w[:, 1:2] * yb_ref[...]
    out = x_ref[...] + gt_ref[0] * y
    if final_norm:
        ms = jnp.mean(out * out, axis=-1, keepdims=True)
        out = out * lax.rsqrt(ms + EPS) * fn_ref[...]
    o_ref[...] = out


def _hier_moe(x2d, seq, g, sh, sc, gate, r1_w, r1_b, r2_w, r2_b, wg, wu, wd, final_w):
    t, d = x2d.shape
    ne = MOE_EXPERTS
    rw = jnp.zeros((128, d), F32).at[0:MOE_GROUPS].set(r1_w.T).at[8:8 + ne].set(r2_w.T)
    rb = jnp.zeros((128, 1), F32).at[0:MOE_GROUPS, 0].set(r1_b).at[8:8 + ne, 0].set(r2_b)
    tmr = min(512, seq)
    bmap = lambda i: ((i * tmr) // seq, 0, 0)
    route, cnt, hp = pl.pallas_call(
        _router_kernel,
        out_shape=(jax.ShapeDtypeStruct((8, t), F32), jax.ShapeDtypeStruct((ne, 128), F32),
                   jax.ShapeDtypeStruct((t, d // 2), jnp.uint32)),
        grid=(t // tmr,),
        in_specs=[pl.BlockSpec((tmr, d), lambda i: (i, 0)),
                  pl.BlockSpec((1, d), lambda i: (0, 0)),
                  pl.BlockSpec((1, 1, d), bmap), pl.BlockSpec((1, 1, d), bmap),
                  pl.BlockSpec((128, d), lambda i: (0, 0)),
                  pl.BlockSpec((128, 1), lambda i: (0, 0))],
        out_specs=(pl.BlockSpec((8, tmr), lambda i: (0, i)),
                   pl.BlockSpec((ne, 128), lambda i: (0, 0)),
                   pl.BlockSpec((tmr, d // 2), lambda i: (i, 0))),
        scratch_shapes=[pltpu.VMEM((ne, 1), F32)],
        compiler_params=_cparams("arbitrary"), name="moe_router",
    )(x2d, g.reshape(1, d), sh[:, None, :], sc[:, None, :], rw, rb)

    e1 = route[0].astype(jnp.int32)
    e2 = route[1].astype(jnp.int32)
    tile = EXPERT_TILE
    counts = cnt[:, 0].astype(jnp.int32)
    padded = ((counts + tile - 1) // tile) * tile
    ends = jnp.cumsum(padded)
    offs = ends - padded
    dest1 = offs[e1] + route[4].astype(jnp.int32)
    dest2 = offs[e2] + route[5].astype(jnp.int32)
    nt = (2 * t) // tile + ne
    n_used = ends[-1] // tile
    tid = jnp.arange(nt, dtype=jnp.int32)
    tsrc = jnp.minimum(tid, n_used - 1)
    texp = jnp.minimum(jnp.sum((tsrc[:, None] * tile >= ends[None, :]).astype(jnp.int32), axis=1), ne - 1)
    wts = jnp.concatenate([route[2:4].T, jnp.zeros((t, 6), F32)], axis=1)

    smem = pl.BlockSpec(memory_space=pltpu.SMEM)
    slot, tok = pl.pallas_call(
        functools.partial(_invert_kernel, n_tok=t, unroll=8),
        out_shape=(jax.ShapeDtypeStruct((nt * tile,), jnp.int32),) * 2,
        in_specs=[smem] * 5, out_specs=(smem, smem), name="moe_invert",
    )(dest1, dest2, counts, offs, ends)

    ff = wg.shape[2]
    n_trash = ne * tile
    ys = pl.pallas_call(
        _expert_kernel,
        out_shape=jax.ShapeDtypeStruct((2 * t + n_trash, d), F32),
        grid_spec=pltpu.PrefetchScalarGridSpec(
            num_scalar_prefetch=4, grid=(nt,),
            in_specs=[pl.BlockSpec(memory_space=pl.ANY),
                      pl.BlockSpec((1, d, ff), lambda i, te, nv, sl, tk: (te[i], 0, 0)),
                      pl.BlockSpec((1, d, ff), lambda i, te, nv, sl, tk: (te[i], 0, 0)),
                      pl.BlockSpec((1, ff, d), lambda i, te, nv, sl, tk: (te[i], 0, 0))],
            out_specs=pl.BlockSpec(memory_space=pl.ANY),
            scratch_shapes=[pltpu.VMEM((tile, d // 2), jnp.uint32), pltpu.VMEM((tile, d // 2), jnp.uint32),
                            pltpu.VMEM((tile, d), F32), pltpu.VMEM((tile, d), F32),
                            pltpu.VMEM((d, ff), BF16), pltpu.VMEM((d, ff), BF16), pltpu.VMEM((ff, d), BF16),
                            pltpu.SemaphoreType.DMA((2,)), pltpu.SemaphoreType.DMA((2,))]),
        compiler_params=_cparams("arbitrary"), name="moe_experts",
    )(texp, n_used.reshape(1).astype(jnp.int32), slot, tok, hp, wg, wu, wd)

    fin = final_w is not None
    fw = (final_w if fin else jnp.ones((d,), F32)).reshape(1, d)
    tmc = min(512, seq)
    nbc = t // tmc
    return pl.pallas_call(
        functools.partial(_combine_kernel, final_norm=fin),
        out_shape=jax.ShapeDtypeStruct((t, d), F32), grid=(nbc,),
        in_specs=[pl.BlockSpec((tmc, d), lambda i: (i, 0)),
                  pl.BlockSpec((tmc, 8), lambda i: (i, 0)),
                  pl.BlockSpec((1, 1, d), lambda i: ((i * tmc) // seq, 0, 0)),
                  pl.BlockSpec((1, d), lambda i: (0, 0)),
                  pl.BlockSpec((tmc, d), lambda i: (i, 0)),
                  pl.BlockSpec((tmc, d), lambda i: (i + nbc, 0))],
        out_specs=pl.BlockSpec((tmc, d), lambda i: (i, 0)),
        compiler_params=_cparams("arbitrary"), name="moe_combine",
    )(x2d, wts, gate[:, None, :], fw, ys, ys)


def _ssd_kernel(z_ref, x_ref, b_ref, c_ref, dt_ref, cwx_ref, cwb_ref, cwc_ref, cbx_ref, cbb_ref, cbc_ref,
                dtb_ref, alog_ref, dsk_ref, nw_ref, o_ref,
                xe_ref, be_ref, ce_ref, st_ref, acs_ref, dtg_ref):
    c = pl.program_id(1)
    lq = SSD_CHUNK
    ng = SSD_GROUPS
    gw = x_ref.shape[1] // ng
    hpg = gw // SSD_HEAD_DIM
    halo = 8
    k0 = halo - (SSD_CONV - 1)

    @pl.when(c == 0)
    def _():
        xe_ref[0:halo] = jnp.zeros((halo, xe_ref.shape[1]), F32)
        be_ref[0:halo] = jnp.zeros((halo, be_ref.shape[1]), F32)
        ce_ref[0:halo] = jnp.zeros((halo, ce_ref.shape[1]), F32)
        st_ref[...] = jnp.zeros_like(st_ref)

    xe_ref[halo:halo + lq] = x_ref[...].astype(F32)
    be_ref[halo:halo + lq] = b_ref[...].astype(F32)
    ce_ref[halo:halo + lq] = c_ref[...].astype(F32)

    def conv_silu(e_ref, w_ref, bias_ref, cols):
        acc = bias_ref[:, cols]
        for k in range(SSD_CONV):
            acc = acc + e_ref[k0 + k:k0 + k + lq, cols] * w_ref[k:k + 1, cols]
        return acc * jax.nn.sigmoid(acc)

    dt = jax.nn.softplus(dt_ref[...] + dtb_ref[...])
    adt = dt * (-jnp.exp(alog_ref[...]))
    li = lax.broadcasted_iota(jnp.int32, (lq, lq), 0)
    si = lax.broadcasted_iota(jnp.int32, (lq, lq), 1)
    tri = jnp.where(si <= li, 1.0, 0.0)
    acs = jnp.dot(tri, adt, precision=HIGHEST, preferred_element_type=F32)
    lane = lax.broadcasted_iota(jnp.int32, (lq, 128), 1)
    for gi in range(ng):
        sh = (128 - gi * hpg) % 128
        acs_ref[gi] = jnp.where(lane < hpg, pltpu.roll(acs, sh, axis=1) if sh else acs, 0.0)
        dtg_ref[gi] = jnp.where(lane < hpg, pltpu.roll(dt, sh, axis=1) if sh else dt, 0.0)
    causal = si <= li
    lane_lo = lane < SSD_HEAD_DIM

    def group_body(gi, carry):
        xc0 = pl.multiple_of(gi * gw, gw)
        nc0 = pl.multiple_of(gi * SSD_STATE, SSD_STATE)
        xg = conv_silu(xe_ref, cwx_ref, cbx_ref, pl.ds(xc0, gw))
        bg = conv_silu(be_ref, cwb_ref, cbb_ref, pl.ds(nc0, SSD_STATE))
        cg = conv_silu(ce_ref, cwc_ref, cbc_ref, pl.ds(nc0, SSD_STATE))
        bt = bg.T
        cgb = cg.astype(BF16)
        cb = jnp.dot(cgb, bt.astype(BF16), preferred_element_type=F32)
        acs_g = acs_ref[gi]
        acs_t = acs_g.T
        dt_t = dtg_ref[gi].T
        xgb = xg.astype(BF16)
        ys = []
        for jp in range(hpg // 2):
            lhs_parts, w_parts, ecols, dlast = [], [], [], []
            for j in (2 * jp, 2 * jp + 1):
                e_col = jnp.broadcast_to(acs_g[:, j:j + 1], (lq, lq))
                r_row = jnp.broadcast_to(acs_t[j:j + 1, :], (lq, lq))
                dt_row = dt_t[j:j + 1, :]
                lmat = jnp.where(causal, jnp.exp(e_col - r_row), 0.0) * cb * dt_row
                lhs_parts.append(lmat.astype(BF16))
                a_last = acs_t[j:j + 1, lq - 1:lq]
                wrow = dt_row * jnp.exp(a_last - acs_t[j:j + 1, :])
                w_parts.append((bt * wrow).astype(BF16))
                ecols.append(e_col)
                dlast.append(jnp.exp(a_last))
            lhs = jnp.concatenate([jnp.concatenate(lhs_parts, axis=1),
                                   jnp.concatenate(w_parts, axis=1)], axis=0)
            xp = xgb[:, jp * 128:(jp + 1) * 128]
            zero = jnp.zeros_like(xp)
            rhs = jnp.concatenate([jnp.where(lane_lo, xp, zero), jnp.where(lane_lo, zero, xp)], axis=0)
            res = jnp.dot(lhs, rhs, preferred_element_type=F32)
            st = st_ref[gi, :, jp * 128:(jp + 1) * 128]
            yoff = jnp.dot(cgb, st.astype(BF16), preferred_element_type=F32)
            escale = jnp.exp(jnp.where(lane_lo, ecols[0], ecols[1]))
            ys.append(res[:lq] + yoff * escale)
            dl = jnp.where(lane_lo[:1], dlast[0], dlast[1])
            st_ref[gi, :, jp * 128:(jp + 1) * 128] = st * dl + res[lq:]
        y = jnp.concatenate(ys, axis=1) + dsk_ref[:, pl.ds(xc0, gw)] * xg
        zg = z_ref[:, pl.ds(xc0, gw)].astype(F32)
        y = y * (zg * jax.nn.sigmoid(zg))
        ms = jnp.mean(y * y, axis=-1, keepdims=True)
        y = y * lax.rsqrt(ms + EPS) * nw_ref[:, pl.ds(xc0, gw)]
        o_ref[:, pl.ds(xc0, gw)] = y.astype(o_ref.dtype)
        return carry

    lax.fori_loop(0, ng, group_body, 0)
    xe_ref[0:halo] = xe_ref[lq:lq + halo]
    be_ref[0:halo] = be_ref[lq:lq + halo]
    ce_ref[0:halo] = ce_ref[lq:lq + halo]


def _ssd_mixer(zx, dt_raw, bsz, seq, conv_w, conv_b, dt_bias, a_log, d_skip, norm_w):
    t = zx.shape[0]
    heads = a_log.shape[0]
    inner = heads * SSD_HEAD_DIM
    bcw = SSD_GROUPS * SSD_STATE
    lq = SSD_CHUNK
    nc = seq // lq
    pad = 128 - heads
    cw = conv_w.astype(F32)
    cwp = jnp.zeros((8, cw.shape[1]), F32).at[:SSD_CONV].set(cw)
    cb = conv_b.astype(F32).reshape(1, -1)
    row = lambda b, c: (b * nc + c, 0)
    const = lambda b, c: (0, 0)
    xb = inner // inner
    bb = (2 * inner) // bcw
    return pl.pallas_call(
        _ssd_kernel, out_shape=jax.ShapeDtypeStruct((t, inner), BF16), grid=(bsz, nc),
        in_specs=[pl.BlockSpec((lq, inner), lambda b, c: (b * nc + c, 0)),
                  pl.BlockSpec((lq, inner), lambda b, c: (b * nc + c, xb)),
                  pl.BlockSpec((lq, bcw), lambda b, c: (b * nc + c, bb)),
                  pl.BlockSpec((lq, bcw), lambda b, c: (b * nc + c, bb + 1)),
                  pl.BlockSpec((lq, 128), row),
                  pl.BlockSpec((8, inner), const), pl.BlockSpec((8, bcw), const), pl.BlockSpec((8, bcw), const),
                  pl.BlockSpec((1, inner), const), pl.BlockSpec((1, bcw), const), pl.BlockSpec((1, bcw), const),
                  pl.BlockSpec((1, 128), const), pl.BlockSpec((1, 128), const),
                  pl.BlockSpec((1, inner), const), pl.BlockSpec((1, inner), const)],
        out_specs=pl.BlockSpec((lq, inner), row),
        scratch_shapes=[pltpu.VMEM((lq + 8, inner), F32), pltpu.VMEM((lq + 8, bcw), F32),
                        pltpu.VMEM((lq + 8, bcw), F32),
                        pltpu.VMEM((SSD_GROUPS, SSD_STATE, inner // SSD_GROUPS), F32),
                        pltpu.VMEM((SSD_GROUPS, lq, 128), F32), pltpu.VMEM((SSD_GROUPS, lq, 128), F32)],
        compiler_params=_cparams("arbitrary", "arbitrary"), name="ssd_mixer",
    )(zx, zx, zx, zx, dt_raw,
      cwp[:, :inner], cwp[:, inner:inner + bcw], cwp[:, inner + bcw:],
      cb[:, :inner], cb[:, inner:inner + bcw], cb[:, inner + bcw:],
      jnp.pad(dt_bias.astype(F32), (0, pad)).reshape(1, 128),
      jnp.pad(a_log.astype(F32), (0, pad)).reshape(1, 128),
      jnp.repeat(d_skip.astype(F32), SSD_HEAD_DIM).reshape(1, inner),
      norm_w.astype(F32).reshape(1, inner))


def kernel(x, c, positions, ada_w, ada_b, norm_mix, norm_ffn, mix_a_w_in, s5_lam_re, s5_lam_im, s5_log_dt, s5_b_re, s5_b_im, s5_c_re, s5_c_im, s5_d, s5_w_glu, attn_sinks, mix_a_w_out, ssd_w_in, ssd_conv_w, ssd_conv_b, ssd_dt_bias, ssd_a_log, ssd_d, ssd_norm_w, ssd_w_out, moe_r1_w, moe_r1_b, moe_r2_w, moe_r2_b, moe_w_gate, moe_w_up, moe_w_down, final_norm):
    bsz, seq, d = x.shape
    depth = ada_w.shape[0]
    t = bsz * seq
    xf = x.reshape(t, d).astype(F32)
    mod = _adaln(c.astype(F32), ada_w, ada_b)
    cos_t, sin_t = _rope_tables(positions)

    for i in range(depth):
        sh1, sc1, g1, sh2, sc2, g2 = jnp.split(mod[i], 6, axis=-1)
        j = i // 2
        if i % 2 == 0:
            s5w = s5_lam_re.shape[1] * S5_GROUP
            w_in = mix_a_w_in[j]
            u, qkv = _inproj0(xf, seq, norm_mix[i], sh1, sc1, w_in[:, :s5w], w_in[:, s5w:], tm=512)
            params = _s5_params(s5_lam_re[j], s5_lam_im[j], s5_log_dt[j], s5_b_re[j], s5_b_im[j],
                                s5_c_re[j], s5_c_im[j], s5_d[j], seq)
            y_s5 = _s5_mixer(u, bsz, seq, params)
            y_glu = _gelu_glu(y_s5, s5_w_glu[j], tm=1024, tn=512)
            wq = attn_sinks.shape[1] * HEAD_DIM
            wk = KV_HEADS * HEAD_DIM
            y_att = _attention(qkv, bsz, seq, cos_t, sin_t, attn_sinks[j],
                               q_col=0, k_col=wq // wk, v_col=wq // wk + 1)
            xf = _matmul_residual([y_glu, y_att], mix_a_w_out[j], xf, seq, g1, tm=1024, tn=512)
        else:
            heads = ssd_a_log.shape[1]
            inner = heads * SSD_HEAD_DIM
            main = inner + inner + 2 * SSD_GROUPS * SSD_STATE
            w_in = ssd_w_in[j]
            w_dt = jnp.pad(w_in[:, main:], ((0, 0), (0, 128 - heads))).astype(BF16)
            zx, dt_raw = _inproj1(xf, seq, norm_mix[i], sh1, sc1, w_in, main, w_dt, tm=1024, tn=512)
            y = _ssd_mixer(zx, dt_raw, bsz, seq, ssd_conv_w[j], ssd_conv_b[j], ssd_dt_bias[j],
                           ssd_a_log[j], ssd_d[j], ssd_norm_w[j])
            xf = _matmul_residual([y], ssd_w_out[j], xf, seq, g1, tm=1024, tn=512)
        xf = _hier_moe(xf, seq, norm_ffn[i], sh2, sc2, g2, moe_r1_w[i], moe_r1_b[i], moe_r2_w[i], moe_r2_b[i],
                       moe_w_gate[i], moe_w_up[i], moe_w_down[i],
                       final_norm if i == depth - 1 else None)
    return xf.reshape(bsz, seq, d).astype(x.dtype)
```

```python
import functools
import math

import jax
import jax.numpy as jnp
from jax import lax
from jax.experimental import pallas as pl
from jax.experimental.pallas import tpu as pltpu

F32 = jnp.float32
BF16 = jnp.bfloat16
HIGHEST = lax.Precision.HIGHEST

EPS = 1e-6
S5_GROUP = 16
S5_STATE = 64
S5_CHUNK = 16
HEAD_DIM = 64
KV_HEADS = 2
WINDOW = 128
ROPE_DIM = 16
ROPE_THETA = 500000.0
SSD_HEAD_DIM = 64
SSD_GROUPS = 8
SSD_STATE = 128
SSD_CONV = 4
SSD_CHUNK = 128
MOE_GROUPS = 4
MOE_EPG = 8
MOE_EXPERTS = MOE_GROUPS * MOE_EPG
EXPERT_TILE = 256
NEG_BIG = -1e30
VMEM_LIMIT = 56 * 1024 * 1024


def _cparams(*sem):
    return pltpu.CompilerParams(dimension_semantics=sem, vmem_limit_bytes=VMEM_LIMIT)


def _modnorm(x, g, sh, sc):
    ms = jnp.mean(x * x, axis=-1, keepdims=True)
    return (x * lax.rsqrt(ms + EPS) * g) * (1.0 + sc) + sh


def _adaln_kernel(c_ref, w_ref, b_ref, o_ref):
    c = c_ref[...]
    a = c * jax.nn.sigmoid(c)
    o_ref[0] = jnp.dot(a, w_ref[0], precision=HIGHEST, preferred_element_type=F32) + b_ref[0]


def _adaln(c, ada_w, ada_b):
    depth, d, n = ada_w.shape
    bsz = c.shape[0]
    rows = 8
    cp = jnp.zeros((rows, d), F32).at[:bsz].set(c)
    tn = 1024
    out = pl.pallas_call(
        _adaln_kernel,
        out_shape=jax.ShapeDtypeStruct((depth, rows, n), F32),
        grid=(depth, n // tn),
        in_specs=[pl.BlockSpec((rows, d), lambda l, j: (0, 0)),
                  pl.BlockSpec((1, d, tn), lambda l, j: (l, 0, j)),
                  pl.BlockSpec((1, 1, tn), lambda l, j: (l, 0, j))],
        out_specs=pl.BlockSpec((1, rows, tn), lambda l, j: (l, 0, j)),
        compiler_params=_cparams("arbitrary", "arbitrary"),
        name="adaln",
    )(cp, ada_w, ada_b.reshape(depth, 1, n))
    return out[:, :bsz]


def _inproj1_kernel(x_ref, g_ref, sh_ref, sc_ref, w_ref, w2_ref, o_ref, o2_ref, h_ref):
    @pl.when(pl.program_id(1) == 0)
    def _():
        h = _modnorm(x_ref[...], g_ref[...], sh_ref[0], sc_ref[0]).astype(BF16)
        h_ref[...] = h
        o2_ref[...] = jnp.dot(h, w2_ref[...], preferred_element_type=F32)

    o_ref[...] = jnp.dot(h_ref[...], w_ref[...].astype(BF16), preferred_element_type=F32).astype(o_ref.dtype)


def _inproj1(x2d, seq, g, sh, sc, w, n_main, w2, tm, tn):
    t, d = x2d.shape
    tm = min(tm, seq)
    n2 = w2.shape[1]
    bmap = lambda i, j: ((i * tm) // seq, 0, 0)
    return pl.pallas_call(
        _inproj1_kernel,
        out_shape=(jax.ShapeDtypeStruct((t, n_main), BF16), jax.ShapeDtypeStruct((t, n2), F32)),
        grid=(t // tm, n_main // tn),
        in_specs=[pl.BlockSpec((tm, d), lambda i, j: (i, 0)),
                  pl.BlockSpec((1, d), lambda i, j: (0, 0)),
                  pl.BlockSpec((1, 1, d), bmap), pl.BlockSpec((1, 1, d), bmap),
                  pl.BlockSpec((d, tn), lambda i, j: (0, j)),
                  pl.BlockSpec((d, n2), lambda i, j: (0, 0))],
        out_specs=(pl.BlockSpec((tm, tn), lambda i, j: (i, j)), pl.BlockSpec((tm, n2), lambda i, j: (i, 0))),
        scratch_shapes=[pltpu.VMEM((tm, d), BF16)],
        compiler_params=_cparams("arbitrary", "arbitrary"), name="inproj1",
    )(x2d, g.reshape(1, d), sh[:, None, :], sc[:, None, :], w, w2)


def _mmres_kernel(*refs, n_in):
    a_refs, w_refs = refs[:n_in], refs[n_in:2 * n_in]
    x_ref, gt_ref, o_ref = refs[2 * n_in:]
    acc = jnp.dot(a_refs[0][...], w_refs[0][...].astype(BF16), preferred_element_type=F32)
    for a_ref, w_ref in zip(a_refs[1:], w_refs[1:]):
        acc = acc + jnp.dot(a_ref[...], w_ref[...].astype(BF16), preferred_element_type=F32)
    o_ref[...] = x_ref[...] + gt_ref[0] * acc


def _matmul_residual(a_list, w, x2d, seq, gate, tm, tn):
    t, n = x2d.shape
    tm = min(tm, seq)
    in_specs, row = [], 0
    for a in a_list:
        in_specs.append(pl.BlockSpec((tm, a.shape[1]), lambda i, j: (i, 0)))
    for a in a_list:
        k = a.shape[1]
        in_specs.append(pl.BlockSpec((k, tn), functools.partial(lambda i, j, rb: (rb, j), rb=row // k)))
        row += k
    in_specs.append(pl.BlockSpec((tm, tn), lambda i, j: (i, j)))
    in_specs.append(pl.BlockSpec((1, 1, tn), lambda i, j: ((i * tm) // seq, 0, j)))
    return pl.pallas_call(
        functools.partial(_mmres_kernel, n_in=len(a_list)),
        out_shape=jax.ShapeDtypeStruct((t, n), F32), grid=(t // tm, n // tn),
        in_specs=in_specs, out_specs=pl.BlockSpec((tm, tn), lambda i, j: (i, j)),
        compiler_params=_cparams("arbitrary", "arbitrary"), name="matmul_residual",
    )(*a_list, *([w] * len(a_list)), x2d, gate[:, None, :])


def _inproj0_kernel(x_ref, g_ref, sh_ref, sc_ref, wu_ref, wa_ref, u_ref, a_ref, wub_ref, wab_ref):
    @pl.when(pl.program_id(0) == 0)
    def _():
        wub_ref[...] = wu_ref[...].astype(BF16)
        wab_ref[...] = wa_ref[...].astype(BF16)

    h = _modnorm(x_ref[...], g_ref[...], sh_ref[0], sc_ref[0]).astype(BF16)
    u_ref[...] = jnp.dot(h, wub_ref[...], preferred_element_type=F32)
    a_ref[...] = jnp.dot(h, wab_ref[...], preferred_element_type=F32).astype(a_ref.dtype)


def _inproj0(x2d, seq, g, sh, sc, w_u, w_a, tm):
    t, d = x2d.shape
    tm = min(tm, seq)
    nu, na = w_u.shape[1], w_a.shape[1]
    bmap = lambda i: ((i * tm) // seq, 0, 0)
    return pl.pallas_call(
        _inproj0_kernel,
        out_shape=(jax.ShapeDtypeStruct((t, nu), F32), jax.ShapeDtypeStruct((t, na), BF16)),
        grid=(t // tm,),
        in_specs=[pl.BlockSpec((tm, d), lambda i: (i, 0)),
                  pl.BlockSpec((1, d), lambda i: (0, 0)),
                  pl.BlockSpec((1, 1, d), bmap), pl.BlockSpec((1, 1, d), bmap),
                  pl.BlockSpec((d, nu), lambda i: (0, 0), pipeline_mode=pl.Buffered(1)),
                  pl.BlockSpec((d, na), lambda i: (0, 0), pipeline_mode=pl.Buffered(1))],
        out_specs=(pl.BlockSpec((tm, nu), lambda i: (i, 0)), pl.BlockSpec((tm, na), lambda i: (i, 0))),
        scratch_shapes=[pltpu.VMEM((d, nu), BF16), pltpu.VMEM((d, na), BF16)],
        compiler_params=_cparams("arbitrary"), name="inproj0",
    )(x2d, g.reshape(1, d), sh[:, None, :], sc[:, None, :], w_u, w_a)


S5_SLAB = 8
S5_SEGS = 8


def _s5_params(lam_re, lam_im, log_dt, b_re, b_im, c_re, c_im, d_skip, seq):
    g, n = lam_re.shape
    p = S5_GROUP
    lc = S5_CHUNK
    gs = S5_SLAB
    nj = g // gs
    lam = lax.complex(lam_re.astype(F32), lam_im.astype(F32))
    dt = jnp.exp(log_dt.astype(F32))[:, None]
    ld = lam * dt
    lam_bar = jnp.exp(ld)
    bmat = lax.complex(b_re.astype(F32), b_im.astype(F32))
    b_bar = ((lam_bar - 1.0) / lam)[..., None] * bmat
    cmat = lax.complex(c_re.astype(F32), c_im.astype(F32))
    ks = jnp.arange(lc, dtype=F32)
    pw = jnp.exp(ld[:, None, :] * ks[None, :, None])
    pw1 = jnp.exp(ld[:, None, :] * (ks[None, :, None] + 1.0))
    kk = jnp.real(jnp.einsum('gpn,gkn,gnq->gkpq', cmat, pw, b_bar, precision=HIGHEST))
    eg = jnp.eye(gs, dtype=F32)
    ex_p = jnp.einsum('ab,pr->apbr', eg, jnp.eye(p, dtype=F32)).reshape(gs, p, gs * p)
    ex_n = jnp.einsum('ab,nr->anbr', eg, jnp.eye(n, dtype=F32)).reshape(gs, n, gs * n)
    kq_ = kk.transpose(0, 1, 3, 2).reshape(nj, gs, lc, p, p)
    kexp = jnp.einsum('jakqp,apc->jkaqc', kq_, ex_p, precision=HIGHEST).reshape(nj, lc, gs * p, gs * p)
    kexp = kexp.at[:, 0].add(jnp.eye(gs * p, dtype=F32)[None] * d_skip.astype(F32).reshape(nj, 1, gs * p))
    kexp = kexp.astype(BF16)
    wb = (pw[:, ::-1, :, None] * b_bar[:, None, :, :]).transpose(0, 1, 3, 2)
    wb = wb.reshape(nj, gs, lc, p, n)
    wb_slab = jnp.concatenate(
        [jnp.einsum('jasqn,anc->jsaqc', part.astype(BF16), ex_n.astype(BF16)).reshape(nj, lc * gs * p, gs * n)
         for part in (jnp.real(wb), jnp.imag(wb))], axis=-1).astype(BF16)
    cl = (cmat[:, None, :, :] * pw1[:, :, None, :]).transpose(0, 3, 1, 2)
    cl = cl.reshape(nj, gs, n, lc, p)
    woff_slab = jnp.concatenate(
        [jnp.einsum('janlp,apc->janlc', part.astype(BF16), ex_p.astype(BF16)).reshape(nj, gs * n, lc * gs * p)
         for part in (jnp.real(cl), -jnp.imag(cl))], axis=1).astype(BF16)
    seglen = (seq // lc) // S5_SEGS
    kq = jnp.arange(seglen + 1, dtype=F32) * float(lc)
    pq = jnp.exp(ld[:, None, :] * kq[None, :, None]).reshape(nj, gs, seglen + 1, n)
    pq = pq.transpose(0, 2, 1, 3).reshape(nj, seglen + 1, gs * n)
    return kexp, wb_slab, woff_slab, jnp.real(pq), jnp.imag(pq)


def _s5_kernel(u_ref, k_ref, wb_ref, woff_ref, pre_ref, pim_ref, y_ref, sloc_ref, sin_ref, m_ref):
    lc = S5_CHUNK
    w = u_ref.shape[1]

    @pl.when((pl.program_id(0) == 0) & (pl.program_id(1) == 0))
    def _():
        m_ref[...] = jnp.zeros_like(m_ref)

    @pl.when(pl.program_id(1) == 0)
    def _():
        for s in range(lc):
            for l in range(s, lc):
                m_ref[s * w:(s + 1) * w, l * w:(l + 1) * w] = k_ref[0, l - s]

    nc = u_ref.shape[0] // lc
    hw = wb_ref.shape[2] // 2
    seglen = nc // S5_SEGS
    u_all = jnp.concatenate([u_ref[pl.ds(s, nc, stride=lc), :] for s in range(lc)], axis=1).astype(BF16)
    nq = hw // w
    sloc = jnp.dot(u_all, wb_ref[0], preferred_element_type=F32)
    for q in range(2 * nq):
        sloc_ref[q] = sloc[:, q * w:(q + 1) * w]
    a_re, a_im = pre_ref[0, 1:2], pim_ref[0, 1:2]

    def put(rows, re, im):
        for q in range(nq):
            sin_ref[q, rows, :] = re[:, q * w:(q + 1) * w]
            sin_ref[nq + q, rows, :] = im[:, q * w:(q + 1) * w]

    def get(ref, rows):
        return (jnp.concatenate([ref[q, rows, :] for q in range(nq)], axis=1),
                jnp.concatenate([ref[nq + q, rows, :] for q in range(nq)], axis=1))

    def seg_step(k, st):
        sre, sim = st
        rows = pl.ds(k, S5_SEGS, stride=seglen)
        put(rows, sre, sim)
        lre, lim = get(sloc_ref, rows)
        return a_re * sre - a_im * sim + lre, a_re * sim + a_im * sre + lim

    z = jnp.zeros((S5_SEGS, hw), F32)
    ere, eim = lax.fori_loop(0, seglen, seg_step, (z, z))
    g_re, g_im = pre_ref[0, seglen:seglen + 1], pim_ref[0, seglen:seglen + 1]
    cre, cim = jnp.zeros((1, hw), F32), jnp.zeros((1, hw), F32)
    pre, pim = pre_ref[0, 0:seglen], pim_ref[0, 0:seglen]
    for sg in range(1, S5_SEGS):
        cre, cim = (g_re * cre - g_im * cim + ere[sg - 1:sg], g_re * cim + g_im * cre + eim[sg - 1:sg])
        rows = slice(sg * seglen, (sg + 1) * seglen)
        cur_re, cur_im = get(sin_ref, rows)
        put(rows, cur_re + pre * cre - pim * cim, cur_im + pre * cim + pim * cre)
    s_in = jnp.concatenate([sin_ref[q] for q in range(2 * nq)], axis=1).astype(BF16)
    for lb in range(lc // 2):
        c0, c1 = 2 * lb * w, (2 * lb + 2) * w
        y2 = (jnp.dot(u_all[:, :c1], m_ref[:c1, c0:c1], preferred_element_type=F32)
              + jnp.dot(s_in, woff_ref[0, :, c0:c1], preferred_element_type=F32))
        y_ref[pl.ds(2 * lb, nc, stride=lc), :] = y2[:, :w]
        y_ref[pl.ds(2 * lb + 1, nc, stride=lc), :] = y2[:, w:]


def _s5_mixer(u, bsz, seq, params):
    kexp, wb_slab, woff_slab, pq_re, pq_im = params
    nj, km, sw = wb_slab.shape
    w = km // S5_CHUNK
    nc = seq // S5_CHUNK
    t = u.shape[0]
    const = lambda j, b: (j, 0, 0)
    return pl.pallas_call(
        _s5_kernel, out_shape=jax.ShapeDtypeStruct((t, nj * w), F32), grid=(nj, bsz),
        in_specs=[pl.BlockSpec((seq, w), lambda j, b: (b, j)),
                  pl.BlockSpec((1, S5_CHUNK, w, w), lambda j, b: (j, 0, 0, 0)),
                  pl.BlockSpec((1, km, sw), const),
                  pl.BlockSpec((1, sw, km), const),
                  pl.BlockSpec((1,) + pq_re.shape[1:], const),
                  pl.BlockSpec((1,) + pq_im.shape[1:], const)],
        out_specs=pl.BlockSpec((seq, w), lambda j, b: (b, j)),
        scratch_shapes=[pltpu.VMEM((sw // w, nc, w), F32), pltpu.VMEM((sw // w, nc, w), F32),
                        pltpu.VMEM((km, km), BF16)],
        compiler_params=_cparams("arbitrary", "arbitrary"), name="s5_mixer",
    )(u, kexp, wb_slab, woff_slab, pq_re, pq_im)


def _glu_kernel(y_ref, wv_ref, wg_ref, o_ref, a_ref):
    @pl.when(pl.program_id(1) == 0)
    def _():
        a_ref[...] = jax.nn.gelu(y_ref[...].astype(F32), approximate=True).astype(BF16)

    a = a_ref[...]
    val = jnp.dot(a, wv_ref[...].astype(BF16), preferred_element_type=F32)
    gate = jnp.dot(a, wg_ref[...].astype(BF16), preferred_element_type=F32)
    o_ref[...] = (val * jax.nn.sigmoid(gate)).astype(o_ref.dtype)


def _gelu_glu(y, w, tm, tn):
    t, k = y.shape
    n = w.shape[1] // 2
    tm = min(tm, t)
    nb = n // tn
    return pl.pallas_call(
        _glu_kernel, out_shape=jax.ShapeDtypeStruct((t, n), BF16), grid=(t // tm, n // tn),
        in_specs=[pl.BlockSpec((tm, k), lambda i, j: (i, 0)),
                  pl.BlockSpec((k, tn), lambda i, j: (0, j)),
                  pl.BlockSpec((k, tn), lambda i, j: (0, j + nb))],
        out_specs=pl.BlockSpec((tm, tn), lambda i, j: (i, j)),
        scratch_shapes=[pltpu.VMEM((tm, k), BF16)],
        compiler_params=_cparams("arbitrary", "arbitrary"), name="gelu_glu",
    )(y, w, w)


def _rope(x, cos, sin):
    w = x.shape[1]
    half = ROPE_DIM // 2
    d = lax.broadcasted_iota(jnp.int32, x.shape, 1) % HEAD_DIM
    partner = jnp.where(d < half, pltpu.roll(x, w - half, axis=1), pltpu.roll(x, half, axis=1))
    return x * cos + partner * sin


def _attn_kernel(sink_ref, q_ref, kc_ref, kp_ref, vc_ref, vp_ref,
                 cq_ref, sq_ref, cp_ref, sp_ref, o_ref):
    n = pl.program_id(1)
    wq = q_ref.shape[1]
    hq = wq // HEAD_DIM
    grp = hq // KV_HEADS
    cos_c, sin_c = cq_ref[...], sq_ref[...]
    reps = wq // cos_c.shape[1]
    q = _rope(q_ref[...].astype(F32),
              jnp.concatenate([cos_c] * reps, axis=1), jnp.concatenate([sin_c] * reps, axis=1))
    q = (q * (HEAD_DIM ** -0.5)).astype(BF16)
    kc = _rope(kc_ref[...].astype(F32), cos_c, sin_c).astype(BF16)
    kp = _rope(kp_ref[...].astype(F32), cp_ref[...], sp_ref[...]).astype(BF16)
    kk = jnp.concatenate([kp, kc], axis=0)
    vv = jnp.concatenate([vp_ref[...], vc_ref[...]], axis=0)
    qi = lax.broadcasted_iota(jnp.int32, (WINDOW, 2 * WINDOW), 0)
    kj = lax.broadcasted_iota(jnp.int32, (WINDOW, 2 * WINDOW), 1)
    first = jnp.where(n > 0, 0, WINDOW)
    valid = (kj > qi) & (kj <= qi + WINDOW) & (kj >= first)
    outs = []
    for h in range(hq):
        kvh = h // grp
        qh = q[:, h * HEAD_DIM:(h + 1) * HEAD_DIM]
        kh = kk[:, kvh * HEAD_DIM:(kvh + 1) * HEAD_DIM]
        vh = vv[:, kvh * HEAD_DIM:(kvh + 1) * HEAD_DIM]
        s = lax.dot_general(qh, kh, (((1,), (1,)), ((), ())), preferred_element_type=F32)
        s = jnp.where(valid, s, NEG_BIG)
        sink = sink_ref[h]
        mx = jnp.maximum(jnp.max(s, axis=-1, keepdims=True), sink)
        p = jnp.exp(s - mx)
        den = jnp.sum(p, axis=-1, keepdims=True) + jnp.exp(sink - mx)
        o = jnp.dot(p.astype(BF16), vh, preferred_element_type=F32)
        outs.append(o / den)
    o_ref[...] = jnp.concatenate(outs, axis=1).astype(o_ref.dtype)


def _attention(proj, bsz, seq, cos_t, sin_t, sinks, q_col, k_col, v_col):
    t = proj.shape[0]
    nb = seq // WINDOW
    wq = sinks.shape[0] * HEAD_DIM
    wk = KV_HEADS * HEAD_DIM
    cur = lambda b, n, s: (b * nb + n, 0)
    prv = lambda b, n, s: (b * nb + jnp.maximum(n - 1, 0), 0)
    return pl.pallas_call(
        _attn_kernel, out_shape=jax.ShapeDtypeStruct((t, wq), BF16),
        grid_spec=pltpu.PrefetchScalarGridSpec(
            num_scalar_prefetch=1, grid=(bsz, nb),
            in_specs=[pl.BlockSpec((WINDOW, wq), lambda b, n, s: (b * nb + n, q_col)),
                      pl.BlockSpec((WINDOW, wk), lambda b, n, s: (b * nb + n, k_col)),
                      pl.BlockSpec((WINDOW, wk), lambda b, n, s: (b * nb + jnp.maximum(n - 1, 0), k_col)),
                      pl.BlockSpec((WINDOW, wk), lambda b, n, s: (b * nb + n, v_col)),
                      pl.BlockSpec((WINDOW, wk), lambda b, n, s: (b * nb + jnp.maximum(n - 1, 0), v_col)),
                      pl.BlockSpec((WINDOW, wk), cur), pl.BlockSpec((WINDOW, wk), cur),
                      pl.BlockSpec((WINDOW, wk), prv), pl.BlockSpec((WINDOW, wk), prv)],
            out_specs=pl.BlockSpec((WINDOW, wq), lambda b, n, s: (b * nb + n, 0))),
        compiler_params=_cparams("arbitrary", "arbitrary"), name="swa_attention",
    )(sinks.astype(F32), proj, proj, proj, proj, proj, cos_t, sin_t, cos_t, sin_t)


def _rope_tables(positions):
    half = ROPE_DIM // 2
    inv = 1.0 / (ROPE_THETA ** (jnp.arange(0, ROPE_DIM, 2, dtype=F32) / ROPE_DIM))
    ang = positions.astype(F32).reshape(-1, 1) * inv[None, :]
    cos, sin = jnp.cos(ang), jnp.sin(ang)
    t = ang.shape[0]
    ones = jnp.ones((t, HEAD_DIM - ROPE_DIM), F32)
    cos_h = jnp.concatenate([cos, cos, ones], axis=1)
    sin_h = jnp.concatenate([-sin, sin, 0.0 * ones], axis=1)
    return jnp.tile(cos_h, (1, KV_HEADS)), jnp.tile(sin_h, (1, KV_HEADS))


def _router_kernel(x_ref, g_ref, sh_ref, sc_ref, rwh_ref, rwl_ref, rb_ref, o_ref, cnt_ref, hp_ref, carry_ref):
    i = pl.program_id(0)
    tm = x_ref.shape[0]
    ne = MOE_EXPERTS

    @pl.when(i == 0)
    def _():
        carry_ref[...] = jnp.zeros_like(carry_ref)

    h = _modnorm(x_ref[...], g_ref[...], sh_ref[0], sc_ref[0])
    half = h.shape[1] // 2
    hi = lax.bitcast_convert_type(h[:, :half].astype(BF16).astype(F32), jnp.uint32)
    lo = lax.bitcast_convert_type(h[:, half:].astype(BF16).astype(F32), jnp.uint32)
    hp_ref[...] = hi | (lo >> 16)
    nt_dims = (((1,), (1,)), ((), ()))
    h_hi = h.astype(BF16)
    h_lo = (h - h_hi.astype(F32)).astype(BF16)
    lt = (lax.dot_general(rwh_ref[...], h_hi, nt_dims, preferred_element_type=F32)
          + lax.dot_general(rwh_ref[...], h_lo, nt_dims, preferred_element_type=F32)
          + lax.dot_general(rwl_ref[...], h_hi, nt_dims, preferred_element_type=F32)) + rb_ref[...]
    gl = lt[0:MOE_GROUPS]
    gm = jnp.max(gl, axis=0, keepdims=True)
    gsum = jnp.sum(jnp.exp(gl - gm), axis=0, keepdims=True)
    g_val = 1.0 / gsum
    grow = lax.broadcasted_iota(jnp.int32, gl.shape, 0)
    g_idx = jnp.min(jnp.where(gl == gm, grow, MOE_GROUPS), axis=0, keepdims=True)
    sel = jnp.zeros((MOE_EPG, tm), F32)
    for gi in range(MOE_GROUPS):
        sel = jnp.where(g_idx == gi, lt[8 + gi * MOE_EPG: 8 + (gi + 1) * MOE_EPG], sel)
    erow = lax.broadcasted_iota(jnp.int32, sel.shape, 0)
    m1 = jnp.max(sel, axis=0, keepdims=True)
    i1 = jnp.min(jnp.where(sel == m1, erow, MOE_EPG), axis=0, keepdims=True)
    sel2 = jnp.where(erow == i1, -jnp.inf, sel)
    m2 = jnp.max(sel2, axis=0, keepdims=True)
    i2 = jnp.min(jnp.where(sel2 == m2, erow, MOE_EPG), axis=0, keepdims=True)
    e21 = jnp.exp(m2 - m1)
    w1 = g_val / (1.0 + e21)
    w2 = g_val * e21 / (1.0 + e21)
    e1 = g_idx * MOE_EPG + i1
    e2 = g_idx * MOE_EPG + i2
    xrow = lax.broadcasted_iota(jnp.int32, (ne, tm), 0)
    oh1 = xrow == e1
    oh2 = xrow == e2
    cmat = jnp.where(oh1 | oh2, 1.0, 0.0)
    ti = lax.broadcasted_iota(jnp.int32, (tm, tm), 0)
    tj = lax.broadcasted_iota(jnp.int32, (tm, tm), 1)
    su = jnp.where(ti < tj, 1.0, 0.0).astype(BF16)
    before = jnp.dot(cmat.astype(BF16), su, preferred_element_type=F32) + carry_ref[...]
    r1 = jnp.sum(jnp.where(oh1, before, 0.0), axis=0, keepdims=True)
    r2 = jnp.sum(jnp.where(oh2, before, 0.0), axis=0, keepdims=True)
    carry = carry_ref[...] + jnp.sum(cmat, axis=1, keepdims=True)
    carry_ref[...] = carry
    cnt_ref[...] = jnp.broadcast_to(carry, cnt_ref.shape)
    o_ref[...] = jnp.concatenate([e1.astype(F32), e2.astype(F32), w1, w2, r1, r2,
                                  jnp.zeros((2, tm), F32)], axis=0)


def _invert_kernel(d1_ref, d2_ref, cnt_ref, off_ref, end_ref, slot_ref, *, n_tok, unroll):
    ne = cnt_ref.shape[0]

    def pad_expert(e, p):
        lo = off_ref[e] + cnt_ref[e]

        def pad_row(r, p):
            slot_ref[r] = 2 * n_tok + p
            return p + 1

        return lax.fori_loop(lo, end_ref[e], pad_row, p)

    lax.fori_loop(0, ne, pad_expert, 0)

    def tok_block(i, c):
        for j in range(unroll):
            t = i * unroll + j
            slot_ref[d1_ref[t]] = t
            slot_ref[d2_ref[t]] = n_tok + t
        return c

    lax.fori_loop(0, n_tok // unroll, tok_block, 0)


def _expert_kernel(te_ref, nv_ref, slot_ref, hp_ref, wg_ref, wu_ref, wd_ref, ys_ref,
                   xb0_ref, xb1_ref, yb0_ref, yb1_ref, wgb_ref, wub_ref, wdb_ref, gsem, ssem, *, n_tok):
    i = pl.program_id(0)
    nv = nv_ref[0]
    tile = xb0_ref.shape[0]

    def gather(tile_idx, xb_ref, sem):
        for r in range(tile):
            s = slot_ref[tile_idx * tile + r]
            tok = jnp.where(s >= 2 * n_tok, 0, jnp.where(s >= n_tok, s - n_tok, s))
            pltpu.make_async_copy(hp_ref.at[pl.ds(tok, 1)], xb_ref.at[pl.ds(r, 1)], sem).start()

    def wait_gather(xb_ref, sem):
        pltpu.make_async_copy(hp_ref.at[pl.ds(0, tile)], xb_ref, sem).wait()

    def scatter(tile_idx, yb_ref, sem):
        for r in range(tile):
            s = slot_ref[tile_idx * tile + r]
            pltpu.make_async_copy(yb_ref.at[pl.ds(r, 1)], ys_ref.at[pl.ds(s, 1)], sem).start()

    def wait_scatter(yb_ref, sem):
        pltpu.make_async_copy(yb_ref, ys_ref.at[pl.ds(0, tile)], sem).wait()

    new_expert = (i == 0) | (te_ref[i] != te_ref[jnp.maximum(i - 1, 0)])

    @pl.when(new_expert & (i < nv))
    def _():
        wgb_ref[...] = wg_ref[0, 0].astype(BF16)
        wub_ref[...] = wu_ref[0, 0].astype(BF16)
        wdb_ref[...] = wd_ref[0, 0].astype(BF16)

    @pl.when(i == 0)
    def _():
        gather(0, xb0_ref, gsem.at[0])

    def step(p, xb_ref, xb_next_ref, yb_ref, yb_other_ref):
        wait_gather(xb_ref, gsem.at[p])

        @pl.when(i >= 2)
        def _():
            wait_scatter(yb_ref, ssem.at[p])

        xp = xb_ref[...]
        x = jnp.concatenate([lax.bitcast_convert_type(xp & jnp.uint32(0xFFFF0000), F32).astype(BF16),
                             lax.bitcast_convert_type(xp << 16, F32).astype(BF16)], axis=1)
        gather(jnp.minimum(i + 1, nv - 1), xb_next_ref, gsem.at[1 - p])
        hg = jnp.dot(x, wgb_ref[...], preferred_element_type=F32)
        hu = jnp.dot(x, wub_ref[...], preferred_element_type=F32)
        hid = (hg * jax.nn.sigmoid(hg) * hu).astype(BF16)
        yb_ref[...] = jnp.dot(hid, wdb_ref[...], preferred_element_type=F32)
        scatter(i, yb_ref, ssem.at[p])

        @pl.when(i == nv - 1)
        def _():
            wait_gather(xb_next_ref, gsem.at[1 - p])
            wait_scatter(yb_ref, ssem.at[p])

            @pl.when(i >= 1)
            def _():
                wait_scatter(yb_other_ref, ssem.at[1 - p])

    odd = lax.rem(i, 2) == 1

    @pl.when((i < nv) & jnp.logical_not(odd))
    def _():
        step(0, xb0_ref, xb1_ref, yb0_ref, yb1_ref)

    @pl.when((i < nv) & odd)
    def _():
        step(1, xb1_ref, xb0_ref, yb1_ref, yb0_ref)


def _combine_kernel(x_ref, w_ref, gt_ref, fn_ref, ya_ref, yb_ref, o_ref, *, final_norm):
    w = w_ref[...]
    y = w[:, 0:1] * ya_ref[...] + w[:, 1:2] * yb_ref[...]
    out = x_ref[...] + gt_ref[0] * y
    if final_norm:
        ms = jnp.mean(out * out, axis=-1, keepdims=True)
        out = out * lax.rsqrt(ms + EPS) * fn_ref[...]
    o_ref[...] = out


def _hier_moe(x2d, seq, g, sh, sc, gate, r1_w, r1_b, r2_w, r2_b, wg, wu, wd, layer, final_w):
    t, d = x2d.shape
    ne = MOE_EXPERTS
    rw = jnp.zeros((128, d), F32).at[0:MOE_GROUPS].set(r1_w.T).at[8:8 + ne].set(r2_w.T)
    rb = jnp.zeros((128, 1), F32).at[0:MOE_GROUPS, 0].set(r1_b).at[8:8 + ne, 0].set(r2_b)
    rw_hi = rw.astype(BF16)
    rw_lo = (rw - rw_hi.astype(F32)).astype(BF16)
    tmr = min(512, seq)
    bmap = lambda i: ((i * tmr) // seq, 0, 0)
    route, cnt, hp = pl.pallas_call(
        _router_kernel,
        out_shape=(jax.ShapeDtypeStruct((8, t), F32), jax.ShapeDtypeStruct((ne, 128), F32),
                   jax.ShapeDtypeStruct((t, d // 2), jnp.uint32)),
        grid=(t // tmr,),
        in_specs=[pl.BlockSpec((tmr, d), lambda i: (i, 0)),
                  pl.BlockSpec((1, d), lambda i: (0, 0)),
                  pl.BlockSpec((1, 1, d), bmap), pl.BlockSpec((1, 1, d), bmap),
                  pl.BlockSpec((128, d), lambda i: (0, 0)), pl.BlockSpec((128, d), lambda i: (0, 0)),
                  pl.BlockSpec((128, 1), lambda i: (0, 0))],
        out_specs=(pl.BlockSpec((8, tmr), lambda i: (0, i)),
                   pl.BlockSpec((ne, 128), lambda i: (0, 0)),
                   pl.BlockSpec((tmr, d // 2), lambda i: (i, 0))),
        scratch_shapes=[pltpu.VMEM((ne, 1), F32)],
        compiler_params=_cparams("arbitrary"), name="moe_router",
    )(x2d, g.reshape(1, d), sh[:, None, :], sc[:, None, :], rw_hi, rw_lo, rb)

    e1 = route[0].astype(jnp.int32)
    e2 = route[1].astype(jnp.int32)
    tile = EXPERT_TILE
    counts = cnt[:, 0].astype(jnp.int32)
    padded = ((counts + tile - 1) // tile) * tile
    ends = jnp.cumsum(padded)
    offs = ends - padded
    dest1 = offs[e1] + route[4].astype(jnp.int32)
    dest2 = offs[e2] + route[5].astype(jnp.int32)
    nt = (2 * t) // tile + ne
    n_used = ends[-1] // tile
    tid = jnp.arange(nt, dtype=jnp.int32)
    tsrc = jnp.minimum(tid, n_used - 1)
    texp = jnp.minimum(jnp.sum((tsrc[:, None] * tile >= ends[None, :]).astype(jnp.int32), axis=1), ne - 1)
    wts = jnp.concatenate([route[2:4].T, jnp.zeros((t, 6), F32)], axis=1)

    smem = pl.BlockSpec(memory_space=pltpu.SMEM)
    slot = pl.pallas_call(
        functools.partial(_invert_kernel, n_tok=t, unroll=8),
        out_shape=jax.ShapeDtypeStruct((nt * tile,), jnp.int32),
        in_specs=[smem] * 5, out_specs=smem, name="moe_invert",
    )(dest1, dest2, counts, offs, ends)

    ff = wg.shape[3]
    n_trash = ne * tile
    ys = pl.pallas_call(
        functools.partial(_expert_kernel, n_tok=t),
        out_shape=jax.ShapeDtypeStruct((2 * t + n_trash, d), F32),
        grid_spec=pltpu.PrefetchScalarGridSpec(
            num_scalar_prefetch=3, grid=(nt,),
            in_specs=[pl.BlockSpec(memory_space=pl.ANY),
                      pl.BlockSpec((1, 1, d, ff), lambda i, te, nv, sl: (layer, te[i], 0, 0)),
                      pl.BlockSpec((1, 1, d, ff), lambda i, te, nv, sl: (layer, te[i], 0, 0)),
                      pl.BlockSpec((1, 1, ff, d), lambda i, te, nv, sl: (layer, te[i], 0, 0))],
            out_specs=pl.BlockSpec(memory_space=pl.ANY),
            scratch_shapes=[pltpu.VMEM((tile, d // 2), jnp.uint32), pltpu.VMEM((tile, d // 2), jnp.uint32),
                            pltpu.VMEM((tile, d), F32), pltpu.VMEM((tile, d), F32),
                            pltpu.VMEM((d, ff), BF16), pltpu.VMEM((d, ff), BF16), pltpu.VMEM((ff, d), BF16),
                            pltpu.SemaphoreType.DMA((2,)), pltpu.SemaphoreType.DMA((2,))]),
        compiler_params=_cparams("arbitrary"), name="moe_experts",
    )(texp, n_used.reshape(1).astype(jnp.int32), slot, hp, wg, wu, wd)

    fin = final_w is not None
    fw = (final_w if fin else jnp.ones((d,), F32)).reshape(1, d)
    tmc = min(512, seq)
    nbc = t // tmc
    return pl.pallas_call(
        functools.partial(_combine_kernel, final_norm=fin),
        out_shape=jax.ShapeDtypeStruct((t, d), F32), grid=(nbc,),
        in_specs=[pl.BlockSpec((tmc, d), lambda i: (i, 0)),
                  pl.BlockSpec((tmc, 8), lambda i: (i, 0)),
                  pl.BlockSpec((1, 1, d), lambda i: ((i * tmc) // seq, 0, 0)),
                  pl.BlockSpec((1, d), lambda i: (0, 0)),
                  pl.BlockSpec((tmc, d), lambda i: (i, 0)),
                  pl.BlockSpec((tmc, d), lambda i: (i + nbc, 0))],
        out_specs=pl.BlockSpec((tmc, d), lambda i: (i, 0)),
        compiler_params=_cparams("arbitrary"), name="moe_combine",
    )(x2d, wts, gate[:, None, :], fw, ys, ys)


def _ssd_kernel(z_ref, x_ref, b_ref, c_ref, dt_ref, cwx_ref, cwb_ref, cwc_ref, cbx_ref, cbb_ref, cbc_ref,
                dtb_ref, alog_ref, dsk_ref, nw_ref, o_ref,
                xe_ref, be_ref, ce_ref, xs_ref, bs_ref, cs_ref, st_ref, acs_ref, dtg_ref):
    c = pl.program_id(1)
    lq = SSD_CHUNK
    ng = SSD_GROUPS
    gw = x_ref.shape[1] // ng
    hpg = gw // SSD_HEAD_DIM
    halo = xe_ref.shape[0] - lq

    @pl.when(c == 0)
    def _():
        xe_ref[0:halo] = jnp.zeros((halo, xe_ref.shape[1]), BF16)
        be_ref[0:halo] = jnp.zeros((halo, be_ref.shape[1]), BF16)
        ce_ref[0:halo] = jnp.zeros((halo, ce_ref.shape[1]), BF16)
        st_ref[...] = jnp.zeros_like(st_ref)

    xe_ref[halo:halo + lq] = x_ref[...]
    be_ref[halo:halo + lq] = b_ref[...]
    ce_ref[halo:halo + lq] = c_ref[...]

    sr = lax.broadcasted_iota(jnp.int32, (SSD_CONV * lq, halo + lq), 0)
    sm = lax.broadcasted_iota(jnp.int32, (SSD_CONV * lq, halo + lq), 1)
    tap = sr // lq
    shift_mat = jnp.where(sm == (sr - tap * lq) + halo - (SSD_CONV - 1) + tap, 1.0, 0.0).astype(BF16)

    def conv_silu(e_ref, w_ref, bias_ref, o_ref, width):
        for c0 in range(0, e_ref.shape[1], width):
            cols = slice(c0, c0 + width)
            sh = jnp.dot(shift_mat, e_ref[:, cols], preferred_element_type=F32)
            acc = bias_ref[:, cols]
            for k in range(SSD_CONV):
                acc = acc + sh[k * lq:(k + 1) * lq] * w_ref[k:k + 1, cols]
            o_ref[:, cols] = acc * jax.nn.sigmoid(acc)

    conv_silu(xe_ref, cwx_ref, cbx_ref, xs_ref, gw)
    conv_silu(be_ref, cwb_ref, cbb_ref, bs_ref, gw)
    conv_silu(ce_ref, cwc_ref, cbc_ref, cs_ref, gw)

    dt = jax.nn.softplus(dt_ref[...] + dtb_ref[...])
    adt = dt * (-jnp.exp(alog_ref[...]))
    li = lax.broadcasted_iota(jnp.int32, (lq, lq), 0)
    si = lax.broadcasted_iota(jnp.int32, (lq, lq), 1)
    tri = jnp.where(si <= li, 1.0, 0.0)
    acs = jnp.dot(tri, adt, precision=HIGHEST, preferred_element_type=F32)
    lane = lax.broadcasted_iota(jnp.int32, (lq, 128), 1)
    for gi in range(ng):
        sh = (128 - gi * hpg) % 128
        acs_ref[gi] = jnp.where(lane < hpg, pltpu.roll(acs, sh, axis=1) if sh else acs, 0.0)
        dtg_ref[gi] = jnp.where(lane < hpg, pltpu.roll(dt, sh, axis=1) if sh else dt, 0.0)
    causal = si <= li
    lane_lo = lane < SSD_HEAD_DIM

    def group_body(gi, carry):
        xc0 = pl.multiple_of(gi * gw, gw)
        nc0 = pl.multiple_of(gi * SSD_STATE, SSD_STATE)
        xg = xs_ref[:, pl.ds(xc0, gw)]
        bg = bs_ref[:, pl.ds(nc0, SSD_STATE)]
        cg = cs_ref[:, pl.ds(nc0, SSD_STATE)]
        bt = bg.T
        cgb = cg.astype(BF16)
        cb = jnp.dot(cgb, bt.astype(BF16), preferred_element_type=F32)
        acs_g = acs_ref[gi]
        acs_t = acs_g.T
        dt_t = dtg_ref[gi].T
        xgb = xg.astype(BF16)
        ys = []
        for jp in range(hpg // 2):
            lhs_parts, w_parts, ecols, dlast = [], [], [], []
            for j in (2 * jp, 2 * jp + 1):
                e_col = jnp.broadcast_to(acs_g[:, j:j + 1], (lq, lq))
                dt_row = dt_t[j:j + 1, :]
                r_row = jnp.broadcast_to(acs_t[j:j + 1, :] - jnp.log(dt_row), (lq, lq))
                lmat = jnp.where(causal, jnp.exp(e_col - r_row), 0.0) * cb
                lhs_parts.append(lmat.astype(BF16))
                a_last = acs_t[j:j + 1, lq - 1:lq]
                wrow = dt_row * jnp.exp(a_last - acs_t[j:j + 1, :])
                w_parts.append((bt * wrow).astype(BF16))
                ecols.append(e_col)
                dlast.append(jnp.exp(a_last))
            lhs = jnp.concatenate([jnp.concatenate(lhs_parts, axis=1),
                                   jnp.concatenate(w_parts, axis=1)], axis=0)
            xp = xgb[:, jp * 128:(jp + 1) * 128]
            zero = jnp.zeros_like(xp)
            rhs = jnp.concatenate([jnp.where(lane_lo, xp, zero), jnp.where(lane_lo, zero, xp)], axis=0)
            res = jnp.dot(lhs, rhs, preferred_element_type=F32)
            st = st_ref[gi, :, jp * 128:(jp + 1) * 128]
            yoff = jnp.dot(cgb, st.astype(BF16), preferred_element_type=F32)
            escale = jnp.exp(jnp.where(lane_lo, ecols[0], ecols[1]))
            ys.append(res[:lq] + yoff * escale)
            dl = jnp.where(lane_lo[:1], dlast[0], dlast[1])
            st_ref[gi, :, jp * 128:(jp + 1) * 128] = st * dl + res[lq:]
        y = jnp.concatenate(ys, axis=1) + dsk_ref[:, pl.ds(xc0, gw)] * xg
        zg = z_ref[:, pl.ds(xc0, gw)].astype(F32)
        y = y * (zg * jax.nn.sigmoid(zg))
        ms = jnp.mean(y * y, axis=-1, keepdims=True)
        y = y * lax.rsqrt(ms + EPS) * nw_ref[:, pl.ds(xc0, gw)]
        o_ref[:, pl.ds(xc0, gw)] = y.astype(o_ref.dtype)
        return carry

    lax.fori_loop(0, ng, group_body, 0)
    xe_ref[0:halo] = xe_ref[lq:lq + halo]
    be_ref[0:halo] = be_ref[lq:lq + halo]
    ce_ref[0:halo] = ce_ref[lq:lq + halo]


def _ssd_mixer(zx, dt_raw, bsz, seq, conv_w, conv_b, dt_bias, a_log, d_skip, norm_w):
    t = zx.shape[0]
    heads = a_log.shape[0]
    inner = heads * SSD_HEAD_DIM
    bcw = SSD_GROUPS * SSD_STATE
    lq = SSD_CHUNK
    nc = seq // lq
    pad = 128 - heads
    cw = conv_w.astype(F32)
    cwp = jnp.zeros((8, cw.shape[1]), F32).at[:SSD_CONV].set(cw)
    cb = conv_b.astype(F32).reshape(1, -1)
    row = lambda b, c: (b * nc + c, 0)
    const = lambda b, c: (0, 0)
    xb = inner // inner
    bb = (2 * inner) // bcw
    return pl.pallas_call(
        _ssd_kernel, out_shape=jax.ShapeDtypeStruct((t, inner), BF16), grid=(bsz, nc),
        in_specs=[pl.BlockSpec((lq, inner), lambda b, c: (b * nc + c, 0)),
                  pl.BlockSpec((lq, inner), lambda b, c: (b * nc + c, xb)),
                  pl.BlockSpec((lq, bcw), lambda b, c: (b * nc + c, bb)),
                  pl.BlockSpec((lq, bcw), lambda b, c: (b * nc + c, bb + 1)),
                  pl.BlockSpec((lq, 128), row),
                  pl.BlockSpec((8, inner), const), pl.BlockSpec((8, bcw), const), pl.BlockSpec((8, bcw), const),
                  pl.BlockSpec((1, inner), const), pl.BlockSpec((1, bcw), const), pl.BlockSpec((1, bcw), const),
                  pl.BlockSpec((1, 128), const), pl.BlockSpec((1, 128), const),
                  pl.BlockSpec((1, inner), const), pl.BlockSpec((1, inner), const)],
        out_specs=pl.BlockSpec((lq, inner), row),
        scratch_shapes=[pltpu.VMEM((lq + 16, inner), BF16), pltpu.VMEM((lq + 16, bcw), BF16),
                        pltpu.VMEM((lq + 16, bcw), BF16),
                        pltpu.VMEM((lq, inner), F32), pltpu.VMEM((lq, bcw), F32), pltpu.VMEM((lq, bcw), F32),
                        pltpu.VMEM((SSD_GROUPS, SSD_STATE, inner // SSD_GROUPS), F32),
                        pltpu.VMEM((SSD_GROUPS, lq, 128), F32), pltpu.VMEM((SSD_GROUPS, lq, 128), F32)],
        compiler_params=_cparams("arbitrary", "arbitrary"), name="ssd_mixer",
    )(zx, zx, zx, zx, dt_raw,
      cwp[:, :inner], cwp[:, inner:inner + bcw], cwp[:, inner + bcw:],
      cb[:, :inner], cb[:, inner:inner + bcw], cb[:, inner + bcw:],
      jnp.pad(dt_bias.astype(F32), (0, pad)).reshape(1, 128),
      jnp.pad(a_log.astype(F32), (0, pad)).reshape(1, 128),
      jnp.repeat(d_skip.astype(F32), SSD_HEAD_DIM).reshape(1, inner),
      norm_w.astype(F32).reshape(1, inner))


def kernel(x, c, positions, ada_w, ada_b, norm_mix, norm_ffn, mix_a_w_in, s5_lam_re, s5_lam_im, s5_log_dt, s5_b_re, s5_b_im, s5_c_re, s5_c_im, s5_d, s5_w_glu, attn_sinks, mix_a_w_out, ssd_w_in, ssd_conv_w, ssd_conv_b, ssd_dt_bias, ssd_a_log, ssd_d, ssd_norm_w, ssd_w_out, moe_r1_w, moe_r1_b, moe_r2_w, moe_r2_b, moe_w_gate, moe_w_up, moe_w_down, final_norm):
    bsz, seq, d = x.shape
    depth = ada_w.shape[0]
    t = bsz * seq
    xf = x.reshape(t, d).astype(F32)
    mod = _adaln(c.astype(F32), ada_w, ada_b)
    cos_t, sin_t = _rope_tables(positions)

    for i in range(depth):
        sh1, sc1, g1, sh2, sc2, g2 = jnp.split(mod[i], 6, axis=-1)
        j = i // 2
        if i % 2 == 0:
            s5w = s5_lam_re.shape[1] * S5_GROUP
            w_in = mix_a_w_in[j]
            u, qkv = _inproj0(xf, seq, norm_mix[i], sh1, sc1, w_in[:, :s5w], w_in[:, s5w:], tm=512)
            params = _s5_params(s5_lam_re[j], s5_lam_im[j], s5_log_dt[j], s5_b_re[j], s5_b_im[j],
                                s5_c_re[j], s5_c_im[j], s5_d[j], seq)
            y_s5 = _s5_mixer(u, bsz, seq, params)
            y_glu = _gelu_glu(y_s5, s5_w_glu[j], tm=1024, tn=512)
            wq = attn_sinks.shape[1] * HEAD_DIM
            wk = KV_HEADS * HEAD_DIM
            y_att = _attention(qkv, bsz, seq, cos_t, sin_t, attn_sinks[j],
                               q_col=0, k_col=wq // wk, v_col=wq // wk + 1)
            xf = _matmul_residual([y_glu, y_att], mix_a_w_out[j], xf, seq, g1, tm=1024, tn=512)
        else:
            heads = ssd_a_log.shape[1]
            inner = heads * SSD_HEAD_DIM
            main = inner + inner + 2 * SSD_GROUPS * SSD_STATE
            w_in = ssd_w_in[j]
            w_dt = jnp.pad(w_in[:, main:], ((0, 0), (0, 128 - heads))).astype(BF16)
            zx, dt_raw = _inproj1(xf, seq, norm_mix[i], sh1, sc1, w_in, main, w_dt, tm=1024, tn=512)
            y = _ssd_mixer(zx, dt_raw, bsz, seq, ssd_conv_w[j], ssd_conv_b[j], ssd_dt_bias[j],
                           ssd_a_log[j], ssd_d[j], ssd_norm_w[j])
            xf = _matmul_residual([y], ssd_w_out[j], xf, seq, g1, tm=1024, tn=512)
        xf = _hier_moe(xf, seq, norm_ffn[i], sh2, sc2, g2, moe_r1_w[i], moe_r1_b[i], moe_r2_w[i], moe_r2_b[i],
                       moe_w_gate, moe_w_up, moe_w_down, i,
                       final_norm if i == depth - 1 else None)
    return xf.reshape(bsz, seq, d).astype(x.dtype)
```

```python
import functools
import math

import jax
import jax.numpy as jnp
from jax import lax
from jax.experimental import pallas as pl
from jax.experimental.pallas import tpu as pltpu

F32 = jnp.float32
BF16 = jnp.bfloat16
HIGHEST = lax.Precision.HIGHEST

EPS = 1e-6
S5_GROUP = 16
S5_STATE = 64
S5_CHUNK = 16
HEAD_DIM = 64
KV_HEADS = 2
WINDOW = 128
ROPE_DIM = 16
ROPE_THETA = 500000.0
SSD_HEAD_DIM = 64
SSD_GROUPS = 8
SSD_STATE = 128
SSD_CONV = 4
SSD_CHUNK = 128
MOE_GROUPS = 4
MOE_EPG = 8
MOE_EXPERTS = MOE_GROUPS * MOE_EPG
EXPERT_TILE = 256
NEG_BIG = -1e30
VMEM_LIMIT = 56 * 1024 * 1024


def _cparams(*sem):
    return pltpu.CompilerParams(dimension_semantics=sem, vmem_limit_bytes=VMEM_LIMIT)


def _modnorm(x, g, sh, sc):
    ms = jnp.mean(x * x, axis=-1, keepdims=True)
    return (x * lax.rsqrt(ms + EPS) * g) * (1.0 + sc) + sh


def _adaln_kernel(c_ref, w_ref, b_ref, o_ref):
    c = c_ref[...]
    a = c * jax.nn.sigmoid(c)
    o_ref[0] = jnp.dot(a, w_ref[0], precision=HIGHEST, preferred_element_type=F32) + b_ref[0]


def _adaln(c, ada_w, ada_b):
    depth, d, n = ada_w.shape
    bsz = c.shape[0]
    rows = 8
    cp = jnp.zeros((rows, d), F32).at[:bsz].set(c)
    tn = 1024
    out = pl.pallas_call(
        _adaln_kernel,
        out_shape=jax.ShapeDtypeStruct((depth, rows, n), F32),
        grid=(depth, n // tn),
        in_specs=[pl.BlockSpec((rows, d), lambda l, j: (0, 0)),
                  pl.BlockSpec((1, d, tn), lambda l, j: (l, 0, j)),
                  pl.BlockSpec((1, 1, tn), lambda l, j: (l, 0, j))],
        out_specs=pl.BlockSpec((1, rows, tn), lambda l, j: (l, 0, j)),
        compiler_params=_cparams("arbitrary", "arbitrary"),
        name="adaln",
    )(cp, ada_w, ada_b.reshape(depth, 1, n))
    return out[:, :bsz]


def _inproj1_kernel(x_ref, g_ref, sh_ref, sc_ref, w_ref, w2_ref, o_ref, o2_ref, h_ref):
    @pl.when(pl.program_id(1) == 0)
    def _():
        h = _modnorm(x_ref[...], g_ref[...], sh_ref[0], sc_ref[0]).astype(BF16)
        h_ref[...] = h
        o2_ref[...] = jnp.dot(h, w2_ref[...], preferred_element_type=F32)

    o_ref[...] = jnp.dot(h_ref[...], w_ref[...].astype(BF16), preferred_element_type=F32).astype(o_ref.dtype)


def _inproj1(x2d, seq, g, sh, sc, w, n_main, w2, tm, tn):
    t, d = x2d.shape
    tm = min(tm, seq)
    n2 = w2.shape[1]
    bmap = lambda i, j: ((i * tm) // seq, 0, 0)
    return pl.pallas_call(
        _inproj1_kernel,
        out_shape=(jax.ShapeDtypeStruct((t, n_main), BF16), jax.ShapeDtypeStruct((t, n2), F32)),
        grid=(t // tm, n_main // tn),
        in_specs=[pl.BlockSpec((tm, d), lambda i, j: (i, 0)),
                  pl.BlockSpec((1, d), lambda i, j: (0, 0)),
                  pl.BlockSpec((1, 1, d), bmap), pl.BlockSpec((1, 1, d), bmap),
                  pl.BlockSpec((d, tn), lambda i, j: (0, j)),
                  pl.BlockSpec((d, n2), lambda i, j: (0, 0))],
        out_specs=(pl.BlockSpec((tm, tn), lambda i, j: (i, j)), pl.BlockSpec((tm, n2), lambda i, j: (i, 0))),
        scratch_shapes=[pltpu.VMEM((tm, d), BF16)],
        compiler_params=_cparams("arbitrary", "arbitrary"), name="inproj1",
    )(x2d, g.reshape(1, d), sh[:, None, :], sc[:, None, :], w, w2)


def _mmres_kernel(*refs, n_in):
    a_refs, w_refs = refs[:n_in], refs[n_in:2 * n_in]
    x_ref, gt_ref, o_ref = refs[2 * n_in:]
    acc = jnp.dot(a_refs[0][...], w_refs[0][...].astype(BF16), preferred_element_type=F32)
    for a_ref, w_ref in zip(a_refs[1:], w_refs[1:]):
        acc = acc + jnp.dot(a_ref[...], w_ref[...].astype(BF16), preferred_element_type=F32)
    o_ref[...] = x_ref[...] + gt_ref[0] * acc


def _matmul_residual(a_list, w, x2d, seq, gate, tm, tn):
    t, n = x2d.shape
    tm = min(tm, seq)
    in_specs, row = [], 0
    for a in a_list:
        in_specs.append(pl.BlockSpec((tm, a.shape[1]), lambda i, j: (i, 0)))
    for a in a_list:
        k = a.shape[1]
        in_specs.append(pl.BlockSpec((k, tn), functools.partial(lambda i, j, rb: (rb, j), rb=row // k)))
        row += k
    in_specs.append(pl.BlockSpec((tm, tn), lambda i, j: (i, j)))
    in_specs.append(pl.BlockSpec((1, 1, tn), lambda i, j: ((i * tm) // seq, 0, j)))
    return pl.pallas_call(
        functools.partial(_mmres_kernel, n_in=len(a_list)),
        out_shape=jax.ShapeDtypeStruct((t, n), F32), grid=(t // tm, n // tn),
        in_specs=in_specs, out_specs=pl.BlockSpec((tm, tn), lambda i, j: (i, j)),
        compiler_params=_cparams("arbitrary", "arbitrary"), name="matmul_residual",
    )(*a_list, *([w] * len(a_list)), x2d, gate[:, None, :])


def _inproj0_kernel(x_ref, g_ref, sh_ref, sc_ref, wu_ref, wa_ref, u_ref, a_ref, wub_ref, wab_ref):
    @pl.when(pl.program_id(0) == 0)
    def _():
        wub_ref[...] = wu_ref[...].astype(BF16)
        wab_ref[...] = wa_ref[...].astype(BF16)

    h = _modnorm(x_ref[...], g_ref[...], sh_ref[0], sc_ref[0]).astype(BF16)
    u_ref[...] = jnp.dot(h, wub_ref[...], preferred_element_type=F32)
    a_ref[...] = jnp.dot(h, wab_ref[...], preferred_element_type=F32).astype(a_ref.dtype)


def _inproj0(x2d, seq, g, sh, sc, w_u, w_a, tm):
    t, d = x2d.shape
    tm = min(tm, seq)
    nu, na = w_u.shape[1], w_a.shape[1]
    bmap = lambda i: ((i * tm) // seq, 0, 0)
    return pl.pallas_call(
        _inproj0_kernel,
        out_shape=(jax.ShapeDtypeStruct((t, nu), F32), jax.ShapeDtypeStruct((t, na), BF16)),
        grid=(t // tm,),
        in_specs=[pl.BlockSpec((tm, d), lambda i: (i, 0)),
                  pl.BlockSpec((1, d), lambda i: (0, 0)),
                  pl.BlockSpec((1, 1, d), bmap), pl.BlockSpec((1, 1, d), bmap),
                  pl.BlockSpec((d, nu), lambda i: (0, 0), pipeline_mode=pl.Buffered(1)),
                  pl.BlockSpec((d, na), lambda i: (0, 0), pipeline_mode=pl.Buffered(1))],
        out_specs=(pl.BlockSpec((tm, nu), lambda i: (i, 0)), pl.BlockSpec((tm, na), lambda i: (i, 0))),
        scratch_shapes=[pltpu.VMEM((d, nu), BF16), pltpu.VMEM((d, na), BF16)],
        compiler_params=_cparams("arbitrary"), name="inproj0",
    )(x2d, g.reshape(1, d), sh[:, None, :], sc[:, None, :], w_u, w_a)


S5_SLAB = 8
S5_SEGS = 8


def _s5_params(lam_re, lam_im, log_dt, b_re, b_im, c_re, c_im, d_skip, seq):
    g, n = lam_re.shape
    p = S5_GROUP
    lc = S5_CHUNK
    gs = S5_SLAB
    nj = g // gs
    lam = lax.complex(lam_re.astype(F32), lam_im.astype(F32))
    dt = jnp.exp(log_dt.astype(F32))[:, None]
    ld = lam * dt
    lam_bar = jnp.exp(ld)
    bmat = lax.complex(b_re.astype(F32), b_im.astype(F32))
    b_bar = ((lam_bar - 1.0) / lam)[..., None] * bmat
    cmat = lax.complex(c_re.astype(F32), c_im.astype(F32))
    ks = jnp.arange(lc, dtype=F32)
    pw = jnp.exp(ld[:, None, :] * ks[None, :, None])
    pw1 = jnp.exp(ld[:, None, :] * (ks[None, :, None] + 1.0))
    kk = jnp.real(jnp.einsum('gpn,gkn,gnq->gkpq', cmat, pw, b_bar, precision=HIGHEST))
    eg = jnp.eye(gs, dtype=F32)
    ex_p = jnp.einsum('ab,pr->apbr', eg, jnp.eye(p, dtype=F32)).reshape(gs, p, gs * p)
    ex_n = jnp.einsum('ab,nr->anbr', eg, jnp.eye(n, dtype=F32)).reshape(gs, n, gs * n)
    kq_ = kk.transpose(0, 1, 3, 2).reshape(nj, gs, lc, p, p)
    kexp = jnp.einsum('jakqp,apc->jkaqc', kq_, ex_p, precision=HIGHEST).reshape(nj, lc, gs * p, gs * p)
    kexp = kexp.at[:, 0].add(jnp.eye(gs * p, dtype=F32)[None] * d_skip.astype(F32).reshape(nj, 1, gs * p))
    kexp = kexp.astype(BF16)
    wb = (pw[:, ::-1, :, None] * b_bar[:, None, :, :]).transpose(0, 1, 3, 2)
    wb = wb.reshape(nj, gs, lc, p, n)
    wb_slab = jnp.concatenate(
        [jnp.einsum('jasqn,anc->jsaqc', part.astype(BF16), ex_n.astype(BF16)).reshape(nj, lc * gs * p, gs * n)
         for part in (jnp.real(wb), jnp.imag(wb))], axis=-1).astype(BF16)
    cl = (cmat[:, None, :, :] * pw1[:, :, None, :]).transpose(0, 3, 1, 2)
    cl = cl.reshape(nj, gs, n, lc, p)
    woff_slab = jnp.concatenate(
        [jnp.einsum('janlp,apc->janlc', part.astype(BF16), ex_p.astype(BF16)).reshape(nj, gs * n, lc * gs * p)
         for part in (jnp.real(cl), -jnp.imag(cl))], axis=1).astype(BF16)
    seglen = (seq // lc) // S5_SEGS
    kq = jnp.arange(seglen + 1, dtype=F32) * float(lc)
    pq = jnp.exp(ld[:, None, :] * kq[None, :, None]).reshape(nj, gs, seglen + 1, n)
    pq = pq.transpose(0, 2, 1, 3).reshape(nj, seglen + 1, gs * n)
    return kexp, wb_slab, woff_slab, jnp.real(pq), jnp.imag(pq)


def _s5_kernel(u_ref, k_ref, wb_ref, woff_ref, pre_ref, pim_ref, y_ref, sloc_ref, sin_ref, m_ref):
    lc = S5_CHUNK
    w = u_ref.shape[1]

    @pl.when((pl.program_id(0) == 0) & (pl.program_id(1) == 0))
    def _():
        m_ref[...] = jnp.zeros_like(m_ref)

    @pl.when(pl.program_id(1) == 0)
    def _():
        for s in range(lc):
            for l in range(s, lc):
                m_ref[s * w:(s + 1) * w, l * w:(l + 1) * w] = k_ref[0, l - s]

    nc = u_ref.shape[0] // lc
    hw = wb_ref.shape[2] // 2
    seglen = nc // S5_SEGS
    u_all = jnp.concatenate([u_ref[pl.ds(s, nc, stride=lc), :] for s in range(lc)], axis=1).astype(BF16)
    nq = hw // w
    sloc = jnp.dot(u_all, wb_ref[0], preferred_element_type=F32)
    for q in range(2 * nq):
        sloc_ref[q] = sloc[:, q * w:(q + 1) * w]
    a_re, a_im = pre_ref[0, 1:2], pim_ref[0, 1:2]

    def put(rows, re, im):
        for q in range(nq):
            sin_ref[q, rows, :] = re[:, q * w:(q + 1) * w]
            sin_ref[nq + q, rows, :] = im[:, q * w:(q + 1) * w]

    def get(ref, rows):
        return (jnp.concatenate([ref[q, rows, :] for q in range(nq)], axis=1),
                jnp.concatenate([ref[nq + q, rows, :] for q in range(nq)], axis=1))

    def seg_step(k, st):
        sre, sim = st
        rows = pl.ds(k, S5_SEGS, stride=seglen)
        put(rows, sre, sim)
        lre, lim = get(sloc_ref, rows)
        return a_re * sre - a_im * sim + lre, a_re * sim + a_im * sre + lim

    z = jnp.zeros((S5_SEGS, hw), F32)
    ere, eim = lax.fori_loop(0, seglen, seg_step, (z, z))
    g_re, g_im = pre_ref[0, seglen:seglen + 1], pim_ref[0, seglen:seglen + 1]
    cre, cim = jnp.zeros((1, hw), F32), jnp.zeros((1, hw), F32)
    pre, pim = pre_ref[0, 0:seglen], pim_ref[0, 0:seglen]
    for sg in range(1, S5_SEGS):
        cre, cim = (g_re * cre - g_im * cim + ere[sg - 1:sg], g_re * cim + g_im * cre + eim[sg - 1:sg])
        rows = slice(sg * seglen, (sg + 1) * seglen)
        cur_re, cur_im = get(sin_ref, rows)
        put(rows, cur_re + pre * cre - pim * cim, cur_im + pre * cim + pim * cre)
    s_in = jnp.concatenate([sin_ref[q] for q in range(2 * nq)], axis=1).astype(BF16)
    for lb in range(lc // 2):
        c0, c1 = 2 * lb * w, (2 * lb + 2) * w
        y2 = (jnp.dot(u_all[:, :c1], m_ref[:c1, c0:c1], preferred_element_type=F32)
              + jnp.dot(s_in, woff_ref[0, :, c0:c1], preferred_element_type=F32))
        y_ref[pl.ds(2 * lb, nc, stride=lc), :] = y2[:, :w]
        y_ref[pl.ds(2 * lb + 1, nc, stride=lc), :] = y2[:, w:]


def _s5_mixer(u, bsz, seq, params):
    kexp, wb_slab, woff_slab, pq_re, pq_im = params
    nj, km, sw = wb_slab.shape
    w = km // S5_CHUNK
    nc = seq // S5_CHUNK
    t = u.shape[0]
    const = lambda j, b: (j, 0, 0)
    return pl.pallas_call(
        _s5_kernel, out_shape=jax.ShapeDtypeStruct((t, nj * w), F32), grid=(nj, bsz),
        in_specs=[pl.BlockSpec((seq, w), lambda j, b: (b, j)),
                  pl.BlockSpec((1, S5_CHUNK, w, w), lambda j, b: (j, 0, 0, 0)),
                  pl.BlockSpec((1, km, sw), const),
                  pl.BlockSpec((1, sw, km), const),
                  pl.BlockSpec((1,) + pq_re.shape[1:], const),
                  pl.BlockSpec((1,) + pq_im.shape[1:], const)],
        out_specs=pl.BlockSpec((seq, w), lambda j, b: (b, j)),
        scratch_shapes=[pltpu.VMEM((sw // w, nc, w), F32), pltpu.VMEM((sw // w, nc, w), F32),
                        pltpu.VMEM((km, km), BF16)],
        compiler_params=_cparams("arbitrary", "arbitrary"), name="s5_mixer",
    )(u, kexp, wb_slab, woff_slab, pq_re, pq_im)


def _glu_kernel(y_ref, wv_ref, wg_ref, o_ref, a_ref):
    @pl.when(pl.program_id(1) == 0)
    def _():
        a_ref[...] = jax.nn.gelu(y_ref[...].astype(F32), approximate=True).astype(BF16)

    a = a_ref[...]
    val = jnp.dot(a, wv_ref[...].astype(BF16), preferred_element_type=F32)
    gate = jnp.dot(a, wg_ref[...].astype(BF16), preferred_element_type=F32)
    o_ref[...] = (val * jax.nn.sigmoid(gate)).astype(o_ref.dtype)


def _gelu_glu(y, w, tm, tn):
    t, k = y.shape
    n = w.shape[1] // 2
    tm = min(tm, t)
    nb = n // tn
    return pl.pallas_call(
        _glu_kernel, out_shape=jax.ShapeDtypeStruct((t, n), BF16), grid=(t // tm, n // tn),
        in_specs=[pl.BlockSpec((tm, k), lambda i, j: (i, 0)),
                  pl.BlockSpec((k, tn), lambda i, j: (0, j)),
                  pl.BlockSpec((k, tn), lambda i, j: (0, j + nb))],
        out_specs=pl.BlockSpec((tm, tn), lambda i, j: (i, j)),
        scratch_shapes=[pltpu.VMEM((tm, k), BF16)],
        compiler_params=_cparams("arbitrary", "arbitrary"), name="gelu_glu",
    )(y, w, w)


def _rope(x, cos, sin):
    w = x.shape[1]
    half = ROPE_DIM // 2
    d = lax.broadcasted_iota(jnp.int32, x.shape, 1) % HEAD_DIM
    partner = jnp.where(d < half, pltpu.roll(x, w - half, axis=1), pltpu.roll(x, half, axis=1))
    return x * cos + partner * sin


def _attn_kernel(sink_ref, q_ref, kc_ref, kp_ref, vc_ref, vp_ref,
                 cq_ref, sq_ref, cp_ref, sp_ref, o_ref):
    n = pl.program_id(1)
    wq = q_ref.shape[1]
    hq = wq // HEAD_DIM
    grp = hq // KV_HEADS
    cos_c, sin_c = cq_ref[...], sq_ref[...]
    reps = wq // cos_c.shape[1]
    q = _rope(q_ref[...].astype(F32),
              jnp.concatenate([cos_c] * reps, axis=1), jnp.concatenate([sin_c] * reps, axis=1))
    q = (q * (HEAD_DIM ** -0.5)).astype(BF16)
    kc = _rope(kc_ref[...].astype(F32), cos_c, sin_c).astype(BF16)
    kp = _rope(kp_ref[...].astype(F32), cp_ref[...], sp_ref[...]).astype(BF16)
    kk = jnp.concatenate([kp, kc], axis=0)
    vv = jnp.concatenate([vp_ref[...], vc_ref[...]], axis=0)
    qi = lax.broadcasted_iota(jnp.int32, (WINDOW, 2 * WINDOW), 0)
    kj = lax.broadcasted_iota(jnp.int32, (WINDOW, 2 * WINDOW), 1)
    first = jnp.where(n > 0, 0, WINDOW)
    valid = (kj > qi) & (kj <= qi + WINDOW) & (kj >= first)
    outs = []
    for h in range(hq):
        kvh = h // grp
        qh = q[:, h * HEAD_DIM:(h + 1) * HEAD_DIM]
        kh = kk[:, kvh * HEAD_DIM:(kvh + 1) * HEAD_DIM]
        vh = vv[:, kvh * HEAD_DIM:(kvh + 1) * HEAD_DIM]
        s = lax.dot_general(qh, kh, (((1,), (1,)), ((), ())), preferred_element_type=F32)
        s = jnp.where(valid, s, NEG_BIG)
        sink = sink_ref[h]
        mx = jnp.maximum(jnp.max(s, axis=-1, keepdims=True), sink)
        p = jnp.exp(s - mx)
        den = jnp.sum(p, axis=-1, keepdims=True) + jnp.exp(sink - mx)
        o = jnp.dot(p.astype(BF16), vh, preferred_element_type=F32)
        outs.append(o / den)
    o_ref[...] = jnp.concatenate(outs, axis=1).astype(o_ref.dtype)


def _attention(proj, bsz, seq, cos_t, sin_t, sinks, q_col, k_col, v_col):
    t = proj.shape[0]
    nb = seq // WINDOW
    wq = sinks.shape[0] * HEAD_DIM
    wk = KV_HEADS * HEAD_DIM
    cur = lambda b, n, s: (b * nb + n, 0)
    prv = lambda b, n, s: (b * nb + jnp.maximum(n - 1, 0), 0)
    return pl.pallas_call(
        _attn_kernel, out_shape=jax.ShapeDtypeStruct((t, wq), BF16),
        grid_spec=pltpu.PrefetchScalarGridSpec(
            num_scalar_prefetch=1, grid=(bsz, nb),
            in_specs=[pl.BlockSpec((WINDOW, wq), lambda b, n, s: (b * nb + n, q_col)),
                      pl.BlockSpec((WINDOW, wk), lambda b, n, s: (b * nb + n, k_col)),
                      pl.BlockSpec((WINDOW, wk), lambda b, n, s: (b * nb + jnp.maximum(n - 1, 0), k_col)),
                      pl.BlockSpec((WINDOW, wk), lambda b, n, s: (b * nb + n, v_col)),
                      pl.BlockSpec((WINDOW, wk), lambda b, n, s: (b * nb + jnp.maximum(n - 1, 0), v_col)),
                      pl.BlockSpec((WINDOW, wk), cur), pl.BlockSpec((WINDOW, wk), cur),
                      pl.BlockSpec((WINDOW, wk), prv), pl.BlockSpec((WINDOW, wk), prv)],
            out_specs=pl.BlockSpec((WINDOW, wq), lambda b, n, s: (b * nb + n, 0))),
        compiler_params=_cparams("arbitrary", "arbitrary"), name="swa_attention",
    )(sinks.astype(F32), proj, proj, proj, proj, proj, cos_t, sin_t, cos_t, sin_t)


def _rope_tables(positions):
    half = ROPE_DIM // 2
    inv = 1.0 / (ROPE_THETA ** (jnp.arange(0, ROPE_DIM, 2, dtype=F32) / ROPE_DIM))
    ang = positions.astype(F32).reshape(-1, 1) * inv[None, :]
    cos, sin = jnp.cos(ang), jnp.sin(ang)
    t = ang.shape[0]
    ones = jnp.ones((t, HEAD_DIM - ROPE_DIM), F32)
    cos_h = jnp.concatenate([cos, cos, ones], axis=1)
    sin_h = jnp.concatenate([-sin, sin, 0.0 * ones], axis=1)
    return jnp.tile(cos_h, (1, KV_HEADS)), jnp.tile(sin_h, (1, KV_HEADS))


def _pack_bf16_pairs(v):
    half = v.shape[1] // 2
    hi = lax.bitcast_convert_type(v[:, :half].astype(BF16).astype(F32), jnp.uint32)
    lo = lax.bitcast_convert_type(v[:, half:].astype(BF16).astype(F32), jnp.uint32)
    return hi | (lo >> 16)


def _unpack_bf16_pairs(p):
    return jnp.concatenate([lax.bitcast_convert_type(p & jnp.uint32(0xFFFF0000), F32),
                            lax.bitcast_convert_type(p << 16, F32)], axis=1)


def _router_kernel(x_ref, g_ref, sh_ref, sc_ref, rwh_ref, rwl_ref, rb_ref, o_ref, cnt_ref, hp_ref, carry_ref):
    i = pl.program_id(0)
    tm = x_ref.shape[0]
    ne = MOE_EXPERTS

    @pl.when(i == 0)
    def _():
        carry_ref[...] = jnp.zeros_like(carry_ref)

    h = _modnorm(x_ref[...], g_ref[...], sh_ref[0], sc_ref[0])
    hp_ref[...] = _pack_bf16_pairs(h)
    nt_dims = (((1,), (1,)), ((), ()))
    h_hi = h.astype(BF16)
    h_lo = (h - h_hi.astype(F32)).astype(BF16)
    lt = (lax.dot_general(rwh_ref[...], h_hi, nt_dims, preferred_element_type=F32)
          + lax.dot_general(rwh_ref[...], h_lo, nt_dims, preferred_element_type=F32)
          + lax.dot_general(rwl_ref[...], h_hi, nt_dims, preferred_element_type=F32)) + rb_ref[...]
    gl = lt[0:MOE_GROUPS]
    gm = jnp.max(gl, axis=0, keepdims=True)
    gsum = jnp.sum(jnp.exp(gl - gm), axis=0, keepdims=True)
    g_val = 1.0 / gsum
    grow = lax.broadcasted_iota(jnp.int32, gl.shape, 0)
    g_idx = jnp.min(jnp.where(gl == gm, grow, MOE_GROUPS), axis=0, keepdims=True)
    sel = jnp.zeros((MOE_EPG, tm), F32)
    for gi in range(MOE_GROUPS):
        sel = jnp.where(g_idx == gi, lt[8 + gi * MOE_EPG: 8 + (gi + 1) * MOE_EPG], sel)
    erow = lax.broadcasted_iota(jnp.int32, sel.shape, 0)
    m1 = jnp.max(sel, axis=0, keepdims=True)
    i1 = jnp.min(jnp.where(sel == m1, erow, MOE_EPG), axis=0, keepdims=True)
    sel2 = jnp.where(erow == i1, -jnp.inf, sel)
    m2 = jnp.max(sel2, axis=0, keepdims=True)
    i2 = jnp.min(jnp.where(sel2 == m2, erow, MOE_EPG), axis=0, keepdims=True)
    e21 = jnp.exp(m2 - m1)
    w1 = g_val / (1.0 + e21)
    w2 = g_val * e21 / (1.0 + e21)
    e1 = g_idx * MOE_EPG + i1
    e2 = g_idx * MOE_EPG + i2
    xrow = lax.broadcasted_iota(jnp.int32, (ne, tm), 0)
    oh1 = xrow == e1
    oh2 = xrow == e2
    cmat = jnp.where(oh1 | oh2, 1.0, 0.0)
    ti = lax.broadcasted_iota(jnp.int32, (tm, tm), 0)
    tj = lax.broadcasted_iota(jnp.int32, (tm, tm), 1)
    su = jnp.where(ti < tj, 1.0, 0.0).astype(BF16)
    before = jnp.dot(cmat.astype(BF16), su, preferred_element_type=F32) + carry_ref[...]
    r1 = jnp.sum(jnp.where(oh1, before, 0.0), axis=0, keepdims=True)
    r2 = jnp.sum(jnp.where(oh2, before, 0.0), axis=0, keepdims=True)
    carry = carry_ref[...] + jnp.sum(cmat, axis=1, keepdims=True)
    carry_ref[...] = carry
    cnt_ref[...] = jnp.broadcast_to(carry, cnt_ref.shape)
    o_ref[...] = jnp.concatenate([e1.astype(F32), e2.astype(F32), w1, w2, r1, r2,
                                  jnp.zeros((2, tm), F32)], axis=0)


def _invert_kernel(d1_ref, d2_ref, cnt_ref, off_ref, end_ref, hp_ref, slot_ref, hpt_ref, sem, *, n_tok, unroll):
    ne = cnt_ref.shape[0]
    lanes = hpt_ref.shape[2]

    def relayout(j):
        return pltpu.make_async_copy(hp_ref.at[:, pl.ds(j * lanes, lanes)], hpt_ref.at[:, j, :], sem)

    for j in range(hpt_ref.shape[1]):
        relayout(j).start()

    def pad_expert(e, p):
        lo = off_ref[e] + cnt_ref[e]

        def pad_row(r, p):
            slot_ref[r] = 2 * n_tok + p
            return p + 1

        return lax.fori_loop(lo, end_ref[e], pad_row, p)

    lax.fori_loop(0, ne, pad_expert, 0)

    def tok_block(i, c):
        for j in range(unroll):
            t = i * unroll + j
            slot_ref[d1_ref[t]] = t
            slot_ref[d2_ref[t]] = n_tok + t
        return c

    lax.fori_loop(0, n_tok // unroll, tok_block, 0)
    for j in range(hpt_ref.shape[1]):
        relayout(j).wait()


def _expert_kernel(te_ref, nv_ref, slot_ref, hp_ref, wg_ref, wu_ref, wd_ref, ys_ref,
                   xb0_ref, xb1_ref, yb0_ref, yb1_ref, wgb_ref, wub_ref, wdb_ref, gsem, ssem, *, n_tok):
    i = pl.program_id(0)
    nv = nv_ref[0]
    pk = hp_ref.shape[1]
    tile = xb0_ref.shape[0] // pk

    def gather(tile_idx, xb_ref, sem):
        for r in range(tile):
            s = slot_ref[tile_idx * tile + r]
            tok = jnp.where(s >= 2 * n_tok, 0, jnp.where(s >= n_tok, s - n_tok, s))
            pltpu.make_async_copy(hp_ref.at[tok], xb_ref.at[pl.ds(r * pk, pk)], sem).start()

    def wait_gather(xb_ref, sem):
        pltpu.make_async_copy(xb_ref, xb_ref, sem).wait()

    def scatter(tile_idx, yb_ref, sem):
        for r in range(tile):
            s = slot_ref[tile_idx * tile + r]
            pltpu.make_async_copy(yb_ref.at[pl.ds(r * pk, pk)], ys_ref.at[s], sem).start()

    def wait_scatter(yb_ref, sem):
        pltpu.make_async_copy(yb_ref, yb_ref, sem).wait()

    new_expert = (i == 0) | (te_ref[i] != te_ref[jnp.maximum(i - 1, 0)])

    @pl.when(new_expert & (i < nv))
    def _():
        wgb_ref[...] = wg_ref[0, 0].astype(BF16)
        wub_ref[...] = wu_ref[0, 0].astype(BF16)
        wdb_ref[...] = wd_ref[0, 0].astype(BF16)

    @pl.when(i == 0)
    def _():
        gather(0, xb0_ref, gsem.at[0])

    def step(p, xb_ref, xb_next_ref, yb_ref, yb_other_ref):
        wait_gather(xb_ref, gsem.at[p])

        @pl.when(i >= 2)
        def _():
            wait_scatter(yb_ref, ssem.at[p])

        xp = jnp.concatenate([xb_ref[pl.ds(j, tile, stride=pk), :] for j in range(pk)], axis=1)
        x = _unpack_bf16_pairs(xp).astype(BF16)
        gather(jnp.minimum(i + 1, nv - 1), xb_next_ref, gsem.at[1 - p])
        hg = jnp.dot(x, wgb_ref[...], preferred_element_type=F32)
        hu = jnp.dot(x, wub_ref[...], preferred_element_type=F32)
        hid = (hg * jax.nn.sigmoid(hg) * hu).astype(BF16)
        y = jnp.dot(hid, wdb_ref[...], preferred_element_type=F32)
        yp = _pack_bf16_pairs(y)
        lanes = yb_ref.shape[1]
        for j in range(pk):
            yb_ref[pl.ds(j, tile, stride=pk), :] = yp[:, j * lanes:(j + 1) * lanes]
        scatter(i, yb_ref, ssem.at[p])

        @pl.when(i == nv - 1)
        def _():
            wait_gather(xb_next_ref, gsem.at[1 - p])
            wait_scatter(yb_ref, ssem.at[p])

            @pl.when(i >= 1)
            def _():
                wait_scatter(yb_other_ref, ssem.at[1 - p])

    odd = lax.rem(i, 2) == 1

    @pl.when((i < nv) & jnp.logical_not(odd))
    def _():
        step(0, xb0_ref, xb1_ref, yb0_ref, yb1_ref)

    @pl.when((i < nv) & odd)
    def _():
        step(1, xb1_ref, xb0_ref, yb1_ref, yb0_ref)


def _combine_kernel(x_ref, w_ref, gt_ref, fn_ref, ya_ref, yb_ref, o_ref, *, final_norm):
    w = w_ref[...]
    tm = x_ref.shape[0]
    pk = ya_ref.shape[0] // tm

    def rows(ref):
        return _unpack_bf16_pairs(jnp.concatenate([ref[pl.ds(j, tm, stride=pk), :] for j in range(pk)], axis=1))

    y = w[:, 0:1] * rows(ya_ref) + w[:, 1:2] * rows(yb_ref)
    out = x_ref[...] + gt_ref[0] * y
    if final_norm:
        ms = jnp.mean(out * out, axis=-1, keepdims=True)
        out = out * lax.rsqrt(ms + EPS) * fn_ref[...]
    o_ref[...] = out


def _hier_moe(x2d, seq, g, sh, sc, gate, r1_w, r1_b, r2_w, r2_b, wg, wu, wd, layer, final_w):
    t, d = x2d.shape
    ne = MOE_EXPERTS
    rw = jnp.zeros((128, d), F32).at[0:MOE_GROUPS].set(r1_w.T).at[8:8 + ne].set(r2_w.T)
    rb = jnp.zeros((128, 1), F32).at[0:MOE_GROUPS, 0].set(r1_b).at[8:8 + ne, 0].set(r2_b)
    rw_hi = rw.astype(BF16)
    rw_lo = (rw - rw_hi.astype(F32)).astype(BF16)
    tmr = min(512, seq)
    bmap = lambda i: ((i * tmr) // seq, 0, 0)
    route, cnt, hp = pl.pallas_call(
        _router_kernel,
        out_shape=(jax.ShapeDtypeStruct((8, t), F32), jax.ShapeDtypeStruct((ne, 128), F32),
                   jax.ShapeDtypeStruct((t, d // 2), jnp.uint32)),
        grid=(t // tmr,),
        in_specs=[pl.BlockSpec((tmr, d), lambda i: (i, 0)),
                  pl.BlockSpec((1, d), lambda i: (0, 0)),
                  pl.BlockSpec((1, 1, d), bmap), pl.BlockSpec((1, 1, d), bmap),
                  pl.BlockSpec((128, d), lambda i: (0, 0)), pl.BlockSpec((128, d), lambda i: (0, 0)),
                  pl.BlockSpec((128, 1), lambda i: (0, 0))],
        out_specs=(pl.BlockSpec((8, tmr), lambda i: (0, i)),
                   pl.BlockSpec((ne, 128), lambda i: (0, 0)),
                   pl.BlockSpec((tmr, d // 2), lambda i: (i, 0))),
        scratch_shapes=[pltpu.VMEM((ne, 1), F32)],
        compiler_params=_cparams("arbitrary"), name="moe_router",
    )(x2d, g.reshape(1, d), sh[:, None, :], sc[:, None, :], rw_hi, rw_lo, rb)

    e1 = route[0].astype(jnp.int32)
    e2 = route[1].astype(jnp.int32)
    tile = EXPERT_TILE
    counts = cnt[:, 0].astype(jnp.int32)
    padded = ((counts + tile - 1) // tile) * tile
    ends = jnp.cumsum(padded)
    offs = ends - padded
    dest1 = offs[e1] + route[4].astype(jnp.int32)
    dest2 = offs[e2] + route[5].astype(jnp.int32)
    nt = (2 * t) // tile + ne
    n_used = ends[-1] // tile
    tid = jnp.arange(nt, dtype=jnp.int32)
    tsrc = jnp.minimum(tid, n_used - 1)
    texp = jnp.minimum(jnp.sum((tsrc[:, None] * tile >= ends[None, :]).astype(jnp.int32), axis=1), ne - 1)
    wts = jnp.concatenate([route[2:4].T, jnp.zeros((t, 6), F32)], axis=1)

    smem = pl.BlockSpec(memory_space=pltpu.SMEM)
    hbm = pl.BlockSpec(memory_space=pl.ANY)
    pk = d // 256
    slot, hpt = pl.pallas_call(
        functools.partial(_invert_kernel, n_tok=t, unroll=8),
        out_shape=(jax.ShapeDtypeStruct((nt * tile,), jnp.int32), jax.ShapeDtypeStruct((t, pk, 128), jnp.uint32)),
        in_specs=[smem] * 5 + [hbm], out_specs=(smem, hbm),
        scratch_shapes=[pltpu.SemaphoreType.DMA(())], name="moe_invert",
    )(dest1, dest2, counts, offs, ends, hp)

    ff = wg.shape[3]
    n_trash = ne * tile
    ys = pl.pallas_call(
        functools.partial(_expert_kernel, n_tok=t),
        out_shape=jax.ShapeDtypeStruct((2 * t + n_trash, pk, 128), jnp.uint32),
        grid_spec=pltpu.PrefetchScalarGridSpec(
            num_scalar_prefetch=3, grid=(nt,),
            in_specs=[pl.BlockSpec(memory_space=pl.ANY),
                      pl.BlockSpec((1, 1, d, ff), lambda i, te, nv, sl: (layer, te[i], 0, 0)),
                      pl.BlockSpec((1, 1, d, ff), lambda i, te, nv, sl: (layer, te[i], 0, 0)),
                      pl.BlockSpec((1, 1, ff, d), lambda i, te, nv, sl: (layer, te[i], 0, 0))],
            out_specs=pl.BlockSpec(memory_space=pl.ANY),
            scratch_shapes=[pltpu.VMEM((tile * pk, 128), jnp.uint32), pltpu.VMEM((tile * pk, 128), jnp.uint32),
                            pltpu.VMEM((tile * pk, 128), jnp.uint32), pltpu.VMEM((tile * pk, 128), jnp.uint32),
                            pltpu.VMEM((d, ff), BF16), pltpu.VMEM((d, ff), BF16), pltpu.VMEM((ff, d), BF16),
                            pltpu.SemaphoreType.DMA((2,)), pltpu.SemaphoreType.DMA((2,))]),
        compiler_params=_cparams("arbitrary"), name="moe_experts",
    )(texp, n_used.reshape(1).astype(jnp.int32), slot, hpt, wg, wu, wd)

    ys2 = ys.reshape(-1, 128)
    fin = final_w is not None
    fw = (final_w if fin else jnp.ones((d,), F32)).reshape(1, d)
    tmc = min(512, seq)
    nbc = t // tmc
    return pl.pallas_call(
        functools.partial(_combine_kernel, final_norm=fin),
        out_shape=jax.ShapeDtypeStruct((t, d), F32), grid=(nbc,),
        in_specs=[pl.BlockSpec((tmc, d), lambda i: (i, 0)),
                  pl.BlockSpec((tmc, 8), lambda i: (i, 0)),
                  pl.BlockSpec((1, 1, d), lambda i: ((i * tmc) // seq, 0, 0)),
                  pl.BlockSpec((1, d), lambda i: (0, 0)),
                  pl.BlockSpec((tmc * pk, 128), lambda i: (i, 0)),
                  pl.BlockSpec((tmc * pk, 128), lambda i: (i + nbc, 0))],
        out_specs=pl.BlockSpec((tmc, d), lambda i: (i, 0)),
        compiler_params=_cparams("arbitrary"), name="moe_combine",
    )(x2d, wts, gate[:, None, :], fw, ys2, ys2)


def _ssd_kernel(z_ref, x_ref, b_ref, c_ref, dt_ref, cwx_ref, cwb_ref, cwc_ref, cbx_ref, cbb_ref, cbc_ref,
                dtb_ref, alog_ref, dsk_ref, nw_ref, o_ref,
                xe_ref, be_ref, ce_ref, xs_ref, bs_ref, cs_ref, st_ref, acs_ref, dtg_ref):
    c = pl.program_id(1)
    lq = SSD_CHUNK
    ng = SSD_GROUPS
    gw = x_ref.shape[1] // ng
    hpg = gw // SSD_HEAD_DIM
    halo = xe_ref.shape[0] - lq

    @pl.when(c == 0)
    def _():
        xe_ref[0:halo] = jnp.zeros((halo, xe_ref.shape[1]), BF16)
        be_ref[0:halo] = jnp.zeros((halo, be_ref.shape[1]), BF16)
        ce_ref[0:halo] = jnp.zeros((halo, ce_ref.shape[1]), BF16)
        st_ref[...] = jnp.zeros_like(st_ref)

    xe_ref[halo:halo + lq] = x_ref[...]
    be_ref[halo:halo + lq] = b_ref[...]
    ce_ref[halo:halo + lq] = c_ref[...]

    sr = lax.broadcasted_iota(jnp.int32, (SSD_CONV * lq, halo + lq), 0)
    sm = lax.broadcasted_iota(jnp.int32, (SSD_CONV * lq, halo + lq), 1)
    tap = sr // lq
    shift_mat = jnp.where(sm == (sr - tap * lq) + halo - (SSD_CONV - 1) + tap, 1.0, 0.0).astype(BF16)

    def conv_silu(e_ref, w_ref, bias_ref, o_ref, width):
        for c0 in range(0, e_ref.shape[1], width):
            cols = slice(c0, c0 + width)
            sh = jnp.dot(shift_mat, e_ref[:, cols], preferred_element_type=F32)
            acc = bias_ref[:, cols]
            for k in range(SSD_CONV):
                acc = acc + sh[k * lq:(k + 1) * lq] * w_ref[k:k + 1, cols]
            o_ref[:, cols] = acc * jax.nn.sigmoid(acc)

    conv_silu(xe_ref, cwx_ref, cbx_ref, xs_ref, gw)
    conv_silu(be_ref, cwb_ref, cbb_ref, bs_ref, gw)
    conv_silu(ce_ref, cwc_ref, cbc_ref, cs_ref, gw)

    dt = jax.nn.softplus(dt_ref[...] + dtb_ref[...])
    adt = dt * (-jnp.exp(alog_ref[...]))
    li = lax.broadcasted_iota(jnp.int32, (lq, lq), 0)
    si = lax.broadcasted_iota(jnp.int32, (lq, lq), 1)
    tri = jnp.where(si <= li, 1.0, 0.0)
    acs = jnp.dot(tri, adt, precision=HIGHEST, preferred_element_type=F32)
    lane = lax.broadcasted_iota(jnp.int32, (lq, 128), 1)
    for gi in range(ng):
        sh = (128 - gi * hpg) % 128
        acs_ref[gi] = jnp.where(lane < hpg, pltpu.roll(acs, sh, axis=1) if sh else acs, 0.0)
        dtg_ref[gi] = jnp.where(lane < hpg, pltpu.roll(dt, sh, axis=1) if sh else dt, 0.0)
    causal = si <= li
    lane_lo = lane < SSD_HEAD_DIM

    def group_body(gi, carry):
        xc0 = pl.multiple_of(gi * gw, gw)
        nc0 = pl.multiple_of(gi * SSD_STATE, SSD_STATE)
        xg = xs_ref[:, pl.ds(xc0, gw)]
        bg = bs_ref[:, pl.ds(nc0, SSD_STATE)]
        cg = cs_ref[:, pl.ds(nc0, SSD_STATE)]
        bt = bg.T
        cgb = cg.astype(BF16)
        cb = jnp.dot(cgb, bt.astype(BF16), preferred_element_type=F32)
        acs_g = acs_ref[gi]
        acs_t = acs_g.T
        dt_t = dtg_ref[gi].T
        xgb = xg.astype(BF16)
        ys = []
        for jp in range(hpg // 2):
            lhs_parts, w_parts, ecols, dlast = [], [], [], []
            for j in (2 * jp, 2 * jp + 1):
                e_col = jnp.broadcast_to(acs_g[:, j:j + 1], (lq, lq))
                dt_row = dt_t[j:j + 1, :]
                r_row = jnp.broadcast_to(acs_t[j:j + 1, :] - jnp.log(dt_row), (lq, lq))
                lmat = jnp.where(causal, jnp.exp(e_col - r_row), 0.0) * cb
                lhs_parts.append(lmat.astype(BF16))
                a_last = acs_t[j:j + 1, lq - 1:lq]
                wrow = dt_row * jnp.exp(a_last - acs_t[j:j + 1, :])
                w_parts.append((bt * wrow).astype(BF16))
                ecols.append(e_col)
                dlast.append(jnp.exp(a_last))
            lhs = jnp.concatenate([jnp.concatenate(lhs_parts, axis=1),
                                   jnp.concatenate(w_parts, axis=1)], axis=0)
            xp = xgb[:, jp * 128:(jp + 1) * 128]
            zero = jnp.zeros_like(xp)
            rhs = jnp.concatenate([jnp.where(lane_lo, xp, zero), jnp.where(lane_lo, zero, xp)], axis=0)
            res = jnp.dot(lhs, rhs, preferred_element_type=F32)
            st = st_ref[gi, :, jp * 128:(jp + 1) * 128]
            yoff = jnp.dot(cgb, st.astype(BF16), preferred_element_type=F32)
            escale = jnp.exp(jnp.where(lane_lo, ecols[0], ecols[1]))
            ys.append(res[:lq] + yoff * escale)
            dl = jnp.where(lane_lo[:1], dlast[0], dlast[1])
            st_ref[gi, :, jp * 128:(jp + 1) * 128] = st * dl + res[lq:]
        y = jnp.concatenate(ys, axis=1) + dsk_ref[:, pl.ds(xc0, gw)] * xg
        zg = z_ref[:, pl.ds(xc0, gw)].astype(F32)
        y = y * (zg * jax.nn.sigmoid(zg))
        ms = jnp.mean(y * y, axis=-1, keepdims=True)
        y = y * lax.rsqrt(ms + EPS) * nw_ref[:, pl.ds(xc0, gw)]
        o_ref[:, pl.ds(xc0, gw)] = y.astype(o_ref.dtype)
        return carry

    lax.fori_loop(0, ng, group_body, 0)
    xe_ref[0:halo] = xe_ref[lq:lq + halo]
    be_ref[0:halo] = be_ref[lq:lq + halo]
    ce_ref[0:halo] = ce_ref[lq:lq + halo]


def _ssd_mixer(zx, dt_raw, bsz, seq, conv_w, conv_b, dt_bias, a_log, d_skip, norm_w):
    t = zx.shape[0]
    heads = a_log.shape[0]
    inner = heads * SSD_HEAD_DIM
    bcw = SSD_GROUPS * SSD_STATE
    lq = SSD_CHUNK
    nc = seq // lq
    pad = 128 - heads
    cw = conv_w.astype(F32)
    cwp = jnp.zeros((8, cw.shape[1]), F32).at[:SSD_CONV].set(cw)
    cb = conv_b.astype(F32).reshape(1, -1)
    row = lambda b, c: (b * nc + c, 0)
    const = lambda b, c: (0, 0)
    xb = inner // inner
    bb = (2 * inner) // bcw
    return pl.pallas_call(
        _ssd_kernel, out_shape=jax.ShapeDtypeStruct((t, inner), BF16), grid=(bsz, nc),
        in_specs=[pl.BlockSpec((lq, inner), lambda b, c: (b * nc + c, 0)),
                  pl.BlockSpec((lq, inner), lambda b, c: (b * nc + c, xb)),
                  pl.BlockSpec((lq, bcw), lambda b, c: (b * nc + c, bb)),
                  pl.BlockSpec((lq, bcw), lambda b, c: (b * nc + c, bb + 1)),
                  pl.BlockSpec((lq, 128), row),
                  pl.BlockSpec((8, inner), const), pl.BlockSpec((8, bcw), const), pl.BlockSpec((8, bcw), const),
                  pl.BlockSpec((1, inner), const), pl.BlockSpec((1, bcw), const), pl.BlockSpec((1, bcw), const),
                  pl.BlockSpec((1, 128), const), pl.BlockSpec((1, 128), const),
                  pl.BlockSpec((1, inner), const), pl.BlockSpec((1, inner), const)],
        out_specs=pl.BlockSpec((lq, inner), row),
        scratch_shapes=[pltpu.VMEM((lq + 16, inner), BF16), pltpu.VMEM((lq + 16, bcw), BF16),
                        pltpu.VMEM((lq + 16, bcw), BF16),
                        pltpu.VMEM((lq, inner), F32), pltpu.VMEM((lq, bcw), F32), pltpu.VMEM((lq, bcw), F32),
                        pltpu.VMEM((SSD_GROUPS, SSD_STATE, inner // SSD_GROUPS), F32),
                        pltpu.VMEM((SSD_GROUPS, lq, 128), F32), pltpu.VMEM((SSD_GROUPS, lq, 128), F32)],
        compiler_params=_cparams("arbitrary", "arbitrary"), name="ssd_mixer",
    )(zx, zx, zx, zx, dt_raw,
      cwp[:, :inner], cwp[:, inner:inner + bcw], cwp[:, inner + bcw:],
      cb[:, :inner], cb[:, inner:inner + bcw], cb[:, inner + bcw:],
      jnp.pad(dt_bias.astype(F32), (0, pad)).reshape(1, 128),
      jnp.pad(a_log.astype(F32), (0, pad)).reshape(1, 128),
      jnp.repeat(d_skip.astype(F32), SSD_HEAD_DIM).reshape(1, inner),
      norm_w.astype(F32).reshape(1, inner))


def kernel(x, c, positions, ada_w, ada_b, norm_mix, norm_ffn, mix_a_w_in, s5_lam_re, s5_lam_im, s5_log_dt, s5_b_re, s5_b_im, s5_c_re, s5_c_im, s5_d, s5_w_glu, attn_sinks, mix_a_w_out, ssd_w_in, ssd_conv_w, ssd_conv_b, ssd_dt_bias, ssd_a_log, ssd_d, ssd_norm_w, ssd_w_out, moe_r1_w, moe_r1_b, moe_r2_w, moe_r2_b, moe_w_gate, moe_w_up, moe_w_down, final_norm):
    bsz, seq, d = x.shape
    depth = ada_w.shape[0]
    t = bsz * seq
    xf = x.reshape(t, d).astype(F32)
    mod = _adaln(c.astype(F32), ada_w, ada_b)
    cos_t, sin_t = _rope_tables(positions)

    for i in range(depth):
        sh1, sc1, g1, sh2, sc2, g2 = jnp.split(mod[i], 6, axis=-1)
        j = i // 2
        if i % 2 == 0:
            s5w = s5_lam_re.shape[1] * S5_GROUP
            w_in = mix_a_w_in[j]
            u, qkv = _inproj0(xf, seq, norm_mix[i], sh1, sc1, w_in[:, :s5w], w_in[:, s5w:], tm=512)
            params = _s5_params(s5_lam_re[j], s5_lam_im[j], s5_log_dt[j], s5_b_re[j], s5_b_im[j],
                                s5_c_re[j], s5_c_im[j], s5_d[j], seq)
            y_s5 = _s5_mixer(u, bsz, seq, params)
            y_glu = _gelu_glu(y_s5, s5_w_glu[j], tm=1024, tn=512)
            wq = attn_sinks.shape[1] * HEAD_DIM
            wk = KV_HEADS * HEAD_DIM
            y_att = _attention(qkv, bsz, seq, cos_t, sin_t, attn_sinks[j],
                               q_col=0, k_col=wq // wk, v_col=wq // wk + 1)
            xf = _matmul_residual([y_glu, y_att], mix_a_w_out[j], xf, seq, g1, tm=1024, tn=1024)
        else:
            heads = ssd_a_log.shape[1]
            inner = heads * SSD_HEAD_DIM
            main = inner + inner + 2 * SSD_GROUPS * SSD_STATE
            w_in = ssd_w_in[j]
            w_dt = jnp.pad(w_in[:, main:], ((0, 0), (0, 128 - heads))).astype(BF16)
            zx, dt_raw = _inproj1(xf, seq, norm_mix[i], sh1, sc1, w_in, main, w_dt, tm=1024, tn=1024)
            y = _ssd_mixer(zx, dt_raw, bsz, seq, ssd_conv_w[j], ssd_conv_b[j], ssd_dt_bias[j],
                           ssd_a_log[j], ssd_d[j], ssd_norm_w[j])
            xf = _matmul_residual([y], ssd_w_out[j], xf, seq, g1, tm=1024, tn=512)
        xf = _hier_moe(xf, seq, norm_ffn[i], sh2, sc2, g2, moe_r1_w[i], moe_r1_b[i], moe_r2_w[i], moe_r2_b[i],
                       moe_w_gate, moe_w_up, moe_w_down, i,
                       final_norm if i == depth - 1 else None)
    return xf.reshape(bsz, seq, d).astype(x.dtype)
```

```python
import functools
import math

import jax
import jax.numpy as jnp
from jax import lax
from jax.experimental import pallas as pl
from jax.experimental.pallas import tpu as pltpu

F32 = jnp.float32
BF16 = jnp.bfloat16
HIGHEST = lax.Precision.HIGHEST

EPS = 1e-6
S5_GROUP = 16
S5_STATE = 64
S5_CHUNK = 16
HEAD_DIM = 64
KV_HEADS = 2
WINDOW = 128
ROPE_DIM = 16
ROPE_THETA = 500000.0
SSD_HEAD_DIM = 64
SSD_GROUPS = 8
SSD_STATE = 128
SSD_CONV = 4
SSD_CHUNK = 128
MOE_GROUPS = 4
MOE_EPG = 8
MOE_EXPERTS = MOE_GROUPS * MOE_EPG
EXPERT_TILE = 256
NEG_BIG = -1e30
VMEM_LIMIT = 56 * 1024 * 1024


def _cparams(*sem):
    return pltpu.CompilerParams(dimension_semantics=sem, vmem_limit_bytes=VMEM_LIMIT)


def _modnorm(x, g, sh, sc):
    ms = jnp.mean(x * x, axis=-1, keepdims=True)
    return (x * lax.rsqrt(ms + EPS) * g) * (1.0 + sc) + sh


def _adaln_kernel(c_ref, w_ref, b_ref, o_ref):
    c = c_ref[...]
    a = c * jax.nn.sigmoid(c)
    o_ref[0] = jnp.dot(a, w_ref[0], precision=HIGHEST, preferred_element_type=F32) + b_ref[0]


def _adaln(c, ada_w, ada_b):
    depth, d, n = ada_w.shape
    bsz = c.shape[0]
    rows = 8
    cp = jnp.zeros((rows, d), F32).at[:bsz].set(c)
    tn = 1024
    out = pl.pallas_call(
        _adaln_kernel,
        out_shape=jax.ShapeDtypeStruct((depth, rows, n), F32),
        grid=(depth, n // tn),
        in_specs=[pl.BlockSpec((rows, d), lambda l, j: (0, 0)),
                  pl.BlockSpec((1, d, tn), lambda l, j: (l, 0, j)),
                  pl.BlockSpec((1, 1, tn), lambda l, j: (l, 0, j))],
        out_specs=pl.BlockSpec((1, rows, tn), lambda l, j: (l, 0, j)),
        compiler_params=_cparams("arbitrary", "arbitrary"),
        name="adaln",
    )(cp, ada_w, ada_b.reshape(depth, 1, n))
    return out[:, :bsz]


def _inproj1_kernel(x_ref, g_ref, sh_ref, sc_ref, w_ref, w2_ref, o_ref, o2_ref, h_ref):
    @pl.when(pl.program_id(1) == 0)
    def _():
        h = _modnorm(x_ref[...], g_ref[...], sh_ref[0], sc_ref[0]).astype(BF16)
        h_ref[...] = h
        o2_ref[...] = jnp.dot(h, w2_ref[...], preferred_element_type=F32)

    o_ref[...] = jnp.dot(h_ref[...], w_ref[...].astype(BF16), preferred_element_type=F32).astype(o_ref.dtype)


def _inproj1(x2d, seq, g, sh, sc, w, n_main, w2, tm, tn):
    t, d = x2d.shape
    tm = min(tm, seq)
    n2 = w2.shape[1]
    bmap = lambda i, j: ((i * tm) // seq, 0, 0)
    return pl.pallas_call(
        _inproj1_kernel,
        out_shape=(jax.ShapeDtypeStruct((t, n_main), BF16), jax.ShapeDtypeStruct((t, n2), F32)),
        grid=(t // tm, n_main // tn),
        in_specs=[pl.BlockSpec((tm, d), lambda i, j: (i, 0)),
                  pl.BlockSpec((1, d), lambda i, j: (0, 0)),
                  pl.BlockSpec((1, 1, d), bmap), pl.BlockSpec((1, 1, d), bmap),
                  pl.BlockSpec((d, tn), lambda i, j: (0, j)),
                  pl.BlockSpec((d, n2), lambda i, j: (0, 0))],
        out_specs=(pl.BlockSpec((tm, tn), lambda i, j: (i, j)), pl.BlockSpec((tm, n2), lambda i, j: (i, 0))),
        scratch_shapes=[pltpu.VMEM((tm, d), BF16)],
        compiler_params=_cparams("arbitrary", "arbitrary"), name="inproj1",
    )(x2d, g.reshape(1, d), sh[:, None, :], sc[:, None, :], w, w2)


def _mmres_kernel(*refs, n_in):
    a_refs, w_refs = refs[:n_in], refs[n_in:2 * n_in]
    x_ref, gt_ref, o_ref = refs[2 * n_in:]
    acc = jnp.dot(a_refs[0][...], w_refs[0][...].astype(BF16), preferred_element_type=F32)
    for a_ref, w_ref in zip(a_refs[1:], w_refs[1:]):
        acc = acc + jnp.dot(a_ref[...], w_ref[...].astype(BF16), preferred_element_type=F32)
    o_ref[...] = x_ref[...] + gt_ref[0] * acc


def _matmul_residual(a_list, w, x2d, seq, gate, tm, tn):
    t, n = x2d.shape
    tm = min(tm, seq)
    in_specs, row = [], 0
    for a in a_list:
        in_specs.append(pl.BlockSpec((tm, a.shape[1]), lambda i, j: (i, 0)))
    for a in a_list:
        k = a.shape[1]
        in_specs.append(pl.BlockSpec((k, tn), functools.partial(lambda i, j, rb: (rb, j), rb=row // k)))
        row += k
    in_specs.append(pl.BlockSpec((tm, tn), lambda i, j: (i, j)))
    in_specs.append(pl.BlockSpec((1, 1, tn), lambda i, j: ((i * tm) // seq, 0, j)))
    return pl.pallas_call(
        functools.partial(_mmres_kernel, n_in=len(a_list)),
        out_shape=jax.ShapeDtypeStruct((t, n), F32), grid=(t // tm, n // tn),
        in_specs=in_specs, out_specs=pl.BlockSpec((tm, tn), lambda i, j: (i, j)),
        compiler_params=_cparams("arbitrary", "arbitrary"), name="matmul_residual",
    )(*a_list, *([w] * len(a_list)), x2d, gate[:, None, :])


def _inproj0_kernel(x_ref, g_ref, sh_ref, sc_ref, wu_ref, wa_ref, u_ref, a_ref, wub_ref, wab_ref):
    @pl.when(pl.program_id(0) == 0)
    def _():
        wub_ref[...] = wu_ref[...].astype(BF16)
        wab_ref[...] = wa_ref[...].astype(BF16)

    h = _modnorm(x_ref[...], g_ref[...], sh_ref[0], sc_ref[0]).astype(BF16)
    u_ref[...] = jnp.dot(h, wub_ref[...], preferred_element_type=F32)
    a_ref[...] = jnp.dot(h, wab_ref[...], preferred_element_type=F32).astype(a_ref.dtype)


def _inproj0(x2d, seq, g, sh, sc, w_u, w_a, tm):
    t, d = x2d.shape
    tm = min(tm, seq)
    nu, na = w_u.shape[1], w_a.shape[1]
    bmap = lambda i: ((i * tm) // seq, 0, 0)
    return pl.pallas_call(
        _inproj0_kernel,
        out_shape=(jax.ShapeDtypeStruct((t, nu), F32), jax.ShapeDtypeStruct((t, na), BF16)),
        grid=(t // tm,),
        in_specs=[pl.BlockSpec((tm, d), lambda i: (i, 0)),
                  pl.BlockSpec((1, d), lambda i: (0, 0)),
                  pl.BlockSpec((1, 1, d), bmap), pl.BlockSpec((1, 1, d), bmap),
                  pl.BlockSpec((d, nu), lambda i: (0, 0), pipeline_mode=pl.Buffered(1)),
                  pl.BlockSpec((d, na), lambda i: (0, 0), pipeline_mode=pl.Buffered(1))],
        out_specs=(pl.BlockSpec((tm, nu), lambda i: (i, 0)), pl.BlockSpec((tm, na), lambda i: (i, 0))),
        scratch_shapes=[pltpu.VMEM((d, nu), BF16), pltpu.VMEM((d, na), BF16)],
        compiler_params=_cparams("arbitrary"), name="inproj0",
    )(x2d, g.reshape(1, d), sh[:, None, :], sc[:, None, :], w_u, w_a)


S5_SLAB = 8
S5_SEGS = 8


def _s5_params(lam_re, lam_im, log_dt, b_re, b_im, c_re, c_im, d_skip, seq):
    g, n = lam_re.shape
    p = S5_GROUP
    lc = S5_CHUNK
    gs = S5_SLAB
    nj = g // gs
    lam = lax.complex(lam_re.astype(F32), lam_im.astype(F32))
    dt = jnp.exp(log_dt.astype(F32))[:, None]
    ld = lam * dt
    lam_bar = jnp.exp(ld)
    bmat = lax.complex(b_re.astype(F32), b_im.astype(F32))
    b_bar = ((lam_bar - 1.0) / lam)[..., None] * bmat
    cmat = lax.complex(c_re.astype(F32), c_im.astype(F32))
    ks = jnp.arange(lc, dtype=F32)
    pw = jnp.exp(ld[:, None, :] * ks[None, :, None])
    pw1 = jnp.exp(ld[:, None, :] * (ks[None, :, None] + 1.0))
    kk = jnp.real(jnp.einsum('gpn,gkn,gnq->gkpq', cmat, pw, b_bar, precision=HIGHEST))
    eg = jnp.eye(gs, dtype=F32)
    ex_p = jnp.einsum('ab,pr->apbr', eg, jnp.eye(p, dtype=F32)).reshape(gs, p, gs * p)
    ex_n = jnp.einsum('ab,nr->anbr', eg, jnp.eye(n, dtype=F32)).reshape(gs, n, gs * n)
    kq_ = kk.transpose(0, 1, 3, 2).reshape(nj, gs, lc, p, p)
    kexp = jnp.einsum('jakqp,apc->jkaqc', kq_, ex_p, precision=HIGHEST).reshape(nj, lc, gs * p, gs * p)
    kexp = kexp.at[:, 0].add(jnp.eye(gs * p, dtype=F32)[None] * d_skip.astype(F32).reshape(nj, 1, gs * p))
    kexp = kexp.astype(BF16)
    wb = (pw[:, ::-1, :, None] * b_bar[:, None, :, :]).transpose(0, 1, 3, 2)
    wb = wb.reshape(nj, gs, lc, p, n)
    wb_slab = jnp.concatenate(
        [jnp.einsum('jasqn,anc->jsaqc', part.astype(BF16), ex_n.astype(BF16)).reshape(nj, lc * gs * p, gs * n)
         for part in (jnp.real(wb), jnp.imag(wb))], axis=-1).astype(BF16)
    cl = (cmat[:, None, :, :] * pw1[:, :, None, :]).transpose(0, 3, 1, 2)
    cl = cl.reshape(nj, gs, n, lc, p)
    woff_slab = jnp.concatenate(
        [jnp.einsum('janlp,apc->janlc', part.astype(BF16), ex_p.astype(BF16)).reshape(nj, gs * n, lc * gs * p)
         for part in (jnp.real(cl), -jnp.imag(cl))], axis=1).astype(BF16)
    seglen = (seq // lc) // S5_SEGS
    kq = jnp.arange(seglen + 1, dtype=F32) * float(lc)
    pq = jnp.exp(ld[:, None, :] * kq[None, :, None]).reshape(nj, gs, seglen + 1, n)
    pq = pq.transpose(0, 2, 1, 3).reshape(nj, seglen + 1, gs * n)
    return kexp, wb_slab, woff_slab, jnp.real(pq), jnp.imag(pq)


def _s5_kernel(u_ref, k_ref, wb_ref, woff_ref, pre_ref, pim_ref, y_ref, sloc_ref, sin_ref, m_ref):
    lc = S5_CHUNK
    w = u_ref.shape[1]

    @pl.when((pl.program_id(0) == 0) & (pl.program_id(1) == 0))
    def _():
        m_ref[...] = jnp.zeros_like(m_ref)

    @pl.when(pl.program_id(1) == 0)
    def _():
        for s in range(lc):
            for l in range(s, lc):
                m_ref[s * w:(s + 1) * w, l * w:(l + 1) * w] = k_ref[0, l - s]

    nc = u_ref.shape[0] // lc
    hw = wb_ref.shape[2] // 2
    seglen = nc // S5_SEGS
    u_all = jnp.concatenate([u_ref[pl.ds(s, nc, stride=lc), :] for s in range(lc)], axis=1).astype(BF16)
    nq = hw // w
    sloc = jnp.dot(u_all, wb_ref[0], preferred_element_type=F32)
    for q in range(2 * nq):
        sloc_ref[q] = sloc[:, q * w:(q + 1) * w]
    a_re, a_im = pre_ref[0, 1:2], pim_ref[0, 1:2]

    def put(rows, re, im):
        for q in range(nq):
            sin_ref[q, rows, :] = re[:, q * w:(q + 1) * w]
            sin_ref[nq + q, rows, :] = im[:, q * w:(q + 1) * w]

    def get(ref, rows):
        return (jnp.concatenate([ref[q, rows, :] for q in range(nq)], axis=1),
                jnp.concatenate([ref[nq + q, rows, :] for q in range(nq)], axis=1))

    def seg_step(k, st):
        sre, sim = st
        rows = pl.ds(k, S5_SEGS, stride=seglen)
        put(rows, sre, sim)
        lre, lim = get(sloc_ref, rows)
        return a_re * sre - a_im * sim + lre, a_re * sim + a_im * sre + lim

    z = jnp.zeros((S5_SEGS, hw), F32)
    ere, eim = lax.fori_loop(0, seglen, seg_step, (z, z))
    g_re, g_im = pre_ref[0, seglen:seglen + 1], pim_ref[0, seglen:seglen + 1]
    cre, cim = jnp.zeros((1, hw), F32), jnp.zeros((1, hw), F32)
    pre, pim = pre_ref[0, 0:seglen], pim_ref[0, 0:seglen]
    for sg in range(1, S5_SEGS):
        cre, cim = (g_re * cre - g_im * cim + ere[sg - 1:sg], g_re * cim + g_im * cre + eim[sg - 1:sg])
        rows = slice(sg * seglen, (sg + 1) * seglen)
        cur_re, cur_im = get(sin_ref, rows)
        put(rows, cur_re + pre * cre - pim * cim, cur_im + pre * cim + pim * cre)
    s_in = jnp.concatenate([sin_ref[q] for q in range(2 * nq)], axis=1).astype(BF16)
    for lb in range(lc // 2):
        c0, c1 = 2 * lb * w, (2 * lb + 2) * w
        y2 = (jnp.dot(u_all[:, :c1], m_ref[:c1, c0:c1], preferred_element_type=F32)
              + jnp.dot(s_in, woff_ref[0, :, c0:c1], preferred_element_type=F32))
        y_ref[pl.ds(2 * lb, nc, stride=lc), :] = y2[:, :w]
        y_ref[pl.ds(2 * lb + 1, nc, stride=lc), :] = y2[:, w:]


def _s5_mixer(u, bsz, seq, params):
    kexp, wb_slab, woff_slab, pq_re, pq_im = params
    nj, km, sw = wb_slab.shape
    w = km // S5_CHUNK
    nc = seq // S5_CHUNK
    t = u.shape[0]
    const = lambda j, b: (j, 0, 0)
    return pl.pallas_call(
        _s5_kernel, out_shape=jax.ShapeDtypeStruct((t, nj * w), F32), grid=(nj, bsz),
        in_specs=[pl.BlockSpec((seq, w), lambda j, b: (b, j)),
                  pl.BlockSpec((1, S5_CHUNK, w, w), lambda j, b: (j, 0, 0, 0)),
                  pl.BlockSpec((1, km, sw), const),
                  pl.BlockSpec((1, sw, km), const),
                  pl.BlockSpec((1,) + pq_re.shape[1:], const),
                  pl.BlockSpec((1,) + pq_im.shape[1:], const)],
        out_specs=pl.BlockSpec((seq, w), lambda j, b: (b, j)),
        scratch_shapes=[pltpu.VMEM((sw // w, nc, w), F32), pltpu.VMEM((sw // w, nc, w), F32),
                        pltpu.VMEM((km, km), BF16)],
        compiler_params=_cparams("arbitrary", "arbitrary"), name="s5_mixer",
    )(u, kexp, wb_slab, woff_slab, pq_re, pq_im)


def _glu_kernel(y_ref, wv_ref, wg_ref, o_ref, a_ref):
    @pl.when(pl.program_id(1) == 0)
    def _():
        a_ref[...] = jax.nn.gelu(y_ref[...].astype(F32), approximate=True).astype(BF16)

    a = a_ref[...]
    val = jnp.dot(a, wv_ref[...].astype(BF16), preferred_element_type=F32)
    gate = jnp.dot(a, wg_ref[...].astype(BF16), preferred_element_type=F32)
    o_ref[...] = (val * jax.nn.sigmoid(gate)).astype(o_ref.dtype)


def _gelu_glu(y, w, tm, tn):
    t, k = y.shape
    n = w.shape[1] // 2
    tm = min(tm, t)
    nb = n // tn
    return pl.pallas_call(
        _glu_kernel, out_shape=jax.ShapeDtypeStruct((t, n), BF16), grid=(t // tm, n // tn),
        in_specs=[pl.BlockSpec((tm, k), lambda i, j: (i, 0)),
                  pl.BlockSpec((k, tn), lambda i, j: (0, j)),
                  pl.BlockSpec((k, tn), lambda i, j: (0, j + nb))],
        out_specs=pl.BlockSpec((tm, tn), lambda i, j: (i, j)),
        scratch_shapes=[pltpu.VMEM((tm, k), BF16)],
        compiler_params=_cparams("arbitrary", "arbitrary"), name="gelu_glu",
    )(y, w, w)


def _rope(x, cos, sin):
    w = x.shape[1]
    half = ROPE_DIM // 2
    d = lax.broadcasted_iota(jnp.int32, x.shape, 1) % HEAD_DIM
    partner = jnp.where(d < half, pltpu.roll(x, w - half, axis=1), pltpu.roll(x, half, axis=1))
    return x * cos + partner * sin


def _attn_kernel(sink_ref, q_ref, kc_ref, kp_ref, vc_ref, vp_ref,
                 cq_ref, sq_ref, cp_ref, sp_ref, o_ref):
    n = pl.program_id(1)
    wq = q_ref.shape[1]
    hq = wq // HEAD_DIM
    grp = hq // KV_HEADS
    cos_c, sin_c = cq_ref[...], sq_ref[...]
    reps = wq // cos_c.shape[1]
    q = _rope(q_ref[...].astype(F32),
              jnp.concatenate([cos_c] * reps, axis=1), jnp.concatenate([sin_c] * reps, axis=1))
    q = (q * (HEAD_DIM ** -0.5)).astype(BF16)
    kc = _rope(kc_ref[...].astype(F32), cos_c, sin_c).astype(BF16)
    kp = _rope(kp_ref[...].astype(F32), cp_ref[...], sp_ref[...]).astype(BF16)
    kk = jnp.concatenate([kp, kc], axis=0)
    vv = jnp.concatenate([vp_ref[...], vc_ref[...]], axis=0)
    qi = lax.broadcasted_iota(jnp.int32, (WINDOW, 2 * WINDOW), 0)
    kj = lax.broadcasted_iota(jnp.int32, (WINDOW, 2 * WINDOW), 1)
    first = jnp.where(n > 0, 0, WINDOW)
    valid = (kj > qi) & (kj <= qi + WINDOW) & (kj >= first)
    outs = []
    for h in range(hq):
        kvh = h // grp
        qh = q[:, h * HEAD_DIM:(h + 1) * HEAD_DIM]
        kh = kk[:, kvh * HEAD_DIM:(kvh + 1) * HEAD_DIM]
        vh = vv[:, kvh * HEAD_DIM:(kvh + 1) * HEAD_DIM]
        s = lax.dot_general(qh, kh, (((1,), (1,)), ((), ())), preferred_element_type=F32)
        s = jnp.where(valid, s, NEG_BIG)
        sink = sink_ref[h]
        mx = jnp.maximum(jnp.max(s, axis=-1, keepdims=True), sink)
        p = jnp.exp(s - mx)
        den = jnp.sum(p, axis=-1, keepdims=True) + jnp.exp(sink - mx)
        o = jnp.dot(p.astype(BF16), vh, preferred_element_type=F32)
        outs.append(o / den)
    o_ref[...] = jnp.concatenate(outs, axis=1).astype(o_ref.dtype)


def _attention(proj, bsz, seq, cos_t, sin_t, sinks, q_col, k_col, v_col):
    t = proj.shape[0]
    nb = seq // WINDOW
    wq = sinks.shape[0] * HEAD_DIM
    wk = KV_HEADS * HEAD_DIM
    cur = lambda b, n, s: (b * nb + n, 0)
    prv = lambda b, n, s: (b * nb + jnp.maximum(n - 1, 0), 0)
    return pl.pallas_call(
        _attn_kernel, out_shape=jax.ShapeDtypeStruct((t, wq), BF16),
        grid_spec=pltpu.PrefetchScalarGridSpec(
            num_scalar_prefetch=1, grid=(bsz, nb),
            in_specs=[pl.BlockSpec((WINDOW, wq), lambda b, n, s: (b * nb + n, q_col)),
                      pl.BlockSpec((WINDOW, wk), lambda b, n, s: (b * nb + n, k_col)),
                      pl.BlockSpec((WINDOW, wk), lambda b, n, s: (b * nb + jnp.maximum(n - 1, 0), k_col)),
                      pl.BlockSpec((WINDOW, wk), lambda b, n, s: (b * nb + n, v_col)),
                      pl.BlockSpec((WINDOW, wk), lambda b, n, s: (b * nb + jnp.maximum(n - 1, 0), v_col)),
                      pl.BlockSpec((WINDOW, wk), cur), pl.BlockSpec((WINDOW, wk), cur),
                      pl.BlockSpec((WINDOW, wk), prv), pl.BlockSpec((WINDOW, wk), prv)],
            out_specs=pl.BlockSpec((WINDOW, wq), lambda b, n, s: (b * nb + n, 0))),
        compiler_params=_cparams("arbitrary", "arbitrary"), name="swa_attention",
    )(sinks.astype(F32), proj, proj, proj, proj, proj, cos_t, sin_t, cos_t, sin_t)


def _rope_tables(positions):
    half = ROPE_DIM // 2
    inv = 1.0 / (ROPE_THETA ** (jnp.arange(0, ROPE_DIM, 2, dtype=F32) / ROPE_DIM))
    ang = positions.astype(F32).reshape(-1, 1) * inv[None, :]
    cos, sin = jnp.cos(ang), jnp.sin(ang)
    t = ang.shape[0]
    ones = jnp.ones((t, HEAD_DIM - ROPE_DIM), F32)
    cos_h = jnp.concatenate([cos, cos, ones], axis=1)
    sin_h = jnp.concatenate([-sin, sin, 0.0 * ones], axis=1)
    return jnp.tile(cos_h, (1, KV_HEADS)), jnp.tile(sin_h, (1, KV_HEADS))


def _pack_bf16_pairs(v):
    half = v.shape[1] // 2
    hi = lax.bitcast_convert_type(v[:, :half].astype(BF16).astype(F32), jnp.uint32)
    lo = lax.bitcast_convert_type(v[:, half:].astype(BF16).astype(F32), jnp.uint32)
    return hi | (lo >> 16)


def _unpack_bf16_pairs(p):
    return jnp.concatenate([lax.bitcast_convert_type(p & jnp.uint32(0xFFFF0000), F32),
                            lax.bitcast_convert_type(p << 16, F32)], axis=1)


def _router_kernel(x_ref, g_ref, sh_ref, sc_ref, rwh_ref, rwl_ref, rb_ref, o_ref, cnt_ref, hp_ref, carry_ref):
    i = pl.program_id(0)
    tm = x_ref.shape[0]
    ne = MOE_EXPERTS

    @pl.when(i == 0)
    def _():
        carry_ref[...] = jnp.zeros_like(carry_ref)

    h = _modnorm(x_ref[...], g_ref[...], sh_ref[0], sc_ref[0])
    hp_ref[...] = _pack_bf16_pairs(h)
    nt_dims = (((1,), (1,)), ((), ()))
    h_hi = h.astype(BF16)
    h_lo = (h - h_hi.astype(F32)).astype(BF16)
    lt = (lax.dot_general(rwh_ref[...], h_hi, nt_dims, preferred_element_type=F32)
          + lax.dot_general(rwh_ref[...], h_lo, nt_dims, preferred_element_type=F32)
          + lax.dot_general(rwl_ref[...], h_hi, nt_dims, preferred_element_type=F32)) + rb_ref[...]
    gl = lt[0:MOE_GROUPS]
    gm = jnp.max(gl, axis=0, keepdims=True)
    gsum = jnp.sum(jnp.exp(gl - gm), axis=0, keepdims=True)
    g_val = 1.0 / gsum
    grow = lax.broadcasted_iota(jnp.int32, gl.shape, 0)
    g_idx = jnp.min(jnp.where(gl == gm, grow, MOE_GROUPS), axis=0, keepdims=True)
    sel = jnp.zeros((MOE_EPG, tm), F32)
    for gi in range(MOE_GROUPS):
        sel = jnp.where(g_idx == gi, lt[8 + gi * MOE_EPG: 8 + (gi + 1) * MOE_EPG], sel)
    erow = lax.broadcasted_iota(jnp.int32, sel.shape, 0)
    m1 = jnp.max(sel, axis=0, keepdims=True)
    i1 = jnp.min(jnp.where(sel == m1, erow, MOE_EPG), axis=0, keepdims=True)
    sel2 = jnp.where(erow == i1, -jnp.inf, sel)
    m2 = jnp.max(sel2, axis=0, keepdims=True)
    i2 = jnp.min(jnp.where(sel2 == m2, erow, MOE_EPG), axis=0, keepdims=True)
    e21 = jnp.exp(m2 - m1)
    w1 = g_val / (1.0 + e21)
    w2 = g_val * e21 / (1.0 + e21)
    e1 = g_idx * MOE_EPG + i1
    e2 = g_idx * MOE_EPG + i2
    xrow = lax.broadcasted_iota(jnp.int32, (ne, tm), 0)
    oh1 = xrow == e1
    oh2 = xrow == e2
    cmat = jnp.where(oh1 | oh2, 1.0, 0.0)
    ti = lax.broadcasted_iota(jnp.int32, (tm, tm), 0)
    tj = lax.broadcasted_iota(jnp.int32, (tm, tm), 1)
    su = jnp.where(ti < tj, 1.0, 0.0).astype(BF16)
    before = jnp.dot(cmat.astype(BF16), su, preferred_element_type=F32) + carry_ref[...]
    r1 = jnp.sum(jnp.where(oh1, before, 0.0), axis=0, keepdims=True)
    r2 = jnp.sum(jnp.where(oh2, before, 0.0), axis=0, keepdims=True)
    carry = carry_ref[...] + jnp.sum(cmat, axis=1, keepdims=True)
    carry_ref[...] = carry
    cnt_ref[...] = jnp.broadcast_to(carry, cnt_ref.shape)
    o_ref[...] = jnp.concatenate([e1.astype(F32), e2.astype(F32), w1, w2, r1, r2,
                                  jnp.zeros((2, tm), F32)], axis=0)


def _invert_kernel(d1_ref, d2_ref, cnt_ref, off_ref, end_ref, slot_ref, *, n_tok, unroll):
    ne = cnt_ref.shape[0]

    def pad_expert(e, p):
        lo = off_ref[e] + cnt_ref[e]

        def pad_row(r, p):
            slot_ref[r] = 2 * n_tok + p
            return p + 1

        return lax.fori_loop(lo, end_ref[e], pad_row, p)

    lax.fori_loop(0, ne, pad_expert, 0)

    def tok_block(i, c):
        for j in range(unroll):
            t = i * unroll + j
            slot_ref[d1_ref[t]] = t
            slot_ref[d2_ref[t]] = n_tok + t
        return c

    lax.fori_loop(0, n_tok // unroll, tok_block, 0)


def _expert_kernel(te_ref, nv_ref, slot_ref, hp_ref, wg_ref, wu_ref, wd_ref, ys_ref,
                   xb0_ref, xb1_ref, yb0_ref, yb1_ref, wgb_ref, wub_ref, wdb_ref, gsem, ssem, *, n_tok):
    i = pl.program_id(0)
    nv = nv_ref[0]
    tile = xb0_ref.shape[0]
    pk = ys_ref.shape[1]

    def gather(tile_idx, xb_ref, sem):
        for r in range(tile):
            s = slot_ref[tile_idx * tile + r]
            tok = jnp.where(s >= 2 * n_tok, 0, jnp.where(s >= n_tok, s - n_tok, s))
            pltpu.make_async_copy(hp_ref.at[pl.ds(tok, 1)], xb_ref.at[pl.ds(r, 1)], sem).start(priority=1)

    def wait_gather(xb_ref, sem):
        pltpu.make_async_copy(xb_ref, xb_ref, sem).wait()

    def scatter(tile_idx, yb_ref, sem):
        for r in range(tile):
            s = slot_ref[tile_idx * tile + r]
            pltpu.make_async_copy(yb_ref.at[pl.ds(r * pk, pk)], ys_ref.at[s], sem).start(priority=r % 2)

    def wait_scatter(yb_ref, sem):
        pltpu.make_async_copy(yb_ref, yb_ref, sem).wait()

    new_expert = (i == 0) | (te_ref[i] != te_ref[jnp.maximum(i - 1, 0)])

    @pl.when(new_expert & (i < nv))
    def _():
        wgb_ref[...] = wg_ref[0, 0].astype(BF16)
        wub_ref[...] = wu_ref[0, 0].astype(BF16)
        wdb_ref[...] = wd_ref[0, 0].astype(BF16)

    @pl.when(i == 0)
    def _():
        gather(0, xb0_ref, gsem.at[0])

    def step(p, xb_ref, xb_next_ref, yb_ref, yb_other_ref):
        wait_gather(xb_ref, gsem.at[p])

        @pl.when(i >= 2)
        def _():
            wait_scatter(yb_ref, ssem.at[p])

        x = _unpack_bf16_pairs(xb_ref[...]).astype(BF16)
        gather(jnp.minimum(i + 1, nv - 1), xb_next_ref, gsem.at[1 - p])
        hg = jnp.dot(x, wgb_ref[...], preferred_element_type=F32)
        hu = jnp.dot(x, wub_ref[...], preferred_element_type=F32)
        hid = (hg * jax.nn.sigmoid(hg) * hu).astype(BF16)
        y = jnp.dot(hid, wdb_ref[...], preferred_element_type=F32)
        yp = _pack_bf16_pairs(y)
        lanes = yb_ref.shape[1]
        for j in range(pk):
            yb_ref[pl.ds(j, tile, stride=pk), :] = yp[:, j * lanes:(j + 1) * lanes]
        scatter(i, yb_ref, ssem.at[p])

        @pl.when(i == nv - 1)
        def _():
            wait_gather(xb_next_ref, gsem.at[1 - p])
            wait_scatter(yb_ref, ssem.at[p])

            @pl.when(i >= 1)
            def _():
                wait_scatter(yb_other_ref, ssem.at[1 - p])

    odd = lax.rem(i, 2) == 1

    @pl.when((i < nv) & jnp.logical_not(odd))
    def _():
        step(0, xb0_ref, xb1_ref, yb0_ref, yb1_ref)

    @pl.when((i < nv) & odd)
    def _():
        step(1, xb1_ref, xb0_ref, yb1_ref, yb0_ref)


def _combine_kernel(x_ref, w_ref, gt_ref, fn_ref, ya_ref, yb_ref, o_ref, *, final_norm):
    w = w_ref[...]
    tm = x_ref.shape[0]
    pk = ya_ref.shape[0] // tm

    def rows(ref):
        return _unpack_bf16_pairs(jnp.concatenate([ref[pl.ds(j, tm, stride=pk), :] for j in range(pk)], axis=1))

    y = w[:, 0:1] * rows(ya_ref) + w[:, 1:2] * rows(yb_ref)
    out = x_ref[...] + gt_ref[0] * y
    if final_norm:
        ms = jnp.mean(out * out, axis=-1, keepdims=True)
        out = out * lax.rsqrt(ms + EPS) * fn_ref[...]
    o_ref[...] = out


def _hier_moe(x2d, seq, g, sh, sc, gate, r1_w, r1_b, r2_w, r2_b, wg, wu, wd, layer, final_w):
    t, d = x2d.shape
    ne = MOE_EXPERTS
    rw = jnp.zeros((128, d), F32).at[0:MOE_GROUPS].set(r1_w.T).at[8:8 + ne].set(r2_w.T)
    rb = jnp.zeros((128, 1), F32).at[0:MOE_GROUPS, 0].set(r1_b).at[8:8 + ne, 0].set(r2_b)
    rw_hi = rw.astype(BF16)
    rw_lo = (rw - rw_hi.astype(F32)).astype(BF16)
    tmr = min(512, seq)
    bmap = lambda i: ((i * tmr) // seq, 0, 0)
    route, cnt, hp = pl.pallas_call(
        _router_kernel,
        out_shape=(jax.ShapeDtypeStruct((8, t), F32), jax.ShapeDtypeStruct((ne, 128), F32),
                   jax.ShapeDtypeStruct((t, d // 2), jnp.uint32)),
        grid=(t // tmr,),
        in_specs=[pl.BlockSpec((tmr, d), lambda i: (i, 0)),
                  pl.BlockSpec((1, d), lambda i: (0, 0)),
                  pl.BlockSpec((1, 1, d), bmap), pl.BlockSpec((1, 1, d), bmap),
                  pl.BlockSpec((128, d), lambda i: (0, 0)), pl.BlockSpec((128, d), lambda i: (0, 0)),
                  pl.BlockSpec((128, 1), lambda i: (0, 0))],
        out_specs=(pl.BlockSpec((8, tmr), lambda i: (0, i)),
                   pl.BlockSpec((ne, 128), lambda i: (0, 0)),
                   pl.BlockSpec((tmr, d // 2), lambda i: (i, 0))),
        scratch_shapes=[pltpu.VMEM((ne, 1), F32)],
        compiler_params=_cparams("arbitrary"), name="moe_router",
    )(x2d, g.reshape(1, d), sh[:, None, :], sc[:, None, :], rw_hi, rw_lo, rb)

    e1 = route[0].astype(jnp.int32)
    e2 = route[1].astype(jnp.int32)
    tile = EXPERT_TILE
    counts = cnt[:, 0].astype(jnp.int32)
    padded = ((counts + tile - 1) // tile) * tile
    ends = jnp.cumsum(padded)
    offs = ends - padded
    dest1 = offs[e1] + route[4].astype(jnp.int32)
    dest2 = offs[e2] + route[5].astype(jnp.int32)
    nt = (2 * t) // tile + ne
    n_used = ends[-1] // tile
    tid = jnp.arange(nt, dtype=jnp.int32)
    tsrc = jnp.minimum(tid, n_used - 1)
    texp = jnp.minimum(jnp.sum((tsrc[:, None] * tile >= ends[None, :]).astype(jnp.int32), axis=1), ne - 1)
    wts = jnp.concatenate([route[2:4].T, jnp.zeros((t, 6), F32)], axis=1)

    smem = pl.BlockSpec(memory_space=pltpu.SMEM)
    pk = d // 256
    slot = pl.pallas_call(
        functools.partial(_invert_kernel, n_tok=t, unroll=8),
        out_shape=jax.ShapeDtypeStruct((nt * tile,), jnp.int32),
        in_specs=[smem] * 5, out_specs=smem, name="moe_invert",
    )(dest1, dest2, counts, offs, ends)

    ff = wg.shape[3]
    n_trash = ne * tile
    ys = pl.pallas_call(
        functools.partial(_expert_kernel, n_tok=t),
        out_shape=jax.ShapeDtypeStruct((2 * t + n_trash, pk, 128), jnp.uint32),
        grid_spec=pltpu.PrefetchScalarGridSpec(
            num_scalar_prefetch=3, grid=(nt,),
            in_specs=[pl.BlockSpec(memory_space=pl.ANY),
                      pl.BlockSpec((1, 1, d, ff), lambda i, te, nv, sl: (layer, te[i], 0, 0)),
                      pl.BlockSpec((1, 1, d, ff), lambda i, te, nv, sl: (layer, te[i], 0, 0)),
                      pl.BlockSpec((1, 1, ff, d), lambda i, te, nv, sl: (layer, te[i], 0, 0))],
            out_specs=pl.BlockSpec(memory_space=pl.ANY),
            scratch_shapes=[pltpu.VMEM((tile, d // 2), jnp.uint32), pltpu.VMEM((tile, d // 2), jnp.uint32),
                            pltpu.VMEM((tile * pk, 128), jnp.uint32), pltpu.VMEM((tile * pk, 128), jnp.uint32),
                            pltpu.VMEM((d, ff), BF16), pltpu.VMEM((d, ff), BF16), pltpu.VMEM((ff, d), BF16),
                            pltpu.SemaphoreType.DMA((2,)), pltpu.SemaphoreType.DMA((2,))]),
        compiler_params=_cparams("arbitrary"), name="moe_experts",
    )(texp, n_used.reshape(1).astype(jnp.int32), slot, hp, wg, wu, wd)

    ys2 = ys.reshape(-1, 128)
    fin = final_w is not None
    fw = (final_w if fin else jnp.ones((d,), F32)).reshape(1, d)
    tmc = min(512, seq)
    nbc = t // tmc
    return pl.pallas_call(
        functools.partial(_combine_kernel, final_norm=fin),
        out_shape=jax.ShapeDtypeStruct((t, d), F32), grid=(nbc,),
        in_specs=[pl.BlockSpec((tmc, d), lambda i: (i, 0)),
                  pl.BlockSpec((tmc, 8), lambda i: (i, 0)),
                  pl.BlockSpec((1, 1, d), lambda i: ((i * tmc) // seq, 0, 0)),
                  pl.BlockSpec((1, d), lambda i: (0, 0)),
                  pl.BlockSpec((tmc * pk, 128), lambda i: (i, 0)),
                  pl.BlockSpec((tmc * pk, 128), lambda i: (i + nbc, 0))],
        out_specs=pl.BlockSpec((tmc, d), lambda i: (i, 0)),
        compiler_params=_cparams("arbitrary"), name="moe_combine",
    )(x2d, wts, gate[:, None, :], fw, ys2, ys2)


def _ssd_kernel(z_ref, x_ref, b_ref, c_ref, dt_ref, cwx_ref, cwb_ref, cwc_ref, cbx_ref, cbb_ref, cbc_ref,
                dtb_ref, alog_ref, dsk_ref, nw_ref, o_ref,
                xe_ref, be_ref, ce_ref, xs_ref, bs_ref, cs_ref, st_ref, acs_ref, dtg_ref):
    c = pl.program_id(1)
    lq = SSD_CHUNK
    ng = SSD_GROUPS
    gw = x_ref.shape[1] // ng
    hpg = gw // SSD_HEAD_DIM
    halo = xe_ref.shape[0] - lq

    @pl.when(c == 0)
    def _():
        xe_ref[0:halo] = jnp.zeros((halo, xe_ref.shape[1]), BF16)
        be_ref[0:halo] = jnp.zeros((halo, be_ref.shape[1]), BF16)
        ce_ref[0:halo] = jnp.zeros((halo, ce_ref.shape[1]), BF16)
        st_ref[...] = jnp.zeros_like(st_ref)

    xe_ref[halo:halo + lq] = x_ref[...]
    be_ref[halo:halo + lq] = b_ref[...]
    ce_ref[halo:halo + lq] = c_ref[...]

    sr = lax.broadcasted_iota(jnp.int32, (SSD_CONV * lq, halo + lq), 0)
    sm = lax.broadcasted_iota(jnp.int32, (SSD_CONV * lq, halo + lq), 1)
    tap = sr // lq
    shift_mat = jnp.where(sm == (sr - tap * lq) + halo - (SSD_CONV - 1) + tap, 1.0, 0.0).astype(BF16)

    def conv_silu(e_ref, w_ref, bias_ref, o_ref, width):
        for c0 in range(0, e_ref.shape[1], width):
            cols = slice(c0, c0 + width)
            e = e_ref[:, cols]
            acc = bias_ref[:, cols] + e[halo:].astype(F32) * w_ref[SSD_CONV - 1:SSD_CONV, cols]
            for k in range(SSD_CONV - 1):
                sh = jnp.dot(shift_mat[k * lq:(k + 1) * lq], e, preferred_element_type=F32)
                acc = acc + sh * w_ref[k:k + 1, cols]
            o_ref[:, cols] = acc + acc * jnp.tanh(acc)

    conv_silu(xe_ref, cwx_ref, cbx_ref, xs_ref, gw)
    conv_silu(be_ref, cwb_ref, cbb_ref, bs_ref, gw)
    conv_silu(ce_ref, cwc_ref, cbc_ref, cs_ref, gw)

    dt = jax.nn.softplus(dt_ref[...] + dtb_ref[...])
    adt = dt * (-jnp.exp(alog_ref[...]))
    li = lax.broadcasted_iota(jnp.int32, (lq, lq), 0)
    si = lax.broadcasted_iota(jnp.int32, (lq, lq), 1)
    tri = jnp.where(si <= li, 1.0, 0.0)
    acs = jnp.dot(tri, adt, precision=HIGHEST, preferred_element_type=F32)
    lane = lax.broadcasted_iota(jnp.int32, (lq, 128), 1)
    for gi in range(ng):
        sh = (128 - gi * hpg) % 128
        acs_ref[gi] = jnp.where(lane < hpg, pltpu.roll(acs, sh, axis=1) if sh else acs, 0.0)
        dtg_ref[gi] = jnp.where(lane < hpg, pltpu.roll(dt, sh, axis=1) if sh else dt, 0.0)
    causal = si <= li
    lane_lo = lane < SSD_HEAD_DIM

    def group_body(gi, carry):
        xc0 = pl.multiple_of(gi * gw, gw)
        nc0 = pl.multiple_of(gi * SSD_STATE, SSD_STATE)
        xg = xs_ref[:, pl.ds(xc0, gw)]
        bg = bs_ref[:, pl.ds(nc0, SSD_STATE)]
        cg = cs_ref[:, pl.ds(nc0, SSD_STATE)]
        bt = bg.T
        cgb = cg.astype(BF16)
        cb = jnp.dot(cgb, bt.astype(BF16), preferred_element_type=F32)
        acs_g = acs_ref[gi]
        acs_t = acs_g.T
        dt_t = dtg_ref[gi].T
        xgb = xg.astype(BF16)
        ys = []
        for jp in range(hpg // 2):
            lhs_parts, w_parts, ecols, dlast = [], [], [], []
            for j in (2 * jp, 2 * jp + 1):
                e_col = jnp.broadcast_to(acs_g[:, j:j + 1], (lq, lq))
                dt_row = dt_t[j:j + 1, :]
                r_row = jnp.broadcast_to(acs_t[j:j + 1, :] - jnp.log(dt_row), (lq, lq))
                lmat = jnp.where(causal, jnp.exp(e_col - r_row), 0.0) * cb
                lhs_parts.append(lmat.astype(BF16))
                a_last = acs_t[j:j + 1, lq - 1:lq]
                wrow = dt_row * jnp.exp(a_last - acs_t[j:j + 1, :])
                w_parts.append((bt * wrow).astype(BF16))
                ecols.append(e_col)
                dlast.append(jnp.exp(a_last))
            lhs = jnp.concatenate([jnp.concatenate(lhs_parts, axis=1),
                                   jnp.concatenate(w_parts, axis=1)], axis=0)
            xp = xgb[:, jp * 128:(jp + 1) * 128]
            zero = jnp.zeros_like(xp)
            rhs = jnp.concatenate([jnp.where(lane_lo, xp, zero), jnp.where(lane_lo, zero, xp)], axis=0)
            res = jnp.dot(lhs, rhs, preferred_element_type=F32)
            st = st_ref[gi, :, jp * 128:(jp + 1) * 128]
            yoff = jnp.dot(cgb, st.astype(BF16), preferred_element_type=F32)
            escale = jnp.exp(jnp.where(lane_lo, ecols[0], ecols[1]))
            ys.append(res[:lq] + yoff * escale)
            dl = jnp.where(lane_lo[:1], dlast[0], dlast[1])
            st_ref[gi, :, jp * 128:(jp + 1) * 128] = st * dl + res[lq:]
        y = jnp.concatenate(ys, axis=1) + dsk_ref[:, pl.ds(xc0, gw)] * xg
        zg = z_ref[:, pl.ds(xc0, gw)].astype(F32)
        y = y * (zg * jax.nn.sigmoid(zg))
        ms = jnp.mean(y * y, axis=-1, keepdims=True)
        y = y * lax.rsqrt(ms + EPS) * nw_ref[:, pl.ds(xc0, gw)]
        o_ref[:, pl.ds(xc0, gw)] = y.astype(o_ref.dtype)
        return carry

    lax.fori_loop(0, ng, group_body, 0)
    xe_ref[0:halo] = xe_ref[lq:lq + halo]
    be_ref[0:halo] = be_ref[lq:lq + halo]
    ce_ref[0:halo] = ce_ref[lq:lq + halo]


def _ssd_mixer(zx, dt_raw, bsz, seq, conv_w, conv_b, dt_bias, a_log, d_skip, norm_w):
    t = zx.shape[0]
    heads = a_log.shape[0]
    inner = heads * SSD_HEAD_DIM
    bcw = SSD_GROUPS * SSD_STATE
    lq = SSD_CHUNK
    nc = seq // lq
    pad = 128 - heads
    cw = 0.5 * conv_w.astype(F32)
    cwp = jnp.zeros((8, cw.shape[1]), F32).at[:SSD_CONV].set(cw)
    cb = 0.5 * conv_b.astype(F32).reshape(1, -1)
    row = lambda b, c: (b * nc + c, 0)
    const = lambda b, c: (0, 0)
    xb = inner // inner
    bb = (2 * inner) // bcw
    return pl.pallas_call(
        _ssd_kernel, out_shape=jax.ShapeDtypeStruct((t, inner), BF16), grid=(bsz, nc),
        in_specs=[pl.BlockSpec((lq, inner), lambda b, c: (b * nc + c, 0)),
                  pl.BlockSpec((lq, inner), lambda b, c: (b * nc + c, xb)),
                  pl.BlockSpec((lq, bcw), lambda b, c: (b * nc + c, bb)),
                  pl.BlockSpec((lq, bcw), lambda b, c: (b * nc + c, bb + 1)),
                  pl.BlockSpec((lq, 128), row),
                  pl.BlockSpec((8, inner), const), pl.BlockSpec((8, bcw), const), pl.BlockSpec((8, bcw), const),
                  pl.BlockSpec((1, inner), const), pl.BlockSpec((1, bcw), const), pl.BlockSpec((1, bcw), const),
                  pl.BlockSpec((1, 128), const), pl.BlockSpec((1, 128), const),
                  pl.BlockSpec((1, inner), const), pl.BlockSpec((1, inner), const)],
        out_specs=pl.BlockSpec((lq, inner), row),
        scratch_shapes=[pltpu.VMEM((lq + 16, inner), BF16), pltpu.VMEM((lq + 16, bcw), BF16),
                        pltpu.VMEM((lq + 16, bcw), BF16),
                        pltpu.VMEM((lq, inner), F32), pltpu.VMEM((lq, bcw), F32), pltpu.VMEM((lq, bcw), F32),
                        pltpu.VMEM((SSD_GROUPS, SSD_STATE, inner // SSD_GROUPS), F32),
                        pltpu.VMEM((SSD_GROUPS, lq, 128), F32), pltpu.VMEM((SSD_GROUPS, lq, 128), F32)],
        compiler_params=_cparams("arbitrary", "arbitrary"), name="ssd_mixer",
    )(zx, zx, zx, zx, dt_raw,
      cwp[:, :inner], cwp[:, inner:inner + bcw], cwp[:, inner + bcw:],
      cb[:, :inner], cb[:, inner:inner + bcw], cb[:, inner + bcw:],
      jnp.pad(dt_bias.astype(F32), (0, pad)).reshape(1, 128),
      jnp.pad(a_log.astype(F32), (0, pad)).reshape(1, 128),
      jnp.repeat(d_skip.astype(F32), SSD_HEAD_DIM).reshape(1, inner),
      norm_w.astype(F32).reshape(1, inner))


def kernel(x, c, positions, ada_w, ada_b, norm_mix, norm_ffn, mix_a_w_in, s5_lam_re, s5_lam_im, s5_log_dt, s5_b_re, s5_b_im, s5_c_re, s5_c_im, s5_d, s5_w_glu, attn_sinks, mix_a_w_out, ssd_w_in, ssd_conv_w, ssd_conv_b, ssd_dt_bias, ssd_a_log, ssd_d, ssd_norm_w, ssd_w_out, moe_r1_w, moe_r1_b, moe_r2_w, moe_r2_b, moe_w_gate, moe_w_up, moe_w_down, final_norm):
    bsz, seq, d = x.shape
    depth = ada_w.shape[0]
    t = bsz * seq
    xf = x.reshape(t, d).astype(F32)
    mod = _adaln(c.astype(F32), ada_w, ada_b)
    cos_t, sin_t = _rope_tables(positions)

    for i in range(depth):
        sh1, sc1, g1, sh2, sc2, g2 = jnp.split(mod[i], 6, axis=-1)
        j = i // 2
        if i % 2 == 0:
            s5w = s5_lam_re.shape[1] * S5_GROUP
            w_in = mix_a_w_in[j]
            u, qkv = _inproj0(xf, seq, norm_mix[i], sh1, sc1, w_in[:, :s5w], w_in[:, s5w:], tm=512)
            params = _s5_params(s5_lam_re[j], s5_lam_im[j], s5_log_dt[j], s5_b_re[j], s5_b_im[j],
                                s5_c_re[j], s5_c_im[j], s5_d[j], seq)
            y_s5 = _s5_mixer(u, bsz, seq, params)
            y_glu = _gelu_glu(y_s5, s5_w_glu[j], tm=1024, tn=512)
            wq = attn_sinks.shape[1] * HEAD_DIM
            wk = KV_HEADS * HEAD_DIM
            y_att = _attention(qkv, bsz, seq, cos_t, sin_t, attn_sinks[j],
                               q_col=0, k_col=wq // wk, v_col=wq // wk + 1)
            xf = _matmul_residual([y_glu, y_att], mix_a_w_out[j], xf, seq, g1, tm=1024, tn=1024)
        else:
            heads = ssd_a_log.shape[1]
            inner = heads * SSD_HEAD_DIM
            main = inner + inner + 2 * SSD_GROUPS * SSD_STATE
            w_in = ssd_w_in[j]
            w_dt = jnp.pad(w_in[:, main:], ((0, 0), (0, 128 - heads))).astype(BF16)
            zx, dt_raw = _inproj1(xf, seq, norm_mix[i], sh1, sc1, w_in, main, w_dt, tm=1024, tn=1024)
            y = _ssd_mixer(zx, dt_raw, bsz, seq, ssd_conv_w[j], ssd_conv_b[j], ssd_dt_bias[j],
                           ssd_a_log[j], ssd_d[j], ssd_norm_w[j])
            xf = _matmul_residual([y], ssd_w_out[j], xf, seq, g1, tm=1024, tn=512)
        xf = _hier_moe(xf, seq, norm_ffn[i], sh2, sc2, g2, moe_r1_w[i], moe_r1_b[i], moe_r2_w[i], moe_r2_b[i],
                       moe_w_gate, moe_w_up, moe_w_down, i,
                       final_norm if i == depth - 1 else None)
    return xf.reshape(bsz, seq, d).astype(x.dtype)
```

```python
import functools
import math

import jax
import jax.numpy as jnp
from jax import lax
from jax.experimental import pallas as pl
from jax.experimental.pallas import tpu as pltpu

F32 = jnp.float32
BF16 = jnp.bfloat16
HIGHEST = lax.Precision.HIGHEST

EPS = 1e-6
S5_GROUP = 16
S5_STATE = 64
S5_CHUNK = 16
HEAD_DIM = 64
KV_HEADS = 2
WINDOW = 128
ROPE_DIM = 16
ROPE_THETA = 500000.0
SSD_HEAD_DIM = 64
SSD_GROUPS = 8
SSD_STATE = 128
SSD_CONV = 4
SSD_CHUNK = 128
MOE_GROUPS = 4
MOE_EPG = 8
MOE_EXPERTS = MOE_GROUPS * MOE_EPG
EXPERT_TILE = 256
MOE_DMA_BLOCK = 32
NEG_BIG = -1e30
VMEM_LIMIT = 56 * 1024 * 1024


def _cparams(*sem):
    return pltpu.CompilerParams(dimension_semantics=sem, vmem_limit_bytes=VMEM_LIMIT)


def _modnorm(x, g, sh, sc):
    ms = jnp.mean(x * x, axis=-1, keepdims=True)
    return (x * lax.rsqrt(ms + EPS) * g) * (1.0 + sc) + sh


def _adaln_kernel(c_ref, w_ref, b_ref, o_ref):
    c = c_ref[...]
    a = c * jax.nn.sigmoid(c)
    o_ref[0] = jnp.dot(a, w_ref[0], precision=HIGHEST, preferred_element_type=F32) + b_ref[0]


def _adaln(c, ada_w, ada_b):
    depth, d, n = ada_w.shape
    bsz = c.shape[0]
    rows = 8
    cp = jnp.zeros((rows, d), F32).at[:bsz].set(c)
    tn = 1024
    out = pl.pallas_call(
        _adaln_kernel,
        out_shape=jax.ShapeDtypeStruct((depth, rows, n), F32),
        grid=(depth, n // tn),
        in_specs=[pl.BlockSpec((rows, d), lambda l, j: (0, 0)),
                  pl.BlockSpec((1, d, tn), lambda l, j: (l, 0, j)),
                  pl.BlockSpec((1, 1, tn), lambda l, j: (l, 0, j))],
        out_specs=pl.BlockSpec((1, rows, tn), lambda l, j: (l, 0, j)),
        compiler_params=_cparams("arbitrary", "arbitrary"),
        name="adaln",
    )(cp, ada_w, ada_b.reshape(depth, 1, n))
    return out[:, :bsz]


def _inproj1_kernel(x_ref, g_ref, sh_ref, sc_ref, w_ref, w2_ref, o_ref, o2_ref, h_ref):
    @pl.when(pl.program_id(1) == 0)
    def _():
        h = _modnorm(x_ref[...], g_ref[...], sh_ref[0], sc_ref[0]).astype(BF16)
        h_ref[...] = h
        o2_ref[...] = jnp.dot(h, w2_ref[...], preferred_element_type=F32)

    o_ref[...] = jnp.dot(h_ref[...], w_ref[...].astype(BF16), preferred_element_type=F32).astype(o_ref.dtype)


def _inproj1(x2d, seq, g, sh, sc, w, n_main, w2, tm, tn):
    t, d = x2d.shape
    tm = min(tm, seq)
    n2 = w2.shape[1]
    bmap = lambda i, j: ((i * tm) // seq, 0, 0)
    return pl.pallas_call(
        _inproj1_kernel,
        out_shape=(jax.ShapeDtypeStruct((t, n_main), BF16), jax.ShapeDtypeStruct((t, n2), F32)),
        grid=(t // tm, n_main // tn),
        in_specs=[pl.BlockSpec((tm, d), lambda i, j: (i, 0)),
                  pl.BlockSpec((1, d), lambda i, j: (0, 0)),
                  pl.BlockSpec((1, 1, d), bmap), pl.BlockSpec((1, 1, d), bmap),
                  pl.BlockSpec((d, tn), lambda i, j: (0, j)),
                  pl.BlockSpec((d, n2), lambda i, j: (0, 0))],
        out_specs=(pl.BlockSpec((tm, tn), lambda i, j: (i, j)), pl.BlockSpec((tm, n2), lambda i, j: (i, 0))),
        scratch_shapes=[pltpu.VMEM((tm, d), BF16)],
        compiler_params=_cparams("arbitrary", "arbitrary"), name="inproj1",
    )(x2d, g.reshape(1, d), sh[:, None, :], sc[:, None, :], w, w2)


def _mmres_kernel(*refs, n_in):
    a_refs, w_refs = refs[:n_in], refs[n_in:2 * n_in]
    x_ref, gt_ref, o_ref = refs[2 * n_in:]
    acc = jnp.dot(a_refs[0][...], w_refs[0][...].astype(BF16), preferred_element_type=F32)
    for a_ref, w_ref in zip(a_refs[1:], w_refs[1:]):
        acc = acc + jnp.dot(a_ref[...], w_ref[...].astype(BF16), preferred_element_type=F32)
    o_ref[...] = x_ref[...] + gt_ref[0] * acc


def _matmul_residual(a_list, w, x2d, seq, gate, tm, tn):
    t, n = x2d.shape
    tm = min(tm, seq)
    in_specs, row = [], 0
    for a in a_list:
        in_specs.append(pl.BlockSpec((tm, a.shape[1]), lambda i, j: (i, 0)))
    for a in a_list:
        k = a.shape[1]
        in_specs.append(pl.BlockSpec((k, tn), functools.partial(lambda i, j, rb: (rb, j), rb=row // k)))
        row += k
    in_specs.append(pl.BlockSpec((tm, tn), lambda i, j: (i, j)))
    in_specs.append(pl.BlockSpec((1, 1, tn), lambda i, j: ((i * tm) // seq, 0, j)))
    return pl.pallas_call(
        functools.partial(_mmres_kernel, n_in=len(a_list)),
        out_shape=jax.ShapeDtypeStruct((t, n), F32), grid=(t // tm, n // tn),
        in_specs=in_specs, out_specs=pl.BlockSpec((tm, tn), lambda i, j: (i, j)),
        compiler_params=_cparams("arbitrary", "arbitrary"), name="matmul_residual",
    )(*a_list, *([w] * len(a_list)), x2d, gate[:, None, :])


def _inproj0_kernel(x_ref, g_ref, sh_ref, sc_ref, wu_ref, wa_ref, u_ref, a_ref, wub_ref, wab_ref):
    @pl.when(pl.program_id(0) == 0)
    def _():
        wub_ref[...] = wu_ref[...].astype(BF16)
        wab_ref[...] = wa_ref[...].astype(BF16)

    h = _modnorm(x_ref[...], g_ref[...], sh_ref[0], sc_ref[0]).astype(BF16)
    u_ref[...] = jnp.dot(h, wub_ref[...], preferred_element_type=F32)
    a_ref[...] = jnp.dot(h, wab_ref[...], preferred_element_type=F32).astype(a_ref.dtype)


def _inproj0(x2d, seq, g, sh, sc, w_u, w_a, tm):
    t, d = x2d.shape
    tm = min(tm, seq)
    nu, na = w_u.shape[1], w_a.shape[1]
    bmap = lambda i: ((i * tm) // seq, 0, 0)
    return pl.pallas_call(
        _inproj0_kernel,
        out_shape=(jax.ShapeDtypeStruct((t, nu), F32), jax.ShapeDtypeStruct((t, na), BF16)),
        grid=(t // tm,),
        in_specs=[pl.BlockSpec((tm, d), lambda i: (i, 0)),
                  pl.BlockSpec((1, d), lambda i: (0, 0)),
                  pl.BlockSpec((1, 1, d), bmap), pl.BlockSpec((1, 1, d), bmap),
                  pl.BlockSpec((d, nu), lambda i: (0, 0), pipeline_mode=pl.Buffered(1)),
                  pl.BlockSpec((d, na), lambda i: (0, 0), pipeline_mode=pl.Buffered(1))],
        out_specs=(pl.BlockSpec((tm, nu), lambda i: (i, 0)), pl.BlockSpec((tm, na), lambda i: (i, 0))),
        scratch_shapes=[pltpu.VMEM((d, nu), BF16), pltpu.VMEM((d, na), BF16)],
        compiler_params=_cparams("arbitrary"), name="inproj0",
    )(x2d, g.reshape(1, d), sh[:, None, :], sc[:, None, :], w_u, w_a)


S5_SLAB = 8
S5_SEGS = 8


def _s5_params(lam_re, lam_im, log_dt, b_re, b_im, c_re, c_im, d_skip, seq):
    g, n = lam_re.shape
    p = S5_GROUP
    lc = S5_CHUNK
    gs = S5_SLAB
    nj = g // gs
    lam = lax.complex(lam_re.astype(F32), lam_im.astype(F32))
    dt = jnp.exp(log_dt.astype(F32))[:, None]
    ld = lam * dt
    lam_bar = jnp.exp(ld)
    bmat = lax.complex(b_re.astype(F32), b_im.astype(F32))
    b_bar = ((lam_bar - 1.0) / lam)[..., None] * bmat
    cmat = lax.complex(c_re.astype(F32), c_im.astype(F32))
    ks = jnp.arange(lc, dtype=F32)
    pw = jnp.exp(ld[:, None, :] * ks[None, :, None])
    pw1 = jnp.exp(ld[:, None, :] * (ks[None, :, None] + 1.0))
    kk = jnp.real(jnp.einsum('gpn,gkn,gnq->gkpq', cmat, pw, b_bar, precision=HIGHEST))
    eg = jnp.eye(gs, dtype=F32)
    ex_p = jnp.einsum('ab,pr->apbr', eg, jnp.eye(p, dtype=F32)).reshape(gs, p, gs * p)
    ex_n = jnp.einsum('ab,nr->anbr', eg, jnp.eye(n, dtype=F32)).reshape(gs, n, gs * n)
    kq_ = kk.transpose(0, 1, 3, 2).reshape(nj, gs, lc, p, p)
    kexp = jnp.einsum('jakqp,apc->jkaqc', kq_, ex_p, precision=HIGHEST).reshape(nj, lc, gs * p, gs * p)
    kexp = kexp.at[:, 0].add(jnp.eye(gs * p, dtype=F32)[None] * d_skip.astype(F32).reshape(nj, 1, gs * p))
    kexp = kexp.astype(BF16)
    wb = (pw[:, ::-1, :, None] * b_bar[:, None, :, :]).transpose(0, 1, 3, 2)
    wb = wb.reshape(nj, gs, lc, p, n)
    wb_slab = jnp.concatenate(
        [jnp.einsum('jasqn,anc->jsaqc', part.astype(BF16), ex_n.astype(BF16)).reshape(nj, lc * gs * p, gs * n)
         for part in (jnp.real(wb), jnp.imag(wb))], axis=-1).astype(BF16)
    cl = (cmat[:, None, :, :] * pw1[:, :, None, :]).transpose(0, 3, 1, 2)
    cl = cl.reshape(nj, gs, n, lc, p)
    woff_slab = jnp.concatenate(
        [jnp.einsum('janlp,apc->janlc', part.astype(BF16), ex_p.astype(BF16)).reshape(nj, gs * n, lc * gs * p)
         for part in (jnp.real(cl), -jnp.imag(cl))], axis=1).astype(BF16)
    seglen = (seq // lc) // S5_SEGS
    kq = jnp.arange(seglen + 1, dtype=F32) * float(lc)
    pq = jnp.exp(ld[:, None, :] * kq[None, :, None]).reshape(nj, gs, seglen + 1, n)
    pq = pq.transpose(0, 2, 1, 3).reshape(nj, seglen + 1, gs * n)
    return kexp, wb_slab, woff_slab, jnp.real(pq), jnp.imag(pq)


def _s5_kernel(u_ref, k_ref, wb_ref, woff_ref, pre_ref, pim_ref, y_ref, sloc_ref, sin_ref, m_ref):
    lc = S5_CHUNK
    w = u_ref.shape[1]

    @pl.when((pl.program_id(0) == 0) & (pl.program_id(1) == 0))
    def _():
        m_ref[...] = jnp.zeros_like(m_ref)

    @pl.when(pl.program_id(1) == 0)
    def _():
        for s in range(lc):
            for l in range(s, lc):
                m_ref[s * w:(s + 1) * w, l * w:(l + 1) * w] = k_ref[0, l - s]

    nc = u_ref.shape[0] // lc
    hw = wb_ref.shape[2] // 2
    seglen = nc // S5_SEGS
    u_all = jnp.concatenate([u_ref[pl.ds(s, nc, stride=lc), :] for s in range(lc)], axis=1).astype(BF16)
    nq = hw // w
    sloc = jnp.dot(u_all, wb_ref[0], preferred_element_type=F32)
    for q in range(2 * nq):
        sloc_ref[q] = sloc[:, q * w:(q + 1) * w]
    a_re, a_im = pre_ref[0, 1:2], pim_ref[0, 1:2]

    def put(rows, re, im):
        for q in range(nq):
            sin_ref[q, rows, :] = re[:, q * w:(q + 1) * w]
            sin_ref[nq + q, rows, :] = im[:, q * w:(q + 1) * w]

    def get(ref, rows):
        return (jnp.concatenate([ref[q, rows, :] for q in range(nq)], axis=1),
                jnp.concatenate([ref[nq + q, rows, :] for q in range(nq)], axis=1))

    def seg_step(k, st):
        sre, sim = st
        rows = pl.ds(k, S5_SEGS, stride=seglen)
        put(rows, sre, sim)
        lre, lim = get(sloc_ref, rows)
        return a_re * sre - a_im * sim + lre, a_re * sim + a_im * sre + lim

    z = jnp.zeros((S5_SEGS, hw), F32)
    ere, eim = lax.fori_loop(0, seglen, seg_step, (z, z))
    g_re, g_im = pre_ref[0, seglen:seglen + 1], pim_ref[0, seglen:seglen + 1]
    cre, cim = jnp.zeros((1, hw), F32), jnp.zeros((1, hw), F32)
    pre, pim = pre_ref[0, 0:seglen], pim_ref[0, 0:seglen]
    for sg in range(1, S5_SEGS):
        cre, cim = (g_re * cre - g_im * cim + ere[sg - 1:sg], g_re * cim + g_im * cre + eim[sg - 1:sg])
        rows = slice(sg * seglen, (sg + 1) * seglen)
        cur_re, cur_im = get(sin_ref, rows)
        put(rows, cur_re + pre * cre - pim * cim, cur_im + pre * cim + pim * cre)
    s_in = jnp.concatenate([sin_ref[q] for q in range(2 * nq)], axis=1).astype(BF16)
    for lb in range(lc // 2):
        c0, c1 = 2 * lb * w, (2 * lb + 2) * w
        y2 = (jnp.dot(u_all[:, :c1], m_ref[:c1, c0:c1], preferred_element_type=F32)
              + jnp.dot(s_in, woff_ref[0, :, c0:c1], preferred_element_type=F32))
        y_ref[pl.ds(2 * lb, nc, stride=lc), :] = y2[:, :w]
        y_ref[pl.ds(2 * lb + 1, nc, stride=lc), :] = y2[:, w:]


def _s5_mixer(u, bsz, seq, params):
    kexp, wb_slab, woff_slab, pq_re, pq_im = params
    nj, km, sw = wb_slab.shape
    w = km // S5_CHUNK
    nc = seq // S5_CHUNK
    t = u.shape[0]
    const = lambda j, b: (j, 0, 0)
    return pl.pallas_call(
        _s5_kernel, out_shape=jax.ShapeDtypeStruct((t, nj * w), F32), grid=(nj, bsz),
        in_specs=[pl.BlockSpec((seq, w), lambda j, b: (b, j)),
                  pl.BlockSpec((1, S5_CHUNK, w, w), lambda j, b: (j, 0, 0, 0)),
                  pl.BlockSpec((1, km, sw), const),
                  pl.BlockSpec((1, sw, km), const),
                  pl.BlockSpec((1,) + pq_re.shape[1:], const),
                  pl.BlockSpec((1,) + pq_im.shape[1:], const)],
        out_specs=pl.BlockSpec((seq, w), lambda j, b: (b, j)),
        scratch_shapes=[pltpu.VMEM((sw // w, nc, w), F32), pltpu.VMEM((sw // w, nc, w), F32),
                        pltpu.VMEM((km, km), BF16)],
        compiler_params=_cparams("arbitrary", "arbitrary"), name="s5_mixer",
    )(u, kexp, wb_slab, woff_slab, pq_re, pq_im)


def _glu_kernel(y_ref, wv_ref, wg_ref, o_ref, a_ref):
    @pl.when(pl.program_id(1) == 0)
    def _():
        a_ref[...] = jax.nn.gelu(y_ref[...].astype(F32), approximate=True).astype(BF16)

    a = a_ref[...]
    val = jnp.dot(a, wv_ref[...].astype(BF16), preferred_element_type=F32)
    gate = jnp.dot(a, wg_ref[...].astype(BF16), preferred_element_type=F32)
    o_ref[...] = (val * jax.nn.sigmoid(gate)).astype(o_ref.dtype)


def _gelu_glu(y, w, tm, tn):
    t, k = y.shape
    n = w.shape[1] // 2
    tm = min(tm, t)
    nb = n // tn
    return pl.pallas_call(
        _glu_kernel, out_shape=jax.ShapeDtypeStruct((t, n), BF16), grid=(t // tm, n // tn),
        in_specs=[pl.BlockSpec((tm, k), lambda i, j: (i, 0)),
                  pl.BlockSpec((k, tn), lambda i, j: (0, j)),
                  pl.BlockSpec((k, tn), lambda i, j: (0, j + nb))],
        out_specs=pl.BlockSpec((tm, tn), lambda i, j: (i, j)),
        scratch_shapes=[pltpu.VMEM((tm, k), BF16)],
        compiler_params=_cparams("arbitrary", "arbitrary"), name="gelu_glu",
    )(y, w, w)


def _rope(x, cos, sin):
    w = x.shape[1]
    half = ROPE_DIM // 2
    d = lax.broadcasted_iota(jnp.int32, x.shape, 1) % HEAD_DIM
    partner = jnp.where(d < half, pltpu.roll(x, w - half, axis=1), pltpu.roll(x, half, axis=1))
    return x * cos + partner * sin


def _attn_kernel(sink_ref, q_ref, kc_ref, kp_ref, vc_ref, vp_ref,
                 cq_ref, sq_ref, cp_ref, sp_ref, o_ref):
    n = pl.program_id(1)
    wq = q_ref.shape[1]
    hq = wq // HEAD_DIM
    grp = hq // KV_HEADS
    cos_c, sin_c = cq_ref[...], sq_ref[...]
    reps = wq // cos_c.shape[1]
    q = _rope(q_ref[...].astype(F32),
              jnp.concatenate([cos_c] * reps, axis=1), jnp.concatenate([sin_c] * reps, axis=1))
    q = (q * (HEAD_DIM ** -0.5)).astype(BF16)
    kc = _rope(kc_ref[...].astype(F32), cos_c, sin_c).astype(BF16)
    kp = _rope(kp_ref[...].astype(F32), cp_ref[...], sp_ref[...]).astype(BF16)
    kk = jnp.concatenate([kp, kc], axis=0)
    vv = jnp.concatenate([vp_ref[...], vc_ref[...]], axis=0)
    qi = lax.broadcasted_iota(jnp.int32, (WINDOW, 2 * WINDOW), 0)
    kj = lax.broadcasted_iota(jnp.int32, (WINDOW, 2 * WINDOW), 1)
    first = jnp.where(n > 0, 0, WINDOW)
    valid = (kj > qi) & (kj <= qi + WINDOW) & (kj >= first)
    outs = []
    for h in range(hq):
        kvh = h // grp
        qh = q[:, h * HEAD_DIM:(h + 1) * HEAD_DIM]
        kh = kk[:, kvh * HEAD_DIM:(kvh + 1) * HEAD_DIM]
        vh = vv[:, kvh * HEAD_DIM:(kvh + 1) * HEAD_DIM]
        s = lax.dot_general(qh, kh, (((1,), (1,)), ((), ())), preferred_element_type=F32)
        s = jnp.where(valid, s, NEG_BIG)
        sink = sink_ref[h]
        mx = jnp.maximum(jnp.max(s, axis=-1, keepdims=True), sink)
        p = jnp.exp(s - mx)
        den = jnp.sum(p, axis=-1, keepdims=True) + jnp.exp(sink - mx)
        o = jnp.dot(p.astype(BF16), vh, preferred_element_type=F32)
        outs.append(o / den)
    o_ref[...] = jnp.concatenate(outs, axis=1).astype(o_ref.dtype)


def _attention(proj, bsz, seq, cos_t, sin_t, sinks, q_col, k_col, v_col):
    t = proj.shape[0]
    nb = seq // WINDOW
    wq = sinks.shape[0] * HEAD_DIM
    wk = KV_HEADS * HEAD_DIM
    cur = lambda b, n, s: (b * nb + n, 0)
    prv = lambda b, n, s: (b * nb + jnp.maximum(n - 1, 0), 0)
    return pl.pallas_call(
        _attn_kernel, out_shape=jax.ShapeDtypeStruct((t, wq), BF16),
        grid_spec=pltpu.PrefetchScalarGridSpec(
            num_scalar_prefetch=1, grid=(bsz, nb),
            in_specs=[pl.BlockSpec((WINDOW, wq), lambda b, n, s: (b * nb + n, q_col)),
                      pl.BlockSpec((WINDOW, wk), lambda b, n, s: (b * nb + n, k_col)),
                      pl.BlockSpec((WINDOW, wk), lambda b, n, s: (b * nb + jnp.maximum(n - 1, 0), k_col)),
                      pl.BlockSpec((WINDOW, wk), lambda b, n, s: (b * nb + n, v_col)),
                      pl.BlockSpec((WINDOW, wk), lambda b, n, s: (b * nb + jnp.maximum(n - 1, 0), v_col)),
                      pl.BlockSpec((WINDOW, wk), cur), pl.BlockSpec((WINDOW, wk), cur),
                      pl.BlockSpec((WINDOW, wk), prv), pl.BlockSpec((WINDOW, wk), prv)],
            out_specs=pl.BlockSpec((WINDOW, wq), lambda b, n, s: (b * nb + n, 0))),
        compiler_params=_cparams("arbitrary", "arbitrary"), name="swa_attention",
    )(sinks.astype(F32), proj, proj, proj, proj, proj, cos_t, sin_t, cos_t, sin_t)


def _rope_tables(positions):
    half = ROPE_DIM // 2
    inv = 1.0 / (ROPE_THETA ** (jnp.arange(0, ROPE_DIM, 2, dtype=F32) / ROPE_DIM))
    ang = positions.astype(F32).reshape(-1, 1) * inv[None, :]
    cos, sin = jnp.cos(ang), jnp.sin(ang)
    t = ang.shape[0]
    ones = jnp.ones((t, HEAD_DIM - ROPE_DIM), F32)
    cos_h = jnp.concatenate([cos, cos, ones], axis=1)
    sin_h = jnp.concatenate([-sin, sin, 0.0 * ones], axis=1)
    return jnp.tile(cos_h, (1, KV_HEADS)), jnp.tile(sin_h, (1, KV_HEADS))


def _pack_bf16_pairs(v):
    half = v.shape[1] // 2
    hi = lax.bitcast_convert_type(v[:, :half].astype(BF16).astype(F32), jnp.uint32)
    lo = lax.bitcast_convert_type(v[:, half:].astype(BF16).astype(F32), jnp.uint32)
    return hi | (lo >> 16)


def _unpack_bf16_pairs(p):
    return jnp.concatenate([lax.bitcast_convert_type(p & jnp.uint32(0xFFFF0000), F32),
                            lax.bitcast_convert_type(p << 16, F32)], axis=1)


def _router_kernel(x_ref, g_ref, sh_ref, sc_ref, rwh_ref, rwl_ref, rb_ref, o_ref, cnt_ref, hp_ref, carry_ref):
    i = pl.program_id(0)
    tm = x_ref.shape[0]
    ne = MOE_EXPERTS

    @pl.when(i == 0)
    def _():
        carry_ref[...] = jnp.zeros_like(carry_ref)

    h = _modnorm(x_ref[...], g_ref[...], sh_ref[0], sc_ref[0])
    hp_ref[...] = _pack_bf16_pairs(h)
    nt_dims = (((1,), (1,)), ((), ()))
    h_hi = h.astype(BF16)
    h_lo = (h - h_hi.astype(F32)).astype(BF16)
    lt = (lax.dot_general(rwh_ref[...], h_hi, nt_dims, preferred_element_type=F32)
          + lax.dot_general(rwh_ref[...], h_lo, nt_dims, preferred_element_type=F32)
          + lax.dot_general(rwl_ref[...], h_hi, nt_dims, preferred_element_type=F32)) + rb_ref[...]
    gl = lt[0:MOE_GROUPS]
    gm = jnp.max(gl, axis=0, keepdims=True)
    gsum = jnp.sum(jnp.exp(gl - gm), axis=0, keepdims=True)
    g_val = 1.0 / gsum
    grow = lax.broadcasted_iota(jnp.int32, gl.shape, 0)
    g_idx = jnp.min(jnp.where(gl == gm, grow, MOE_GROUPS), axis=0, keepdims=True)
    sel = jnp.zeros((MOE_EPG, tm), F32)
    for gi in range(MOE_GROUPS):
        sel = jnp.where(g_idx == gi, lt[8 + gi * MOE_EPG: 8 + (gi + 1) * MOE_EPG], sel)
    erow = lax.broadcasted_iota(jnp.int32, sel.shape, 0)
    m1 = jnp.max(sel, axis=0, keepdims=True)
    i1 = jnp.min(jnp.where(sel == m1, erow, MOE_EPG), axis=0, keepdims=True)
    sel2 = jnp.where(erow == i1, -jnp.inf, sel)
    m2 = jnp.max(sel2, axis=0, keepdims=True)
    i2 = jnp.min(jnp.where(sel2 == m2, erow, MOE_EPG), axis=0, keepdims=True)
    e21 = jnp.exp(m2 - m1)
    w1 = g_val / (1.0 + e21)
    w2 = g_val * e21 / (1.0 + e21)
    e1 = g_idx * MOE_EPG + i1
    e2 = g_idx * MOE_EPG + i2
    xrow = lax.broadcasted_iota(jnp.int32, (ne, tm), 0)
    oh1 = xrow == e1
    oh2 = xrow == e2
    cmat = jnp.where(oh1 | oh2, 1.0, 0.0)
    ti = lax.broadcasted_iota(jnp.int32, (tm, tm), 0)
    tj = lax.broadcasted_iota(jnp.int32, (tm, tm), 1)
    su = jnp.where(ti < tj, 1.0, 0.0).astype(BF16)
    before = jnp.dot(cmat.astype(BF16), su, preferred_element_type=F32) + carry_ref[...]
    r1 = jnp.sum(jnp.where(oh1, before, 0.0), axis=0, keepdims=True)
    r2 = jnp.sum(jnp.where(oh2, before, 0.0), axis=0, keepdims=True)
    carry = carry_ref[...] + jnp.sum(cmat, axis=1, keepdims=True)
    carry_ref[...] = carry
    cnt_ref[...] = jnp.broadcast_to(carry, cnt_ref.shape)
    o_ref[...] = jnp.concatenate([e1.astype(F32), e2.astype(F32), w1, w2, r1, r2,
                                  jnp.zeros((2, tm), F32)], axis=0)


def _invert_kernel(d1_ref, d2_ref, cnt_ref, off_ref, end_ref, slot_ref, *, n_tok, unroll):
    ne = cnt_ref.shape[0]

    def pad_expert(e, p):
        lo = off_ref[e] + cnt_ref[e]

        def pad_row(r, p):
            slot_ref[r] = 2 * n_tok + p
            return p + 1

        return lax.fori_loop(lo, end_ref[e], pad_row, p)

    lax.fori_loop(0, ne, pad_expert, 0)

    def tok_block(i, c):
        for j in range(unroll):
            t = i * unroll + j
            slot_ref[d1_ref[t]] = t
            slot_ref[d2_ref[t]] = n_tok + t
        return c

    lax.fori_loop(0, n_tok // unroll, tok_block, 0)


def _expert_kernel(te_ref, nv_ref, nb_ref, slot_ref, hp_ref, wg_ref, wu_ref, wd_ref, ys_ref,
                   xb0_ref, xb1_ref, yb0_ref, yb1_ref, wgb_ref, wub_ref, wdb_ref, gsem, ssem, *, n_tok):
    i = pl.program_id(0)
    nv = nv_ref[0]
    tile = xb0_ref.shape[0]
    pk = ys_ref.shape[1]
    blk = MOE_DMA_BLOCK

    def for_blocks(tile_idx, fn):
        nb = nb_ref[tile_idx]
        for b in range(tile // blk):
            @pl.when(b < nb)
            def _():
                fn(b)

    def gather(tile_idx, xb_ref, sem):
        def issue(b):
            for r in range(b * blk, (b + 1) * blk):
                s = slot_ref[tile_idx * tile + r]
                tok = jnp.where(s >= 2 * n_tok, 0, jnp.where(s >= n_tok, s - n_tok, s))
                pltpu.make_async_copy(hp_ref.at[pl.ds(tok, 1)], xb_ref.at[pl.ds(r, 1)], sem).start(priority=1)
        for_blocks(tile_idx, issue)

    def wait_gather(tile_idx, xb_ref, sem):
        part = xb_ref.at[pl.ds(0, blk)]
        for_blocks(tile_idx, lambda b: pltpu.make_async_copy(part, part, sem).wait())

    def scatter(tile_idx, yb_ref, sem):
        def issue(b):
            for r in range(b * blk, (b + 1) * blk):
                s = slot_ref[tile_idx * tile + r]
                pltpu.make_async_copy(yb_ref.at[pl.ds(r * pk, pk)], ys_ref.at[s], sem).start(priority=r % 2)
        for_blocks(tile_idx, issue)

    def wait_scatter(tile_idx, yb_ref, sem):
        part = yb_ref.at[pl.ds(0, blk * pk)]
        for_blocks(tile_idx, lambda b: pltpu.make_async_copy(part, part, sem).wait())

    new_expert = (i == 0) | (te_ref[i] != te_ref[jnp.maximum(i - 1, 0)])

    @pl.when(new_expert & (i < nv))
    def _():
        wgb_ref[...] = wg_ref[0, 0].astype(BF16)
        wub_ref[...] = wu_ref[0, 0].astype(BF16)
        wdb_ref[...] = wd_ref[0, 0].astype(BF16)

    @pl.when(i == 0)
    def _():
        xb0_ref[...] = jnp.zeros_like(xb0_ref)
        xb1_ref[...] = jnp.zeros_like(xb1_ref)
        gather(0, xb0_ref, gsem.at[0])

    def step(p, xb_ref, xb_next_ref, yb_ref, yb_other_ref):
        nxt = jnp.minimum(i + 1, nv - 1)
        gather(nxt, xb_next_ref, gsem.at[1 - p])
        wait_gather(i, xb_ref, gsem.at[p])

        @pl.when(i >= 2)
        def _():
            wait_scatter(i - 2, yb_ref, ssem.at[p])

        x = _unpack_bf16_pairs(xb_ref[...]).astype(BF16)
        hg = jnp.dot(x, wgb_ref[...], preferred_element_type=F32)
        hu = jnp.dot(x, wub_ref[...], preferred_element_type=F32)
        hid = (hg * jax.nn.sigmoid(hg) * hu).astype(BF16)
        y = jnp.dot(hid, wdb_ref[...], preferred_element_type=F32)
        yp = _pack_bf16_pairs(y)
        lanes = yb_ref.shape[1]
        for j in range(pk):
            yb_ref[pl.ds(j, tile, stride=pk), :] = yp[:, j * lanes:(j + 1) * lanes]
        scatter(i, yb_ref, ssem.at[p])

        @pl.when(i == nv - 1)
        def _():
            wait_gather(nxt, xb_next_ref, gsem.at[1 - p])
            wait_scatter(i, yb_ref, ssem.at[p])

            @pl.when(i >= 1)
            def _():
                wait_scatter(i - 1, yb_other_ref, ssem.at[1 - p])

    odd = lax.rem(i, 2) == 1

    @pl.when((i < nv) & jnp.logical_not(odd))
    def _():
        step(0, xb0_ref, xb1_ref, yb0_ref, yb1_ref)

    @pl.when((i < nv) & odd)
    def _():
        step(1, xb1_ref, xb0_ref, yb1_ref, yb0_ref)


def _combine_kernel(x_ref, w_ref, gt_ref, fn_ref, ya_ref, yb_ref, o_ref, *, final_norm):
    w = w_ref[...]
    tm = x_ref.shape[0]
    pk = ya_ref.shape[0] // tm

    def rows(ref):
        return _unpack_bf16_pairs(jnp.concatenate([ref[pl.ds(j, tm, stride=pk), :] for j in range(pk)], axis=1))

    y = w[:, 0:1] * rows(ya_ref) + w[:, 1:2] * rows(yb_ref)
    out = x_ref[...] + gt_ref[0] * y
    if final_norm:
        ms = jnp.mean(out * out, axis=-1, keepdims=True)
        out = out * lax.rsqrt(ms + EPS) * fn_ref[...]
    o_ref[...] = out


def _hier_moe(x2d, seq, g, sh, sc, gate, r1_w, r1_b, r2_w, r2_b, wg, wu, wd, layer, final_w):
    t, d = x2d.shape
    ne = MOE_EXPERTS
    rw = jnp.zeros((128, d), F32).at[0:MOE_GROUPS].set(r1_w.T).at[8:8 + ne].set(r2_w.T)
    rb = jnp.zeros((128, 1), F32).at[0:MOE_GROUPS, 0].set(r1_b).at[8:8 + ne, 0].set(r2_b)
    rw_hi = rw.astype(BF16)
    rw_lo = (rw - rw_hi.astype(F32)).astype(BF16)
    tmr = min(512, seq)
    bmap = lambda i: ((i * tmr) // seq, 0, 0)
    route, cnt, hp = pl.pallas_call(
        _router_kernel,
        out_shape=(jax.ShapeDtypeStruct((8, t), F32), jax.ShapeDtypeStruct((ne, 128), F32),
                   jax.ShapeDtypeStruct((t, d // 2), jnp.uint32)),
        grid=(t // tmr,),
        in_specs=[pl.BlockSpec((tmr, d), lambda i: (i, 0)),
                  pl.BlockSpec((1, d), lambda i: (0, 0)),
                  pl.BlockSpec((1, 1, d), bmap), pl.BlockSpec((1, 1, d), bmap),
                  pl.BlockSpec((128, d), lambda i: (0, 0)), pl.BlockSpec((128, d), lambda i: (0, 0)),
                  pl.BlockSpec((128, 1), lambda i: (0, 0))],
        out_specs=(pl.BlockSpec((8, tmr), lambda i: (0, i)),
                   pl.BlockSpec((ne, 128), lambda i: (0, 0)),
                   pl.BlockSpec((tmr, d // 2), lambda i: (i, 0))),
        scratch_shapes=[pltpu.VMEM((ne, 1), F32)],
        compiler_params=_cparams("arbitrary"), name="moe_router",
    )(x2d, g.reshape(1, d), sh[:, None, :], sc[:, None, :], rw_hi, rw_lo, rb)

    e1 = route[0].astype(jnp.int32)
    e2 = route[1].astype(jnp.int32)
    tile = EXPERT_TILE
    counts = cnt[:, 0].astype(jnp.int32)
    padded = ((counts + tile - 1) // tile) * tile
    ends = jnp.cumsum(padded)
    offs = ends - padded
    dest1 = offs[e1] + route[4].astype(jnp.int32)
    dest2 = offs[e2] + route[5].astype(jnp.int32)
    nt = (2 * t) // tile + ne
    n_used = ends[-1] // tile
    tid = jnp.arange(nt, dtype=jnp.int32)
    tsrc = jnp.minimum(tid, n_used - 1)
    texp = jnp.minimum(jnp.sum((tsrc[:, None] * tile >= ends[None, :]).astype(jnp.int32), axis=1), ne - 1)
    wts = jnp.concatenate([route[2:4].T, jnp.zeros((t, 6), F32)], axis=1)
    tvalid = jnp.clip(offs[texp] + counts[texp] - tsrc * tile, 0, tile)
    tblocks = (tvalid + MOE_DMA_BLOCK - 1) // MOE_DMA_BLOCK

    smem = pl.BlockSpec(memory_space=pltpu.SMEM)
    pk = d // 256
    slot = pl.pallas_call(
        functools.partial(_invert_kernel, n_tok=t, unroll=8),
        out_shape=jax.ShapeDtypeStruct((nt * tile,), jnp.int32),
        in_specs=[smem] * 5, out_specs=smem, name="moe_invert",
    )(dest1, dest2, counts, offs, ends)

    ff = wg.shape[3]
    n_trash = ne * tile
    ys = pl.pallas_call(
        functools.partial(_expert_kernel, n_tok=t),
        out_shape=jax.ShapeDtypeStruct((2 * t + n_trash, pk, 128), jnp.uint32),
        grid_spec=pltpu.PrefetchScalarGridSpec(
            num_scalar_prefetch=4, grid=(nt,),
            in_specs=[pl.BlockSpec(memory_space=pl.ANY),
                      pl.BlockSpec((1, 1, d, ff), lambda i, te, nv, nb, sl: (layer, te[i], 0, 0)),
                      pl.BlockSpec((1, 1, d, ff), lambda i, te, nv, nb, sl: (layer, te[i], 0, 0)),
                      pl.BlockSpec((1, 1, ff, d), lambda i, te, nv, nb, sl: (layer, te[i], 0, 0))],
            out_specs=pl.BlockSpec(memory_space=pl.ANY),
            scratch_shapes=[pltpu.VMEM((tile, d // 2), jnp.uint32), pltpu.VMEM((tile, d // 2), jnp.uint32),
                            pltpu.VMEM((tile * pk, 128), jnp.uint32), pltpu.VMEM((tile * pk, 128), jnp.uint32),
                            pltpu.VMEM((d, ff), BF16), pltpu.VMEM((d, ff), BF16), pltpu.VMEM((ff, d), BF16),
                            pltpu.SemaphoreType.DMA((2,)), pltpu.SemaphoreType.DMA((2,))]),
        compiler_params=_cparams("arbitrary"), name="moe_experts",
    )(texp, n_used.reshape(1).astype(jnp.int32), tblocks.astype(jnp.int32), slot, hp, wg, wu, wd)

    ys2 = ys.reshape(-1, 128)
    fin = final_w is not None
    fw = (final_w if fin else jnp.ones((d,), F32)).reshape(1, d)
    tmc = min(512, seq)
    nbc = t // tmc
    return pl.pallas_call(
        functools.partial(_combine_kernel, final_norm=fin),
        out_shape=jax.ShapeDtypeStruct((t, d), F32), grid=(nbc,),
        in_specs=[pl.BlockSpec((tmc, d), lambda i: (i, 0)),
                  pl.BlockSpec((tmc, 8), lambda i: (i, 0)),
                  pl.BlockSpec((1, 1, d), lambda i: ((i * tmc) // seq, 0, 0)),
                  pl.BlockSpec((1, d), lambda i: (0, 0)),
                  pl.BlockSpec((tmc * pk, 128), lambda i: (i, 0)),
                  pl.BlockSpec((tmc * pk, 128), lambda i: (i + nbc, 0))],
        out_specs=pl.BlockSpec((tmc, d), lambda i: (i, 0)),
        compiler_params=_cparams("arbitrary"), name="moe_combine",
    )(x2d, wts, gate[:, None, :], fw, ys2, ys2)


def _ssd_kernel(z_ref, x_ref, b_ref, c_ref, dt_ref, cwx_ref, cwb_ref, cwc_ref, cbx_ref, cbb_ref, cbc_ref,
                dtb_ref, alog_ref, dsk_ref, nw_ref, o_ref,
                xe_ref, be_ref, ce_ref, xs_ref, bs_ref, cs_ref, st_ref, acs_ref, dtg_ref):
    c = pl.program_id(1)
    lq = SSD_CHUNK
    ng = SSD_GROUPS
    gw = x_ref.shape[1] // ng
    hpg = gw // SSD_HEAD_DIM
    halo = xe_ref.shape[0] - lq

    @pl.when(c == 0)
    def _():
        xe_ref[0:halo] = jnp.zeros((halo, xe_ref.shape[1]), BF16)
        be_ref[0:halo] = jnp.zeros((halo, be_ref.shape[1]), BF16)
        ce_ref[0:halo] = jnp.zeros((halo, ce_ref.shape[1]), BF16)
        st_ref[...] = jnp.zeros_like(st_ref)

    xe_ref[halo:halo + lq] = x_ref[...]
    be_ref[halo:halo + lq] = b_ref[...]
    ce_ref[halo:halo + lq] = c_ref[...]

    sr = lax.broadcasted_iota(jnp.int32, (SSD_CONV * lq, halo + lq), 0)
    sm = lax.broadcasted_iota(jnp.int32, (SSD_CONV * lq, halo + lq), 1)
    tap = sr // lq
    shift_mat = jnp.where(sm == (sr - tap * lq) + halo - (SSD_CONV - 1) + tap, 1.0, 0.0).astype(BF16)

    def conv_silu(e_ref, w_ref, bias_ref, o_ref, width):
        for c0 in range(0, e_ref.shape[1], width):
            cols = slice(c0, c0 + width)
            e = e_ref[:, cols]
            acc = bias_ref[:, cols] + e[halo:].astype(F32) * w_ref[SSD_CONV - 1:SSD_CONV, cols]
            for k in range(SSD_CONV - 1):
                sh = jnp.dot(shift_mat[k * lq:(k + 1) * lq], e, preferred_element_type=F32)
                acc = acc + sh * w_ref[k:k + 1, cols]
            o_ref[:, cols] = acc + acc * jnp.tanh(acc)

    conv_silu(xe_ref, cwx_ref, cbx_ref, xs_ref, gw)
    conv_silu(be_ref, cwb_ref, cbb_ref, bs_ref, gw)
    conv_silu(ce_ref, cwc_ref, cbc_ref, cs_ref, gw)

    dt = jax.nn.softplus(dt_ref[...] + dtb_ref[...])
    adt = dt * (-jnp.exp(alog_ref[...]))
    li = lax.broadcasted_iota(jnp.int32, (lq, lq), 0)
    si = lax.broadcasted_iota(jnp.int32, (lq, lq), 1)
    tri = jnp.where(si <= li, 1.0, 0.0)
    acs = jnp.dot(tri, adt, precision=HIGHEST, preferred_element_type=F32)
    lane = lax.broadcasted_iota(jnp.int32, (lq, 128), 1)
    for gi in range(ng):
        sh = (128 - gi * hpg) % 128
        acs_ref[gi] = jnp.where(lane < hpg, pltpu.roll(acs, sh, axis=1) if sh else acs, 0.0)
        dtg_ref[gi] = jnp.where(lane < hpg, pltpu.roll(dt, sh, axis=1) if sh else dt, 0.0)
    causal = si <= li
    lane_lo = lane < SSD_HEAD_DIM

    def group_body(gi, carry):
        xc0 = pl.multiple_of(gi * gw, gw)
        nc0 = pl.multiple_of(gi * SSD_STATE, SSD_STATE)
        xg = xs_ref[:, pl.ds(xc0, gw)]
        bg = bs_ref[:, pl.ds(nc0, SSD_STATE)]
        cg = cs_ref[:, pl.ds(nc0, SSD_STATE)]
        bt = bg.T
        cgb = cg.astype(BF16)
        cb = jnp.dot(cgb, bt.astype(BF16), preferred_element_type=F32)
        acs_g = acs_ref[gi]
        acs_t = acs_g.T
        dt_t = dtg_ref[gi].T
        xgb = xg.astype(BF16)
        ys = []
        for jp in range(hpg // 2):
            lhs_parts, w_parts, ecols, dlast = [], [], [], []
            for j in (2 * jp, 2 * jp + 1):
                e_col = jnp.broadcast_to(acs_g[:, j:j + 1], (lq, lq))
                dt_row = dt_t[j:j + 1, :]
                r_row = jnp.broadcast_to(acs_t[j:j + 1, :] - jnp.log(dt_row), (lq, lq))
                lmat = jnp.where(causal, jnp.exp(e_col - r_row), 0.0) * cb
                lhs_parts.append(lmat.astype(BF16))
                a_last = acs_t[j:j + 1, lq - 1:lq]
                wrow = dt_row * jnp.exp(a_last - acs_t[j:j + 1, :])
                w_parts.append((bt * wrow).astype(BF16))
                ecols.append(e_col)
                dlast.append(jnp.exp(a_last))
            lhs = jnp.concatenate([jnp.concatenate(lhs_parts, axis=1),
                                   jnp.concatenate(w_parts, axis=1)], axis=0)
            xp = xgb[:, jp * 128:(jp + 1) * 128]
            zero = jnp.zeros_like(xp)
            rhs = jnp.concatenate([jnp.where(lane_lo, xp, zero), jnp.where(lane_lo, zero, xp)], axis=0)
            res = jnp.dot(lhs, rhs, preferred_element_type=F32)
            st = st_ref[gi, :, jp * 128:(jp + 1) * 128]
            yoff = jnp.dot(cgb, st.astype(BF16), preferred_element_type=F32)
            escale = jnp.exp(jnp.where(lane_lo, ecols[0], ecols[1]))
            ys.append(res[:lq] + yoff * escale)
            dl = jnp.where(lane_lo[:1], dlast[0], dlast[1])
            st_ref[gi, :, jp * 128:(jp + 1) * 128] = st * dl + res[lq:]
        y = jnp.concatenate(ys, axis=1) + dsk_ref[:, pl.ds(xc0, gw)] * xg
        zg = z_ref[:, pl.ds(xc0, gw)].astype(F32)
        y = y * (zg * jax.nn.sigmoid(zg))
        ms = jnp.mean(y * y, axis=-1, keepdims=True)
        y = y * lax.rsqrt(ms + EPS) * nw_ref[:, pl.ds(xc0, gw)]
        o_ref[:, pl.ds(xc0, gw)] = y.astype(o_ref.dtype)
        return carry

    lax.fori_loop(0, ng, group_body, 0)
    xe_ref[0:halo] = xe_ref[lq:lq + halo]
    be_ref[0:halo] = be_ref[lq:lq + halo]
    ce_ref[0:halo] = ce_ref[lq:lq + halo]


def _ssd_mixer(zx, dt_raw, bsz, seq, conv_w, conv_b, dt_bias, a_log, d_skip, norm_w):
    t = zx.shape[0]
    heads = a_log.shape[0]
    inner = heads * SSD_HEAD_DIM
    bcw = SSD_GROUPS * SSD_STATE
    lq = SSD_CHUNK
    nc = seq // lq
    pad = 128 - heads
    cw = 0.5 * conv_w.astype(F32)
    cwp = jnp.zeros((8, cw.shape[1]), F32).at[:SSD_CONV].set(cw)
    cb = 0.5 * conv_b.astype(F32).reshape(1, -1)
    row = lambda b, c: (b * nc + c, 0)
    const = lambda b, c: (0, 0)
    xb = inner // inner
    bb = (2 * inner) // bcw
    return pl.pallas_call(
        _ssd_kernel, out_shape=jax.ShapeDtypeStruct((t, inner), BF16), grid=(bsz, nc),
        in_specs=[pl.BlockSpec((lq, inner), lambda b, c: (b * nc + c, 0)),
                  pl.BlockSpec((lq, inner), lambda b, c: (b * nc + c, xb)),
                  pl.BlockSpec((lq, bcw), lambda b, c: (b * nc + c, bb)),
                  pl.BlockSpec((lq, bcw), lambda b, c: (b * nc + c, bb + 1)),
                  pl.BlockSpec((lq, 128), row),
                  pl.BlockSpec((8, inner), const), pl.BlockSpec((8, bcw), const), pl.BlockSpec((8, bcw), const),
                  pl.BlockSpec((1, inner), const), pl.BlockSpec((1, bcw), const), pl.BlockSpec((1, bcw), const),
                  pl.BlockSpec((1, 128), const), pl.BlockSpec((1, 128), const),
                  pl.BlockSpec((1, inner), const), pl.BlockSpec((1, inner), const)],
        out_specs=pl.BlockSpec((lq, inner), row),
        scratch_shapes=[pltpu.VMEM((lq + 16, inner), BF16), pltpu.VMEM((lq + 16, bcw), BF16),
                        pltpu.VMEM((lq + 16, bcw), BF16),
                        pltpu.VMEM((lq, inner), F32), pltpu.VMEM((lq, bcw), F32), pltpu.VMEM((lq, bcw), F32),
                        pltpu.VMEM((SSD_GROUPS, SSD_STATE, inner // SSD_GROUPS), F32),
                        pltpu.VMEM((SSD_GROUPS, lq, 128), F32), pltpu.VMEM((SSD_GROUPS, lq, 128), F32)],
        compiler_params=_cparams("arbitrary", "arbitrary"), name="ssd_mixer",
    )(zx, zx, zx, zx, dt_raw,
      cwp[:, :inner], cwp[:, inner:inner + bcw], cwp[:, inner + bcw:],
      cb[:, :inner], cb[:, inner:inner + bcw], cb[:, inner + bcw:],
      jnp.pad(dt_bias.astype(F32), (0, pad)).reshape(1, 128),
      jnp.pad(a_log.astype(F32), (0, pad)).reshape(1, 128),
      jnp.repeat(d_skip.astype(F32), SSD_HEAD_DIM).reshape(1, inner),
      norm_w.astype(F32).reshape(1, inner))


def kernel(x, c, positions, ada_w, ada_b, norm_mix, norm_ffn, mix_a_w_in, s5_lam_re, s5_lam_im, s5_log_dt, s5_b_re, s5_b_im, s5_c_re, s5_c_im, s5_d, s5_w_glu, attn_sinks, mix_a_w_out, ssd_w_in, ssd_conv_w, ssd_conv_b, ssd_dt_bias, ssd_a_log, ssd_d, ssd_norm_w, ssd_w_out, moe_r1_w, moe_r1_b, moe_r2_w, moe_r2_b, moe_w_gate, moe_w_up, moe_w_down, final_norm):
    bsz, seq, d = x.shape
    depth = ada_w.shape[0]
    t = bsz * seq
    xf = x.reshape(t, d).astype(F32)
    mod = _adaln(c.astype(F32), ada_w, ada_b)
    cos_t, sin_t = _rope_tables(positions)

    for i in range(depth):
        sh1, sc1, g1, sh2, sc2, g2 = jnp.split(mod[i], 6, axis=-1)
        j = i // 2
        if i % 2 == 0:
            s5w = s5_lam_re.shape[1] * S5_GROUP
            w_in = mix_a_w_in[j]
            u, qkv = _inproj0(xf, seq, norm_mix[i], sh1, sc1, w_in[:, :s5w], w_in[:, s5w:], tm=512)
            params = _s5_params(s5_lam_re[j], s5_lam_im[j], s5_log_dt[j], s5_b_re[j], s5_b_im[j],
                                s5_c_re[j], s5_c_im[j], s5_d[j], seq)
            y_s5 = _s5_mixer(u, bsz, seq, params)
            y_glu = _gelu_glu(y_s5, s5_w_glu[j], tm=1024, tn=512)
            wq = attn_sinks.shape[1] * HEAD_DIM
            wk = KV_HEADS * HEAD_DIM
            y_att = _attention(qkv, bsz, seq, cos_t, sin_t, attn_sinks[j],
                               q_col=0, k_col=wq // wk, v_col=wq // wk + 1)
            xf = _matmul_residual([y_glu, y_att], mix_a_w_out[j], xf, seq, g1, tm=1024, tn=1024)
        else:
            heads = ssd_a_log.shape[1]
            inner = heads * SSD_HEAD_DIM
            main = inner + inner + 2 * SSD_GROUPS * SSD_STATE
            w_in = ssd_w_in[j]
            w_dt = jnp.pad(w_in[:, main:], ((0, 0), (0, 128 - heads))).astype(BF16)
            zx, dt_raw = _inproj1(xf, seq, norm_mix[i], sh1, sc1, w_in, main, w_dt, tm=1024, tn=1024)
            y = _ssd_mixer(zx, dt_raw, bsz, seq, ssd_conv_w[j], ssd_conv_b[j], ssd_dt_bias[j],
                           ssd_a_log[j], ssd_d[j], ssd_norm_w[j])
            xf = _matmul_residual([y], ssd_w_out[j], xf, seq, g1, tm=1024, tn=512)
        xf = _hier_moe(xf, seq, norm_ffn[i], sh2, sc2, g2, moe_r1_w[i], moe_r1_b[i], moe_r2_w[i], moe_r2_b[i],
                       moe_w_gate, moe_w_up, moe_w_down, i,
                       final_norm if i == depth - 1 else None)
    return xf.reshape(bsz, seq, d).astype(x.dtype)
```

```python
import functools
import math

import jax
import jax.numpy as jnp
from jax import lax
from jax.experimental import pallas as pl
from jax.experimental.pallas import tpu as pltpu

F32 = jnp.float32
BF16 = jnp.bfloat16
HIGHEST = lax.Precision.HIGHEST

EPS = 1e-6
S5_GROUP = 16
S5_STATE = 64
S5_CHUNK = 16
HEAD_DIM = 64
KV_HEADS = 2
WINDOW = 128
ROPE_DIM = 16
ROPE_THETA = 500000.0
SSD_HEAD_DIM = 64
SSD_GROUPS = 8
SSD_STATE = 128
SSD_CONV = 4
SSD_CHUNK = 128
MOE_GROUPS = 4
MOE_EPG = 8
MOE_EXPERTS = MOE_GROUPS * MOE_EPG
EXPERT_TILE = 256
MOE_DMA_BLOCK = 32
NEG_BIG = -1e30
VMEM_LIMIT = 56 * 1024 * 1024


def _cparams(*sem):
    return pltpu.CompilerParams(dimension_semantics=sem, vmem_limit_bytes=VMEM_LIMIT)


def _modnorm(x, g, sh, sc):
    ms = jnp.mean(x * x, axis=-1, keepdims=True)
    return (x * lax.rsqrt(ms + EPS) * g) * (1.0 + sc) + sh


def _adaln_kernel(c_ref, w_ref, b_ref, o_ref):
    c = c_ref[...]
    a = c * jax.nn.sigmoid(c)
    o_ref[0] = jnp.dot(a, w_ref[0], precision=HIGHEST, preferred_element_type=F32) + b_ref[0]


def _adaln(c, ada_w, ada_b):
    depth, d, n = ada_w.shape
    bsz = c.shape[0]
    rows = 8
    cp = jnp.zeros((rows, d), F32).at[:bsz].set(c)
    tn = 2048
    out = pl.pallas_call(
        _adaln_kernel,
        out_shape=jax.ShapeDtypeStruct((depth, rows, n), F32),
        grid=(depth, n // tn),
        in_specs=[pl.BlockSpec((rows, d), lambda l, j: (0, 0)),
                  pl.BlockSpec((1, d, tn), lambda l, j: (l, 0, j)),
                  pl.BlockSpec((1, 1, tn), lambda l, j: (l, 0, j))],
        out_specs=pl.BlockSpec((1, rows, tn), lambda l, j: (l, 0, j)),
        compiler_params=_cparams("arbitrary", "arbitrary"),
        name="adaln",
    )(cp, ada_w, ada_b.reshape(depth, 1, n))
    return out[:, :bsz]


def _inproj1_kernel(x_ref, g_ref, sh_ref, sc_ref, w_ref, w2_ref, o_ref, o2_ref, h_ref):
    @pl.when(pl.program_id(1) == 0)
    def _():
        h = _modnorm(x_ref[...], g_ref[...], sh_ref[0], sc_ref[0]).astype(BF16)
        h_ref[...] = h
        o2_ref[...] = jnp.dot(h, w2_ref[...], preferred_element_type=F32)

    o_ref[...] = jnp.dot(h_ref[...], w_ref[...].astype(BF16), preferred_element_type=F32).astype(o_ref.dtype)


def _inproj1(x2d, seq, g, sh, sc, w, n_main, w2, tm, tn):
    t, d = x2d.shape
    tm = min(tm, seq)
    n2 = w2.shape[1]
    bmap = lambda i, j: ((i * tm) // seq, 0, 0)
    return pl.pallas_call(
        _inproj1_kernel,
        out_shape=(jax.ShapeDtypeStruct((t, n_main), BF16), jax.ShapeDtypeStruct((t, n2), F32)),
        grid=(t // tm, n_main // tn),
        in_specs=[pl.BlockSpec((tm, d), lambda i, j: (i, 0)),
                  pl.BlockSpec((1, d), lambda i, j: (0, 0)),
                  pl.BlockSpec((1, 1, d), bmap), pl.BlockSpec((1, 1, d), bmap),
                  pl.BlockSpec((d, tn), lambda i, j: (0, j)),
                  pl.BlockSpec((d, n2), lambda i, j: (0, 0))],
        out_specs=(pl.BlockSpec((tm, tn), lambda i, j: (i, j)), pl.BlockSpec((tm, n2), lambda i, j: (i, 0))),
        scratch_shapes=[pltpu.VMEM((tm, d), BF16)],
        compiler_params=_cparams("arbitrary", "arbitrary"), name="inproj1",
    )(x2d, g.reshape(1, d), sh[:, None, :], sc[:, None, :], w, w2)


def _mmres_kernel(*refs, n_in):
    a_refs, w_refs = refs[:n_in], refs[n_in:2 * n_in]
    x_ref, gt_ref, o_ref = refs[2 * n_in:]
    acc = jnp.dot(a_refs[0][...], w_refs[0][...].astype(BF16), preferred_element_type=F32)
    for a_ref, w_ref in zip(a_refs[1:], w_refs[1:]):
        acc = acc + jnp.dot(a_ref[...], w_ref[...].astype(BF16), preferred_element_type=F32)
    o_ref[...] = x_ref[...] + gt_ref[0] * acc


def _matmul_residual(a_list, w, x2d, seq, gate, tm, tn):
    t, n = x2d.shape
    tm = min(tm, seq)
    in_specs, row = [], 0
    for a in a_list:
        in_specs.append(pl.BlockSpec((tm, a.shape[1]), lambda i, j: (i, 0)))
    for a in a_list:
        k = a.shape[1]
        in_specs.append(pl.BlockSpec((k, tn), functools.partial(lambda i, j, rb: (rb, j), rb=row // k)))
        row += k
    in_specs.append(pl.BlockSpec((tm, tn), lambda i, j: (i, j)))
    in_specs.append(pl.BlockSpec((1, 1, tn), lambda i, j: ((i * tm) // seq, 0, j)))
    return pl.pallas_call(
        functools.partial(_mmres_kernel, n_in=len(a_list)),
        out_shape=jax.ShapeDtypeStruct((t, n), F32), grid=(t // tm, n // tn),
        in_specs=in_specs, out_specs=pl.BlockSpec((tm, tn), lambda i, j: (i, j)),
        compiler_params=_cparams("arbitrary", "arbitrary"), name="matmul_residual",
    )(*a_list, *([w] * len(a_list)), x2d, gate[:, None, :])


def _inproj0_kernel(x_ref, g_ref, sh_ref, sc_ref, wu_ref, wa_ref, u_ref, a_ref, wub_ref, wab_ref):
    @pl.when(pl.program_id(0) == 0)
    def _():
        wub_ref[...] = wu_ref[...].astype(BF16)
        wab_ref[...] = wa_ref[...].astype(BF16)

    h = _modnorm(x_ref[...], g_ref[...], sh_ref[0], sc_ref[0]).astype(BF16)
    u_ref[...] = jnp.dot(h, wub_ref[...], preferred_element_type=F32)
    a_ref[...] = jnp.dot(h, wab_ref[...], preferred_element_type=F32).astype(a_ref.dtype)


def _inproj0(x2d, seq, g, sh, sc, w_u, w_a, tm):
    t, d = x2d.shape
    tm = min(tm, seq)
    nu, na = w_u.shape[1], w_a.shape[1]
    bmap = lambda i: ((i * tm) // seq, 0, 0)
    return pl.pallas_call(
        _inproj0_kernel,
        out_shape=(jax.ShapeDtypeStruct((t, nu), F32), jax.ShapeDtypeStruct((t, na), BF16)),
        grid=(t // tm,),
        in_specs=[pl.BlockSpec((tm, d), lambda i: (i, 0)),
                  pl.BlockSpec((1, d), lambda i: (0, 0)),
                  pl.BlockSpec((1, 1, d), bmap), pl.BlockSpec((1, 1, d), bmap),
                  pl.BlockSpec((d, nu), lambda i: (0, 0), pipeline_mode=pl.Buffered(1)),
                  pl.BlockSpec((d, na), lambda i: (0, 0), pipeline_mode=pl.Buffered(1))],
        out_specs=(pl.BlockSpec((tm, nu), lambda i: (i, 0)), pl.BlockSpec((tm, na), lambda i: (i, 0))),
        scratch_shapes=[pltpu.VMEM((d, nu), BF16), pltpu.VMEM((d, na), BF16)],
        compiler_params=_cparams("arbitrary"), name="inproj0",
    )(x2d, g.reshape(1, d), sh[:, None, :], sc[:, None, :], w_u, w_a)


S5_SLAB = 8
S5_SEGS = 8


def _s5_params(lam_re, lam_im, log_dt, b_re, b_im, c_re, c_im, d_skip, seq):
    g, n = lam_re.shape
    p = S5_GROUP
    lc = S5_CHUNK
    gs = S5_SLAB
    nj = g // gs
    lam = lax.complex(lam_re.astype(F32), lam_im.astype(F32))
    dt = jnp.exp(log_dt.astype(F32))[:, None]
    ld = lam * dt
    lam_bar = jnp.exp(ld)
    bmat = lax.complex(b_re.astype(F32), b_im.astype(F32))
    b_bar = ((lam_bar - 1.0) / lam)[..., None] * bmat
    cmat = lax.complex(c_re.astype(F32), c_im.astype(F32))
    ks = jnp.arange(lc, dtype=F32)
    pw = jnp.exp(ld[:, None, :] * ks[None, :, None])
    pw1 = jnp.exp(ld[:, None, :] * (ks[None, :, None] + 1.0))
    kk = jnp.real(jnp.einsum('gpn,gkn,gnq->gkpq', cmat, pw, b_bar, precision=HIGHEST))
    eg = jnp.eye(gs, dtype=F32)
    ex_p = jnp.einsum('ab,pr->apbr', eg, jnp.eye(p, dtype=F32)).reshape(gs, p, gs * p)
    ex_n = jnp.einsum('ab,nr->anbr', eg, jnp.eye(n, dtype=F32)).reshape(gs, n, gs * n)
    kq_ = kk.transpose(0, 1, 3, 2).reshape(nj, gs, lc, p, p)
    kexp = jnp.einsum('jakqp,apc->jkaqc', kq_, ex_p, precision=HIGHEST).reshape(nj, lc, gs * p, gs * p)
    kexp = kexp.at[:, 0].add(jnp.eye(gs * p, dtype=F32)[None] * d_skip.astype(F32).reshape(nj, 1, gs * p))
    kexp = kexp.astype(BF16)
    wb = (pw[:, ::-1, :, None] * b_bar[:, None, :, :]).transpose(0, 1, 3, 2)
    wb = wb.reshape(nj, gs, lc, p, n)
    wb_slab = jnp.concatenate(
        [jnp.einsum('jasqn,anc->jsaqc', part.astype(BF16), ex_n.astype(BF16)).reshape(nj, lc * gs * p, gs * n)
         for part in (jnp.real(wb), jnp.imag(wb))], axis=-1).astype(BF16)
    cl = (cmat[:, None, :, :] * pw1[:, :, None, :]).transpose(0, 3, 1, 2)
    cl = cl.reshape(nj, gs, n, lc, p)
    woff_slab = jnp.concatenate(
        [jnp.einsum('janlp,apc->janlc', part.astype(BF16), ex_p.astype(BF16)).reshape(nj, gs * n, lc * gs * p)
         for part in (jnp.real(cl), -jnp.imag(cl))], axis=1).astype(BF16)
    seglen = (seq // lc) // S5_SEGS
    kq = jnp.arange(seglen + 1, dtype=F32) * float(lc)
    pq = jnp.exp(ld[:, None, :] * kq[None, :, None]).reshape(nj, gs, seglen + 1, n)
    pq = pq.transpose(0, 2, 1, 3).reshape(nj, seglen + 1, gs * n)
    return kexp, wb_slab, woff_slab, jnp.real(pq), jnp.imag(pq)


def _s5_kernel(u_ref, k_ref, wb_ref, woff_ref, pre_ref, pim_ref, y_ref, sloc_ref, sin_ref, m_ref):
    lc = S5_CHUNK
    w = u_ref.shape[1]

    @pl.when((pl.program_id(0) == 0) & (pl.program_id(1) == 0))
    def _():
        m_ref[...] = jnp.zeros_like(m_ref)

    @pl.when(pl.program_id(1) == 0)
    def _():
        for s in range(lc):
            for l in range(s, lc):
                m_ref[s * w:(s + 1) * w, l * w:(l + 1) * w] = k_ref[0, l - s]

    nc = u_ref.shape[0] // lc
    hw = wb_ref.shape[2] // 2
    seglen = nc // S5_SEGS
    u_all = jnp.concatenate([u_ref[pl.ds(s, nc, stride=lc), :] for s in range(lc)], axis=1).astype(BF16)
    nq = hw // w
    sloc = jnp.dot(u_all, wb_ref[0], preferred_element_type=F32)
    for q in range(2 * nq):
        sloc_ref[q] = sloc[:, q * w:(q + 1) * w]
    a_re, a_im = pre_ref[0, 1:2], pim_ref[0, 1:2]

    def put(rows, re, im):
        for q in range(nq):
            sin_ref[q, rows, :] = re[:, q * w:(q + 1) * w]
            sin_ref[nq + q, rows, :] = im[:, q * w:(q + 1) * w]

    def get(ref, rows):
        return (jnp.concatenate([ref[q, rows, :] for q in range(nq)], axis=1),
                jnp.concatenate([ref[nq + q, rows, :] for q in range(nq)], axis=1))

    def seg_step(k, st):
        sre, sim = st
        rows = pl.ds(k, S5_SEGS, stride=seglen)
        put(rows, sre, sim)
        lre, lim = get(sloc_ref, rows)
        return a_re * sre - a_im * sim + lre, a_re * sim + a_im * sre + lim

    z = jnp.zeros((S5_SEGS, hw), F32)
    ere, eim = lax.fori_loop(0, seglen, seg_step, (z, z))
    g_re, g_im = pre_ref[0, seglen:seglen + 1], pim_ref[0, seglen:seglen + 1]
    cre, cim = jnp.zeros((1, hw), F32), jnp.zeros((1, hw), F32)
    pre, pim = pre_ref[0, 0:seglen], pim_ref[0, 0:seglen]
    for sg in range(1, S5_SEGS):
        cre, cim = (g_re * cre - g_im * cim + ere[sg - 1:sg], g_re * cim + g_im * cre + eim[sg - 1:sg])
        rows = slice(sg * seglen, (sg + 1) * seglen)
        cur_re, cur_im = get(sin_ref, rows)
        put(rows, cur_re + pre * cre - pim * cim, cur_im + pre * cim + pim * cre)
    s_in = jnp.concatenate([sin_ref[q] for q in range(2 * nq)], axis=1).astype(BF16)
    for lb in range(lc // 2):
        c0, c1 = 2 * lb * w, (2 * lb + 2) * w
        y2 = (jnp.dot(u_all[:, :c1], m_ref[:c1, c0:c1], preferred_element_type=F32)
              + jnp.dot(s_in, woff_ref[0, :, c0:c1], preferred_element_type=F32))
        y_ref[pl.ds(2 * lb, nc, stride=lc), :] = y2[:, :w]
        y_ref[pl.ds(2 * lb + 1, nc, stride=lc), :] = y2[:, w:]


def _s5_mixer(u, bsz, seq, params):
    kexp, wb_slab, woff_slab, pq_re, pq_im = params
    nj, km, sw = wb_slab.shape
    w = km // S5_CHUNK
    nc = seq // S5_CHUNK
    t = u.shape[0]
    const = lambda j, b: (j, 0, 0)
    return pl.pallas_call(
        _s5_kernel, out_shape=jax.ShapeDtypeStruct((t, nj * w), F32), grid=(nj, bsz),
        in_specs=[pl.BlockSpec((seq, w), lambda j, b: (b, j)),
                  pl.BlockSpec((1, S5_CHUNK, w, w), lambda j, b: (j, 0, 0, 0)),
                  pl.BlockSpec((1, km, sw), const),
                  pl.BlockSpec((1, sw, km), const),
                  pl.BlockSpec((1,) + pq_re.shape[1:], const),
                  pl.BlockSpec((1,) + pq_im.shape[1:], const)],
        out_specs=pl.BlockSpec((seq, w), lambda j, b: (b, j)),
        scratch_shapes=[pltpu.VMEM((sw // w, nc, w), F32), pltpu.VMEM((sw // w, nc, w), F32),
                        pltpu.VMEM((km, km), BF16)],
        compiler_params=_cparams("arbitrary", "arbitrary"), name="s5_mixer",
    )(u, kexp, wb_slab, woff_slab, pq_re, pq_im)


def _glu_kernel(y_ref, wv_ref, wg_ref, o_ref, a_ref):
    @pl.when(pl.program_id(1) == 0)
    def _():
        a_ref[...] = jax.nn.gelu(y_ref[...].astype(F32), approximate=True).astype(BF16)

    a = a_ref[...]
    val = jnp.dot(a, wv_ref[...].astype(BF16), preferred_element_type=F32)
    gate = jnp.dot(a, wg_ref[...].astype(BF16), preferred_element_type=F32)
    o_ref[...] = (val * jax.nn.sigmoid(gate)).astype(o_ref.dtype)


def _gelu_glu(y, w, tm, tn):
    t, k = y.shape
    n = w.shape[1] // 2
    tm = min(tm, t)
    nb = n // tn
    return pl.pallas_call(
        _glu_kernel, out_shape=jax.ShapeDtypeStruct((t, n), BF16), grid=(t // tm, n // tn),
        in_specs=[pl.BlockSpec((tm, k), lambda i, j: (i, 0)),
                  pl.BlockSpec((k, tn), lambda i, j: (0, j)),
                  pl.BlockSpec((k, tn), lambda i, j: (0, j + nb))],
        out_specs=pl.BlockSpec((tm, tn), lambda i, j: (i, j)),
        scratch_shapes=[pltpu.VMEM((tm, k), BF16)],
        compiler_params=_cparams("arbitrary", "arbitrary"), name="gelu_glu",
    )(y, w, w)


def _rope(x, cos, sin):
    w = x.shape[1]
    half = ROPE_DIM // 2
    d = lax.broadcasted_iota(jnp.int32, x.shape, 1) % HEAD_DIM
    partner = jnp.where(d < half, pltpu.roll(x, w - half, axis=1), pltpu.roll(x, half, axis=1))
    return x * cos + partner * sin


def _attn_kernel(sink_ref, q_ref, kc_ref, kp_ref, vc_ref, vp_ref,
                 cq_ref, sq_ref, cp_ref, sp_ref, o_ref):
    n = pl.program_id(1)
    wq = q_ref.shape[1]
    hq = wq // HEAD_DIM
    grp = hq // KV_HEADS
    cos_c, sin_c = cq_ref[...], sq_ref[...]
    reps = wq // cos_c.shape[1]
    q = _rope(q_ref[...].astype(F32),
              jnp.concatenate([cos_c] * reps, axis=1), jnp.concatenate([sin_c] * reps, axis=1))
    q = (q * (HEAD_DIM ** -0.5)).astype(BF16)
    kc = _rope(kc_ref[...].astype(F32), cos_c, sin_c).astype(BF16)
    kp = _rope(kp_ref[...].astype(F32), cp_ref[...], sp_ref[...]).astype(BF16)
    kk = jnp.concatenate([kp, kc], axis=0)
    vv = jnp.concatenate([vp_ref[...], vc_ref[...]], axis=0)
    qi = lax.broadcasted_iota(jnp.int32, (WINDOW, 2 * WINDOW), 0)
    kj = lax.broadcasted_iota(jnp.int32, (WINDOW, 2 * WINDOW), 1)
    first = jnp.where(n > 0, 0, WINDOW)
    valid = (kj > qi) & (kj <= qi + WINDOW) & (kj >= first)
    outs = []
    for h in range(hq):
        kvh = h // grp
        qh = q[:, h * HEAD_DIM:(h + 1) * HEAD_DIM]
        kh = kk[:, kvh * HEAD_DIM:(kvh + 1) * HEAD_DIM]
        vh = vv[:, kvh * HEAD_DIM:(kvh + 1) * HEAD_DIM]
        s = lax.dot_general(qh, kh, (((1,), (1,)), ((), ())), preferred_element_type=F32)
        s = jnp.where(valid, s, NEG_BIG)
        sink = sink_ref[h]
        mx = jnp.maximum(jnp.max(s, axis=-1, keepdims=True), sink)
        p = jnp.exp(s - mx)
        den = jnp.sum(p, axis=-1, keepdims=True) + jnp.exp(sink - mx)
        o = jnp.dot(p.astype(BF16), vh, preferred_element_type=F32)
        outs.append(o / den)
    o_ref[...] = jnp.concatenate(outs, axis=1).astype(o_ref.dtype)


def _attention(proj, bsz, seq, cos_t, sin_t, sinks, q_col, k_col, v_col):
    t = proj.shape[0]
    nb = seq // WINDOW
    wq = sinks.shape[0] * HEAD_DIM
    wk = KV_HEADS * HEAD_DIM
    cur = lambda b, n, s: (b * nb + n, 0)
    prv = lambda b, n, s: (b * nb + jnp.maximum(n - 1, 0), 0)
    return pl.pallas_call(
        _attn_kernel, out_shape=jax.ShapeDtypeStruct((t, wq), BF16),
        grid_spec=pltpu.PrefetchScalarGridSpec(
            num_scalar_prefetch=1, grid=(bsz, nb),
            in_specs=[pl.BlockSpec((WINDOW, wq), lambda b, n, s: (b * nb + n, q_col)),
                      pl.BlockSpec((WINDOW, wk), lambda b, n, s: (b * nb + n, k_col)),
                      pl.BlockSpec((WINDOW, wk), lambda b, n, s: (b * nb + jnp.maximum(n - 1, 0), k_col)),
                      pl.BlockSpec((WINDOW, wk), lambda b, n, s: (b * nb + n, v_col)),
                      pl.BlockSpec((WINDOW, wk), lambda b, n, s: (b * nb + jnp.maximum(n - 1, 0), v_col)),
                      pl.BlockSpec((WINDOW, wk), cur), pl.BlockSpec((WINDOW, wk), cur),
                      pl.BlockSpec((WINDOW, wk), prv), pl.BlockSpec((WINDOW, wk), prv)],
            out_specs=pl.BlockSpec((WINDOW, wq), lambda b, n, s: (b * nb + n, 0))),
        compiler_params=_cparams("arbitrary", "arbitrary"), name="swa_attention",
    )(sinks.astype(F32), proj, proj, proj, proj, proj, cos_t, sin_t, cos_t, sin_t)


def _rope_tables(positions):
    half = ROPE_DIM // 2
    inv = 1.0 / (ROPE_THETA ** (jnp.arange(0, ROPE_DIM, 2, dtype=F32) / ROPE_DIM))
    ang = positions.astype(F32).reshape(-1, 1) * inv[None, :]
    cos, sin = jnp.cos(ang), jnp.sin(ang)
    t = ang.shape[0]
    ones = jnp.ones((t, HEAD_DIM - ROPE_DIM), F32)
    cos_h = jnp.concatenate([cos, cos, ones], axis=1)
    sin_h = jnp.concatenate([-sin, sin, 0.0 * ones], axis=1)
    return jnp.tile(cos_h, (1, KV_HEADS)), jnp.tile(sin_h, (1, KV_HEADS))


def _pack_bf16_pairs(v):
    half = v.shape[1] // 2
    hi = lax.bitcast_convert_type(v[:, :half].astype(BF16).astype(F32), jnp.uint32)
    lo = lax.bitcast_convert_type(v[:, half:].astype(BF16).astype(F32), jnp.uint32)
    return hi | (lo >> 16)


def _unpack_bf16_pairs(p):
    return jnp.concatenate([lax.bitcast_convert_type(p & jnp.uint32(0xFFFF0000), F32),
                            lax.bitcast_convert_type(p << 16, F32)], axis=1)


def _router_kernel(x_ref, g_ref, sh_ref, sc_ref, rwh_ref, rwl_ref, rb_ref, o_ref, cnt_ref, hp_ref, carry_ref):
    i = pl.program_id(0)
    tm = x_ref.shape[0]
    ne = MOE_EXPERTS

    @pl.when(i == 0)
    def _():
        carry_ref[...] = jnp.zeros_like(carry_ref)

    h = _modnorm(x_ref[...], g_ref[...], sh_ref[0], sc_ref[0])
    hp_ref[...] = _pack_bf16_pairs(h)
    nt_dims = (((1,), (1,)), ((), ()))
    h_hi = h.astype(BF16)
    h_lo = (h - h_hi.astype(F32)).astype(BF16)
    lt = (lax.dot_general(rwh_ref[...], h_hi, nt_dims, preferred_element_type=F32)
          + lax.dot_general(rwh_ref[...], h_lo, nt_dims, preferred_element_type=F32)
          + lax.dot_general(rwl_ref[...], h_hi, nt_dims, preferred_element_type=F32)) + rb_ref[...]
    gl = lt[0:MOE_GROUPS]
    gm = jnp.max(gl, axis=0, keepdims=True)
    gsum = jnp.sum(jnp.exp(gl - gm), axis=0, keepdims=True)
    g_val = 1.0 / gsum
    grow = lax.broadcasted_iota(jnp.int32, gl.shape, 0)
    g_idx = jnp.min(jnp.where(gl == gm, grow, MOE_GROUPS), axis=0, keepdims=True)
    sel = jnp.zeros((MOE_EPG, tm), F32)
    for gi in range(MOE_GROUPS):
        sel = jnp.where(g_idx == gi, lt[8 + gi * MOE_EPG: 8 + (gi + 1) * MOE_EPG], sel)
    erow = lax.broadcasted_iota(jnp.int32, sel.shape, 0)
    m1 = jnp.max(sel, axis=0, keepdims=True)
    i1 = jnp.min(jnp.where(sel == m1, erow, MOE_EPG), axis=0, keepdims=True)
    sel2 = jnp.where(erow == i1, -jnp.inf, sel)
    m2 = jnp.max(sel2, axis=0, keepdims=True)
    i2 = jnp.min(jnp.where(sel2 == m2, erow, MOE_EPG), axis=0, keepdims=True)
    e21 = jnp.exp(m2 - m1)
    w1 = g_val / (1.0 + e21)
    w2 = g_val * e21 / (1.0 + e21)
    e1 = g_idx * MOE_EPG + i1
    e2 = g_idx * MOE_EPG + i2
    xrow = lax.broadcasted_iota(jnp.int32, (ne, tm), 0)
    oh1 = xrow == e1
    oh2 = xrow == e2
    cmat = jnp.where(oh1 | oh2, 1.0, 0.0)
    ti = lax.broadcasted_iota(jnp.int32, (tm, tm), 0)
    tj = lax.broadcasted_iota(jnp.int32, (tm, tm), 1)
    su = jnp.where(ti < tj, 1.0, 0.0).astype(BF16)
    before = jnp.dot(cmat.astype(BF16), su, preferred_element_type=F32) + carry_ref[...]
    r1 = jnp.sum(jnp.where(oh1, before, 0.0), axis=0, keepdims=True)
    r2 = jnp.sum(jnp.where(oh2, before, 0.0), axis=0, keepdims=True)
    carry = carry_ref[...] + jnp.sum(cmat, axis=1, keepdims=True)
    carry_ref[...] = carry
    cnt_ref[...] = jnp.broadcast_to(carry, cnt_ref.shape)
    o_ref[...] = jnp.concatenate([e1.astype(F32), e2.astype(F32), w1, w2, r1, r2,
                                  jnp.zeros((2, tm), F32)], axis=0)


def _invert_kernel(d1_ref, d2_ref, cnt_ref, off_ref, end_ref, slot_ref, *, n_tok, unroll):
    ne = cnt_ref.shape[0]

    def pad_expert(e, p):
        lo = off_ref[e] + cnt_ref[e]

        def pad_row(r, p):
            slot_ref[r] = 2 * n_tok + p
            return p + 1

        return lax.fori_loop(lo, end_ref[e], pad_row, p)

    lax.fori_loop(0, ne, pad_expert, 0)

    def tok_block(i, c):
        for j in range(unroll):
            t = i * unroll + j
            slot_ref[d1_ref[t]] = t
            slot_ref[d2_ref[t]] = n_tok + t
        return c

    lax.fori_loop(0, n_tok // unroll, tok_block, 0)


def _expert_kernel(te_ref, nv_ref, nb_ref, slot_ref, hp_ref, wg_ref, wu_ref, wd_ref, ys_ref,
                   xb0_ref, xb1_ref, yb0_ref, yb1_ref, wgb_ref, wub_ref, wdb_ref, gsem, ssem, *, n_tok):
    i = pl.program_id(0)
    nv = nv_ref[0]
    tile = xb0_ref.shape[0]
    pk = ys_ref.shape[1]
    blk = MOE_DMA_BLOCK

    def for_blocks(tile_idx, fn):
        nb = nb_ref[tile_idx]
        for b in range(tile // blk):
            @pl.when(b < nb)
            def _():
                fn(b)

    def gather(tile_idx, xb_ref, sem):
        def issue(b):
            for r in range(b * blk, (b + 1) * blk):
                s = slot_ref[tile_idx * tile + r]
                tok = jnp.where(s >= 2 * n_tok, 0, jnp.where(s >= n_tok, s - n_tok, s))
                pltpu.make_async_copy(hp_ref.at[pl.ds(tok, 1)], xb_ref.at[pl.ds(r, 1)], sem).start(priority=1)
        for_blocks(tile_idx, issue)

    def wait_gather(tile_idx, xb_ref, sem):
        part = xb_ref.at[pl.ds(0, blk)]
        for_blocks(tile_idx, lambda b: pltpu.make_async_copy(part, part, sem).wait())

    def scatter(tile_idx, yb_ref, sem):
        def issue(b):
            for r in range(b * blk, (b + 1) * blk):
                s = slot_ref[tile_idx * tile + r]
                pltpu.make_async_copy(yb_ref.at[pl.ds(r * pk, pk)], ys_ref.at[s], sem).start(priority=r % 2)
        for_blocks(tile_idx, issue)

    def wait_scatter(tile_idx, yb_ref, sem):
        part = yb_ref.at[pl.ds(0, blk * pk)]
        for_blocks(tile_idx, lambda b: pltpu.make_async_copy(part, part, sem).wait())

    new_expert = (i == 0) | (te_ref[i] != te_ref[jnp.maximum(i - 1, 0)])

    @pl.when(new_expert & (i < nv))
    def _():
        wgb_ref[...] = wg_ref[0, 0].astype(BF16)
        wub_ref[...] = wu_ref[0, 0].astype(BF16)
        wdb_ref[...] = wd_ref[0, 0].astype(BF16)

    @pl.when(i == 0)
    def _():
        xb0_ref[...] = jnp.zeros_like(xb0_ref)
        xb1_ref[...] = jnp.zeros_like(xb1_ref)
        gather(0, xb0_ref, gsem.at[0])

    def step(p, xb_ref, xb_next_ref, yb_ref, yb_other_ref):
        nxt = jnp.minimum(i + 1, nv - 1)
        gather(nxt, xb_next_ref, gsem.at[1 - p])
        wait_gather(i, xb_ref, gsem.at[p])

        @pl.when(i >= 2)
        def _():
            wait_scatter(i - 2, yb_ref, ssem.at[p])

        x = _unpack_bf16_pairs(xb_ref[...]).astype(BF16)
        hg = jnp.dot(x, wgb_ref[...], preferred_element_type=F32)
        hu = jnp.dot(x, wub_ref[...], preferred_element_type=F32)
        hid = (hg * jax.nn.sigmoid(hg) * hu).astype(BF16)
        y = jnp.dot(hid, wdb_ref[...], preferred_element_type=F32)
        yp = _pack_bf16_pairs(y)
        lanes = yb_ref.shape[1]
        for j in range(pk):
            yb_ref[pl.ds(j, tile, stride=pk), :] = yp[:, j * lanes:(j + 1) * lanes]
        scatter(i, yb_ref, ssem.at[p])

        @pl.when(i == nv - 1)
        def _():
            wait_gather(nxt, xb_next_ref, gsem.at[1 - p])
            wait_scatter(i, yb_ref, ssem.at[p])

            @pl.when(i >= 1)
            def _():
                wait_scatter(i - 1, yb_other_ref, ssem.at[1 - p])

    odd = lax.rem(i, 2) == 1

    @pl.when((i < nv) & jnp.logical_not(odd))
    def _():
        step(0, xb0_ref, xb1_ref, yb0_ref, yb1_ref)

    @pl.when((i < nv) & odd)
    def _():
        step(1, xb1_ref, xb0_ref, yb1_ref, yb0_ref)


def _combine_kernel(x_ref, w_ref, gt_ref, fn_ref, ya_ref, yb_ref, o_ref, *, final_norm):
    w = w_ref[...]
    tm = x_ref.shape[0]
    pk = ya_ref.shape[0] // tm

    def rows(ref):
        return _unpack_bf16_pairs(jnp.concatenate([ref[pl.ds(j, tm, stride=pk), :] for j in range(pk)], axis=1))

    y = w[:, 0:1] * rows(ya_ref) + w[:, 1:2] * rows(yb_ref)
    out = x_ref[...] + gt_ref[0] * y
    if final_norm:
        ms = jnp.mean(out * out, axis=-1, keepdims=True)
        out = out * lax.rsqrt(ms + EPS) * fn_ref[...]
    o_ref[...] = out


def _hier_moe(x2d, seq, g, sh, sc, gate, r1_w, r1_b, r2_w, r2_b, wg, wu, wd, layer, final_w):
    t, d = x2d.shape
    ne = MOE_EXPERTS
    rw = jnp.zeros((128, d), F32).at[0:MOE_GROUPS].set(r1_w.T).at[8:8 + ne].set(r2_w.T)
    rb = jnp.zeros((128, 1), F32).at[0:MOE_GROUPS, 0].set(r1_b).at[8:8 + ne, 0].set(r2_b)
    rw_hi = rw.astype(BF16)
    rw_lo = (rw - rw_hi.astype(F32)).astype(BF16)
    tmr = min(512, seq)
    bmap = lambda i: ((i * tmr) // seq, 0, 0)
    route, cnt, hp = pl.pallas_call(
        _router_kernel,
        out_shape=(jax.ShapeDtypeStruct((8, t), F32), jax.ShapeDtypeStruct((ne, 128), F32),
                   jax.ShapeDtypeStruct((t, d // 2), jnp.uint32)),
        grid=(t // tmr,),
        in_specs=[pl.BlockSpec((tmr, d), lambda i: (i, 0)),
                  pl.BlockSpec((1, d), lambda i: (0, 0)),
                  pl.BlockSpec((1, 1, d), bmap), pl.BlockSpec((1, 1, d), bmap),
                  pl.BlockSpec((128, d), lambda i: (0, 0)), pl.BlockSpec((128, d), lambda i: (0, 0)),
                  pl.BlockSpec((128, 1), lambda i: (0, 0))],
        out_specs=(pl.BlockSpec((8, tmr), lambda i: (0, i)),
                   pl.BlockSpec((ne, 128), lambda i: (0, 0)),
                   pl.BlockSpec((tmr, d // 2), lambda i: (i, 0))),
        scratch_shapes=[pltpu.VMEM((ne, 1), F32)],
        compiler_params=_cparams("arbitrary"), name="moe_router",
    )(x2d, g.reshape(1, d), sh[:, None, :], sc[:, None, :], rw_hi, rw_lo, rb)

    e1 = route[0].astype(jnp.int32)
    e2 = route[1].astype(jnp.int32)
    tile = EXPERT_TILE
    counts = cnt[:, 0].astype(jnp.int32)
    padded = ((counts + tile - 1) // tile) * tile
    ends = jnp.cumsum(padded)
    offs = ends - padded
    dest1 = offs[e1] + route[4].astype(jnp.int32)
    dest2 = offs[e2] + route[5].astype(jnp.int32)
    nt = (2 * t) // tile + ne
    n_used = ends[-1] // tile
    tid = jnp.arange(nt, dtype=jnp.int32)
    tsrc = jnp.minimum(tid, n_used - 1)
    texp = jnp.minimum(jnp.sum((tsrc[:, None] * tile >= ends[None, :]).astype(jnp.int32), axis=1), ne - 1)
    wts = jnp.concatenate([route[2:4].T, jnp.zeros((t, 6), F32)], axis=1)
    tvalid = jnp.clip(offs[texp] + counts[texp] - tsrc * tile, 0, tile)
    tblocks = (tvalid + MOE_DMA_BLOCK - 1) // MOE_DMA_BLOCK

    smem = pl.BlockSpec(memory_space=pltpu.SMEM)
    pk = d // 256
    slot = pl.pallas_call(
        functools.partial(_invert_kernel, n_tok=t, unroll=8),
        out_shape=jax.ShapeDtypeStruct((nt * tile,), jnp.int32),
        in_specs=[smem] * 5, out_specs=smem, name="moe_invert",
    )(dest1, dest2, counts, offs, ends)

    ff = wg.shape[3]
    n_trash = ne * tile
    ys = pl.pallas_call(
        functools.partial(_expert_kernel, n_tok=t),
        out_shape=jax.ShapeDtypeStruct((2 * t + n_trash, pk, 128), jnp.uint32),
        grid_spec=pltpu.PrefetchScalarGridSpec(
            num_scalar_prefetch=4, grid=(nt,),
            in_specs=[pl.BlockSpec(memory_space=pl.ANY),
                      pl.BlockSpec((1, 1, d, ff), lambda i, te, nv, nb, sl: (layer, te[i], 0, 0)),
                      pl.BlockSpec((1, 1, d, ff), lambda i, te, nv, nb, sl: (layer, te[i], 0, 0)),
                      pl.BlockSpec((1, 1, ff, d), lambda i, te, nv, nb, sl: (layer, te[i], 0, 0))],
            out_specs=pl.BlockSpec(memory_space=pl.ANY),
            scratch_shapes=[pltpu.VMEM((tile, d // 2), jnp.uint32), pltpu.VMEM((tile, d // 2), jnp.uint32),
                            pltpu.VMEM((tile * pk, 128), jnp.uint32), pltpu.VMEM((tile * pk, 128), jnp.uint32),
                            pltpu.VMEM((d, ff), BF16), pltpu.VMEM((d, ff), BF16), pltpu.VMEM((ff, d), BF16),
                            pltpu.SemaphoreType.DMA((2,)), pltpu.SemaphoreType.DMA((2,))]),
        compiler_params=_cparams("arbitrary"), name="moe_experts",
    )(texp, n_used.reshape(1).astype(jnp.int32), tblocks.astype(jnp.int32), slot, hp, wg, wu, wd)

    ys2 = ys.reshape(-1, 128)
    fin = final_w is not None
    fw = (final_w if fin else jnp.ones((d,), F32)).reshape(1, d)
    tmc = min(512, seq)
    nbc = t // tmc
    return pl.pallas_call(
        functools.partial(_combine_kernel, final_norm=fin),
        out_shape=jax.ShapeDtypeStruct((t, d), F32), grid=(nbc,),
        in_specs=[pl.BlockSpec((tmc, d), lambda i: (i, 0)),
                  pl.BlockSpec((tmc, 8), lambda i: (i, 0)),
                  pl.BlockSpec((1, 1, d), lambda i: ((i * tmc) // seq, 0, 0)),
                  pl.BlockSpec((1, d), lambda i: (0, 0)),
                  pl.BlockSpec((tmc * pk, 128), lambda i: (i, 0)),
                  pl.BlockSpec((tmc * pk, 128), lambda i: (i + nbc, 0))],
        out_specs=pl.BlockSpec((tmc, d), lambda i: (i, 0)),
        compiler_params=_cparams("arbitrary"), name="moe_combine",
    )(x2d, wts, gate[:, None, :], fw, ys2, ys2)


def _ssd_kernel(z_ref, x_ref, b_ref, c_ref, dt_ref, cwx_ref, cwb_ref, cwc_ref, cbx_ref, cbb_ref, cbc_ref,
                dtb_ref, alog_ref, dsk_ref, nw_ref, o_ref,
                xe_ref, be_ref, ce_ref, xs_ref, bs_ref, cs_ref, st_ref, acs_ref, dtg_ref):
    c = pl.program_id(1)
    lq = SSD_CHUNK
    ng = SSD_GROUPS
    gw = x_ref.shape[1] // ng
    hpg = gw // SSD_HEAD_DIM
    halo = xe_ref.shape[0] - lq

    @pl.when(c == 0)
    def _():
        xe_ref[0:halo] = jnp.zeros((halo, xe_ref.shape[1]), BF16)
        be_ref[0:halo] = jnp.zeros((halo, be_ref.shape[1]), BF16)
        ce_ref[0:halo] = jnp.zeros((halo, ce_ref.shape[1]), BF16)
        st_ref[...] = jnp.zeros_like(st_ref)

    xe_ref[halo:halo + lq] = x_ref[...]
    be_ref[halo:halo + lq] = b_ref[...]
    ce_ref[halo:halo + lq] = c_ref[...]

    sr = lax.broadcasted_iota(jnp.int32, (SSD_CONV * lq, halo + lq), 0)
    sm = lax.broadcasted_iota(jnp.int32, (SSD_CONV * lq, halo + lq), 1)
    tap = sr // lq
    shift_mat = jnp.where(sm == (sr - tap * lq) + halo - (SSD_CONV - 1) + tap, 1.0, 0.0).astype(BF16)

    def conv_silu(e_ref, w_ref, bias_ref, o_ref, width):
        for c0 in range(0, e_ref.shape[1], width):
            cols = slice(c0, c0 + width)
            e = e_ref[:, cols]
            acc = bias_ref[:, cols] + e[halo:].astype(F32) * w_ref[SSD_CONV - 1:SSD_CONV, cols]
            for k in range(SSD_CONV - 1):
                sh = jnp.dot(shift_mat[k * lq:(k + 1) * lq], e, preferred_element_type=F32)
                acc = acc + sh * w_ref[k:k + 1, cols]
            o_ref[:, cols] = acc + acc * jnp.tanh(acc)

    conv_silu(xe_ref, cwx_ref, cbx_ref, xs_ref, gw)
    conv_silu(be_ref, cwb_ref, cbb_ref, bs_ref, gw)
    conv_silu(ce_ref, cwc_ref, cbc_ref, cs_ref, gw)

    dt = jax.nn.softplus(dt_ref[...] + dtb_ref[...])
    adt = dt * (-jnp.exp(alog_ref[...]))
    li = lax.broadcasted_iota(jnp.int32, (lq, lq), 0)
    si = lax.broadcasted_iota(jnp.int32, (lq, lq), 1)
    tri = jnp.where(si <= li, 1.0, 0.0)
    acs = jnp.dot(tri, adt, precision=HIGHEST, preferred_element_type=F32)
    lane = lax.broadcasted_iota(jnp.int32, (lq, 128), 1)
    for gi in range(ng):
        sh = (128 - gi * hpg) % 128
        acs_ref[gi] = jnp.where(lane < hpg, pltpu.roll(acs, sh, axis=1) if sh else acs, 0.0)
        dtg_ref[gi] = jnp.where(lane < hpg, pltpu.roll(dt, sh, axis=1) if sh else dt, 0.0)
    causal = si <= li
    lane_lo = lane < SSD_HEAD_DIM

    def group_body(gi, carry):
        xc0 = pl.multiple_of(gi * gw, gw)
        nc0 = pl.multiple_of(gi * SSD_STATE, SSD_STATE)
        xg = xs_ref[:, pl.ds(xc0, gw)]
        bg = bs_ref[:, pl.ds(nc0, SSD_STATE)]
        cg = cs_ref[:, pl.ds(nc0, SSD_STATE)]
        bt = bg.T
        cgb = cg.astype(BF16)
        cb = jnp.dot(cgb, bt.astype(BF16), preferred_element_type=F32)
        acs_g = acs_ref[gi]
        acs_t = acs_g.T
        dt_t = dtg_ref[gi].T
        xgb = xg.astype(BF16)
        ys = []
        for jp in range(hpg // 2):
            lhs_parts, w_parts, ecols, dlast = [], [], [], []
            for j in (2 * jp, 2 * jp + 1):
                e_col = jnp.broadcast_to(acs_g[:, j:j + 1], (lq, lq))
                dt_row = dt_t[j:j + 1, :]
                r_row = jnp.broadcast_to(acs_t[j:j + 1, :] - jnp.log(dt_row), (lq, lq))
                lmat = jnp.where(causal, jnp.exp(e_col - r_row), 0.0) * cb
                lhs_parts.append(lmat.astype(BF16))
                a_last = acs_t[j:j + 1, lq - 1:lq]
                wrow = dt_row * jnp.exp(a_last - acs_t[j:j + 1, :])
                w_parts.append((bt * wrow).astype(BF16))
                ecols.append(e_col)
                dlast.append(jnp.exp(a_last))
            lhs = jnp.concatenate([jnp.concatenate(lhs_parts, axis=1),
                                   jnp.concatenate(w_parts, axis=1)], axis=0)
            xp = xgb[:, jp * 128:(jp + 1) * 128]
            zero = jnp.zeros_like(xp)
            rhs = jnp.concatenate([jnp.where(lane_lo, xp, zero), jnp.where(lane_lo, zero, xp)], axis=0)
            res = jnp.dot(lhs, rhs, preferred_element_type=F32)
            st = st_ref[gi, :, jp * 128:(jp + 1) * 128]
            yoff = jnp.dot(cgb, st.astype(BF16), preferred_element_type=F32)
            escale = jnp.exp(jnp.where(lane_lo, ecols[0], ecols[1]))
            ys.append(res[:lq] + yoff * escale)
            dl = jnp.where(lane_lo[:1], dlast[0], dlast[1])
            st_ref[gi, :, jp * 128:(jp + 1) * 128] = st * dl + res[lq:]
        y = jnp.concatenate(ys, axis=1) + dsk_ref[:, pl.ds(xc0, gw)] * xg
        zg = z_ref[:, pl.ds(xc0, gw)].astype(F32)
        y = y * (zg * jax.nn.sigmoid(zg))
        ms = jnp.mean(y * y, axis=-1, keepdims=True)
        y = y * lax.rsqrt(ms + EPS) * nw_ref[:, pl.ds(xc0, gw)]
        o_ref[:, pl.ds(xc0, gw)] = y.astype(o_ref.dtype)
        return carry

    lax.fori_loop(0, ng, group_body, 0)
    xe_ref[0:halo] = xe_ref[lq:lq + halo]
    be_ref[0:halo] = be_ref[lq:lq + halo]
    ce_ref[0:halo] = ce_ref[lq:lq + halo]


def _ssd_mixer(zx, dt_raw, bsz, seq, conv_w, conv_b, dt_bias, a_log, d_skip, norm_w):
    t = zx.shape[0]
    heads = a_log.shape[0]
    inner = heads * SSD_HEAD_DIM
    bcw = SSD_GROUPS * SSD_STATE
    lq = SSD_CHUNK
    nc = seq // lq
    pad = 128 - heads
    cw = 0.5 * conv_w.astype(F32)
    cwp = jnp.zeros((8, cw.shape[1]), F32).at[:SSD_CONV].set(cw)
    cb = 0.5 * conv_b.astype(F32).reshape(1, -1)
    row = lambda b, c: (b * nc + c, 0)
    const = lambda b, c: (0, 0)
    xb = inner // inner
    bb = (2 * inner) // bcw
    return pl.pallas_call(
        _ssd_kernel, out_shape=jax.ShapeDtypeStruct((t, inner), BF16), grid=(bsz, nc),
        in_specs=[pl.BlockSpec((lq, inner), lambda b, c: (b * nc + c, 0)),
                  pl.BlockSpec((lq, inner), lambda b, c: (b * nc + c, xb)),
                  pl.BlockSpec((lq, bcw), lambda b, c: (b * nc + c, bb)),
                  pl.BlockSpec((lq, bcw), lambda b, c: (b * nc + c, bb + 1)),
                  pl.BlockSpec((lq, 128), row),
                  pl.BlockSpec((8, inner), const), pl.BlockSpec((8, bcw), const), pl.BlockSpec((8, bcw), const),
                  pl.BlockSpec((1, inner), const), pl.BlockSpec((1, bcw), const), pl.BlockSpec((1, bcw), const),
                  pl.BlockSpec((1, 128), const), pl.BlockSpec((1, 128), const),
                  pl.BlockSpec((1, inner), const), pl.BlockSpec((1, inner), const)],
        out_specs=pl.BlockSpec((lq, inner), row),
        scratch_shapes=[pltpu.VMEM((lq + 16, inner), BF16), pltpu.VMEM((lq + 16, bcw), BF16),
                        pltpu.VMEM((lq + 16, bcw), BF16),
                        pltpu.VMEM((lq, inner), F32), pltpu.VMEM((lq, bcw), F32), pltpu.VMEM((lq, bcw), F32),
                        pltpu.VMEM((SSD_GROUPS, SSD_STATE, inner // SSD_GROUPS), F32),
                        pltpu.VMEM((SSD_GROUPS, lq, 128), F32), pltpu.VMEM((SSD_GROUPS, lq, 128), F32)],
        compiler_params=_cparams("arbitrary", "arbitrary"), name="ssd_mixer",
    )(zx, zx, zx, zx, dt_raw,
      cwp[:, :inner], cwp[:, inner:inner + bcw], cwp[:, inner + bcw:],
      cb[:, :inner], cb[:, inner:inner + bcw], cb[:, inner + bcw:],
      jnp.pad(dt_bias.astype(F32), (0, pad)).reshape(1, 128),
      jnp.pad(a_log.astype(F32), (0, pad)).reshape(1, 128),
      jnp.repeat(d_skip.astype(F32), SSD_HEAD_DIM).reshape(1, inner),
      norm_w.astype(F32).reshape(1, inner))


def kernel(x, c, positions, ada_w, ada_b, norm_mix, norm_ffn, mix_a_w_in, s5_lam_re, s5_lam_im, s5_log_dt, s5_b_re, s5_b_im, s5_c_re, s5_c_im, s5_d, s5_w_glu, attn_sinks, mix_a_w_out, ssd_w_in, ssd_conv_w, ssd_conv_b, ssd_dt_bias, ssd_a_log, ssd_d, ssd_norm_w, ssd_w_out, moe_r1_w, moe_r1_b, moe_r2_w, moe_r2_b, moe_w_gate, moe_w_up, moe_w_down, final_norm):
    bsz, seq, d = x.shape
    depth = ada_w.shape[0]
    t = bsz * seq
    xf = x.reshape(t, d).astype(F32)
    mod = _adaln(c.astype(F32), ada_w, ada_b)
    cos_t, sin_t = _rope_tables(positions)

    for i in range(depth):
        sh1, sc1, g1, sh2, sc2, g2 = jnp.split(mod[i], 6, axis=-1)
        j = i // 2
        if i % 2 == 0:
            s5w = s5_lam_re.shape[1] * S5_GROUP
            w_in = mix_a_w_in[j]
            u, qkv = _inproj0(xf, seq, norm_mix[i], sh1, sc1, w_in[:, :s5w], w_in[:, s5w:], tm=512)
            params = _s5_params(s5_lam_re[j], s5_lam_im[j], s5_log_dt[j], s5_b_re[j], s5_b_im[j],
                                s5_c_re[j], s5_c_im[j], s5_d[j], seq)
            y_s5 = _s5_mixer(u, bsz, seq, params)
            y_glu = _gelu_glu(y_s5, s5_w_glu[j], tm=2048, tn=512)
            wq = attn_sinks.shape[1] * HEAD_DIM
            wk = KV_HEADS * HEAD_DIM
            y_att = _attention(qkv, bsz, seq, cos_t, sin_t, attn_sinks[j],
                               q_col=0, k_col=wq // wk, v_col=wq // wk + 1)
            xf = _matmul_residual([y_glu, y_att], mix_a_w_out[j], xf, seq, g1, tm=1024, tn=1024)
        else:
            heads = ssd_a_log.shape[1]
            inner = heads * SSD_HEAD_DIM
            main = inner + inner + 2 * SSD_GROUPS * SSD_STATE
            w_in = ssd_w_in[j]
            w_dt = jnp.pad(w_in[:, main:], ((0, 0), (0, 128 - heads))).astype(BF16)
            zx, dt_raw = _inproj1(xf, seq, norm_mix[i], sh1, sc1, w_in, main, w_dt, tm=1024, tn=1024)
            y = _ssd_mixer(zx, dt_raw, bsz, seq, ssd_conv_w[j], ssd_conv_b[j], ssd_dt_bias[j],
                           ssd_a_log[j], ssd_d[j], ssd_norm_w[j])
            xf = _matmul_residual([y], ssd_w_out[j], xf, seq, g1, tm=1024, tn=512)
        xf = _hier_moe(xf, seq, norm_ffn[i], sh2, sc2, g2, moe_r1_w[i], moe_r1_b[i], moe_r2_w[i], moe_r2_b[i],
                       moe_w_gate, moe_w_up, moe_w_down, i,
                       final_norm if i == depth - 1 else None)
    return xf.reshape(bsz, seq, d).astype(x.dtype)
```

```python
import functools
import math

import jax
import jax.numpy as jnp
from jax import lax
from jax.experimental import pallas as pl
from jax.experimental.pallas import tpu as pltpu

F32 = jnp.float32
BF16 = jnp.bfloat16
HIGHEST = lax.Precision.HIGHEST

EPS = 1e-6
S5_GROUP = 16
S5_STATE = 64
S5_CHUNK = 16
HEAD_DIM = 64
KV_HEADS = 2
WINDOW = 128
ROPE_DIM = 16
ROPE_THETA = 500000.0
SSD_HEAD_DIM = 64
SSD_GROUPS = 8
SSD_STATE = 128
SSD_CONV = 4
SSD_CHUNK = 128
MOE_GROUPS = 4
MOE_EPG = 8
MOE_EXPERTS = MOE_GROUPS * MOE_EPG
EXPERT_TILE = 256
MOE_DMA_BLOCK = 32
NEG_BIG = -1e30
VMEM_LIMIT = 56 * 1024 * 1024


def _cparams(*sem):
    return pltpu.CompilerParams(dimension_semantics=sem, vmem_limit_bytes=VMEM_LIMIT)


def _modnorm(x, g, sh, sc):
    ms = jnp.mean(x * x, axis=-1, keepdims=True)
    return (x * lax.rsqrt(ms + EPS) * g) * (1.0 + sc) + sh


def _adaln_kernel(c_ref, w_ref, b_ref, o_ref):
    c = c_ref[...]
    a = c * jax.nn.sigmoid(c)
    o_ref[0] = jnp.dot(a, w_ref[0], precision=HIGHEST, preferred_element_type=F32) + b_ref[0]


def _adaln(c, ada_w, ada_b):
    depth, d, n = ada_w.shape
    bsz = c.shape[0]
    rows = 8
    cp = jnp.zeros((rows, d), F32).at[:bsz].set(c)
    tn = 2048
    out = pl.pallas_call(
        _adaln_kernel,
        out_shape=jax.ShapeDtypeStruct((depth, rows, n), F32),
        grid=(depth, n // tn),
        in_specs=[pl.BlockSpec((rows, d), lambda l, j: (0, 0)),
                  pl.BlockSpec((1, d, tn), lambda l, j: (l, 0, j)),
                  pl.BlockSpec((1, 1, tn), lambda l, j: (l, 0, j))],
        out_specs=pl.BlockSpec((1, rows, tn), lambda l, j: (l, 0, j)),
        compiler_params=_cparams("arbitrary", "arbitrary"),
        name="adaln",
    )(cp, ada_w, ada_b.reshape(depth, 1, n))
    return out[:, :bsz]


def _inproj1_kernel(x_ref, g_ref, sh_ref, sc_ref, w_ref, w2_ref, o_ref, o2_ref, h_ref):
    @pl.when(pl.program_id(1) == 0)
    def _():
        h = _modnorm(x_ref[...], g_ref[...], sh_ref[0], sc_ref[0]).astype(BF16)
        h_ref[...] = h
        o2_ref[...] = jnp.dot(h, w2_ref[...], preferred_element_type=F32)

    o_ref[...] = jnp.dot(h_ref[...], w_ref[...].astype(BF16), preferred_element_type=F32).astype(o_ref.dtype)


def _inproj1(x2d, seq, g, sh, sc, w, n_main, w2, tm, tn):
    t, d = x2d.shape
    tm = min(tm, seq)
    n2 = w2.shape[1]
    bmap = lambda i, j: ((i * tm) // seq, 0, 0)
    return pl.pallas_call(
        _inproj1_kernel,
        out_shape=(jax.ShapeDtypeStruct((t, n_main), BF16), jax.ShapeDtypeStruct((t, n2), F32)),
        grid=(t // tm, n_main // tn),
        in_specs=[pl.BlockSpec((tm, d), lambda i, j: (i, 0)),
                  pl.BlockSpec((1, d), lambda i, j: (0, 0)),
                  pl.BlockSpec((1, 1, d), bmap), pl.BlockSpec((1, 1, d), bmap),
                  pl.BlockSpec((d, tn), lambda i, j: (0, j)),
                  pl.BlockSpec((d, n2), lambda i, j: (0, 0))],
        out_specs=(pl.BlockSpec((tm, tn), lambda i, j: (i, j)), pl.BlockSpec((tm, n2), lambda i, j: (i, 0))),
        scratch_shapes=[pltpu.VMEM((tm, d), BF16)],
        compiler_params=_cparams("arbitrary", "arbitrary"), name="inproj1",
    )(x2d, g.reshape(1, d), sh[:, None, :], sc[:, None, :], w, w2)


def _mmres_kernel(*refs, n_in):
    a_refs, w_refs = refs[:n_in], refs[n_in:2 * n_in]
    x_ref, gt_ref, o_ref = refs[2 * n_in:]
    acc = jnp.dot(a_refs[0][...], w_refs[0][...].astype(BF16), preferred_element_type=F32)
    for a_ref, w_ref in zip(a_refs[1:], w_refs[1:]):
        acc = acc + jnp.dot(a_ref[...], w_ref[...].astype(BF16), preferred_element_type=F32)
    o_ref[...] = x_ref[...] + gt_ref[0] * acc


def _matmul_residual(a_list, w, x2d, seq, gate, tm, tn):
    t, n = x2d.shape
    tm = min(tm, seq)
    in_specs, row = [], 0
    for a in a_list:
        in_specs.append(pl.BlockSpec((tm, a.shape[1]), lambda i, j: (i, 0)))
    for a in a_list:
        k = a.shape[1]
        in_specs.append(pl.BlockSpec((k, tn), functools.partial(lambda i, j, rb: (rb, j), rb=row // k)))
        row += k
    in_specs.append(pl.BlockSpec((tm, tn), lambda i, j: (i, j)))
    in_specs.append(pl.BlockSpec((1, 1, tn), lambda i, j: ((i * tm) // seq, 0, j)))
    return pl.pallas_call(
        functools.partial(_mmres_kernel, n_in=len(a_list)),
        out_shape=jax.ShapeDtypeStruct((t, n), F32), grid=(t // tm, n // tn),
        in_specs=in_specs, out_specs=pl.BlockSpec((tm, tn), lambda i, j: (i, j)),
        compiler_params=_cparams("arbitrary", "arbitrary"), name="matmul_residual",
    )(*a_list, *([w] * len(a_list)), x2d, gate[:, None, :])


def _inproj0_kernel(x_ref, g_ref, sh_ref, sc_ref, wu_ref, wa_ref, u_ref, a_ref, wub_ref, wab_ref):
    @pl.when(pl.program_id(0) == 0)
    def _():
        wub_ref[...] = wu_ref[...].astype(BF16)
        wab_ref[...] = wa_ref[...].astype(BF16)

    h = _modnorm(x_ref[...], g_ref[...], sh_ref[0], sc_ref[0]).astype(BF16)
    u_ref[...] = jnp.dot(h, wub_ref[...], preferred_element_type=F32)
    a_ref[...] = jnp.dot(h, wab_ref[...], preferred_element_type=F32).astype(a_ref.dtype)


def _inproj0(x2d, seq, g, sh, sc, w_u, w_a, tm):
    t, d = x2d.shape
    tm = min(tm, seq)
    nu, na = w_u.shape[1], w_a.shape[1]
    bmap = lambda i: ((i * tm) // seq, 0, 0)
    return pl.pallas_call(
        _inproj0_kernel,
        out_shape=(jax.ShapeDtypeStruct((t, nu), F32), jax.ShapeDtypeStruct((t, na), BF16)),
        grid=(t // tm,),
        in_specs=[pl.BlockSpec((tm, d), lambda i: (i, 0)),
                  pl.BlockSpec((1, d), lambda i: (0, 0)),
                  pl.BlockSpec((1, 1, d), bmap), pl.BlockSpec((1, 1, d), bmap),
                  pl.BlockSpec((d, nu), lambda i: (0, 0), pipeline_mode=pl.Buffered(1)),
                  pl.BlockSpec((d, na), lambda i: (0, 0), pipeline_mode=pl.Buffered(1))],
        out_specs=(pl.BlockSpec((tm, nu), lambda i: (i, 0)), pl.BlockSpec((tm, na), lambda i: (i, 0))),
        scratch_shapes=[pltpu.VMEM((d, nu), BF16), pltpu.VMEM((d, na), BF16)],
        compiler_params=_cparams("arbitrary"), name="inproj0",
    )(x2d, g.reshape(1, d), sh[:, None, :], sc[:, None, :], w_u, w_a)


S5_SLAB = 8
S5_SEGS = 8


def _s5_params(lam_re, lam_im, log_dt, b_re, b_im, c_re, c_im, d_skip, seq):
    g, n = lam_re.shape
    p = S5_GROUP
    lc = S5_CHUNK
    gs = S5_SLAB
    nj = g // gs
    lam = lax.complex(lam_re.astype(F32), lam_im.astype(F32))
    dt = jnp.exp(log_dt.astype(F32))[:, None]
    ld = lam * dt
    lam_bar = jnp.exp(ld)
    bmat = lax.complex(b_re.astype(F32), b_im.astype(F32))
    b_bar = ((lam_bar - 1.0) / lam)[..., None] * bmat
    cmat = lax.complex(c_re.astype(F32), c_im.astype(F32))
    ks = jnp.arange(lc, dtype=F32)
    pw = jnp.exp(ld[:, None, :] * ks[None, :, None])
    pw1 = jnp.exp(ld[:, None, :] * (ks[None, :, None] + 1.0))
    kk = jnp.real(jnp.einsum('gpn,gkn,gnq->gkpq', cmat, pw, b_bar, precision=HIGHEST))
    eg = jnp.eye(gs, dtype=F32)
    ex_p = jnp.einsum('ab,pr->apbr', eg, jnp.eye(p, dtype=F32)).reshape(gs, p, gs * p)
    ex_n = jnp.einsum('ab,nr->anbr', eg, jnp.eye(n, dtype=F32)).reshape(gs, n, gs * n)
    kq_ = kk.transpose(0, 1, 3, 2).reshape(nj, gs, lc, p, p)
    kexp = jnp.einsum('jakqp,apc->jkaqc', kq_, ex_p, precision=HIGHEST).reshape(nj, lc, gs * p, gs * p)
    kexp = kexp.at[:, 0].add(jnp.eye(gs * p, dtype=F32)[None] * d_skip.astype(F32).reshape(nj, 1, gs * p))
    kexp = kexp.astype(BF16)
    wb = (pw[:, ::-1, :, None] * b_bar[:, None, :, :]).transpose(0, 1, 3, 2)
    wb = wb.reshape(nj, gs, lc, p, n)
    wb_slab = jnp.concatenate(
        [jnp.einsum('jasqn,anc->jsaqc', part.astype(BF16), ex_n.astype(BF16)).reshape(nj, lc * gs * p, gs * n)
         for part in (jnp.real(wb), jnp.imag(wb))], axis=-1).astype(BF16)
    cl = (cmat[:, None, :, :] * pw1[:, :, None, :]).transpose(0, 3, 1, 2)
    cl = cl.reshape(nj, gs, n, lc, p)
    woff_slab = jnp.concatenate(
        [jnp.einsum('janlp,apc->janlc', part.astype(BF16), ex_p.astype(BF16)).reshape(nj, gs * n, lc * gs * p)
         for part in (jnp.real(cl), -jnp.imag(cl))], axis=1).astype(BF16)
    seglen = (seq // lc) // S5_SEGS
    kq = jnp.arange(seglen + 1, dtype=F32) * float(lc)
    pq = jnp.exp(ld[:, None, :] * kq[None, :, None]).reshape(nj, gs, seglen + 1, n)
    pq = pq.transpose(0, 2, 1, 3).reshape(nj, seglen + 1, gs * n)
    return kexp, wb_slab, woff_slab, jnp.real(pq), jnp.imag(pq)


def _s5_kernel(u_ref, k_ref, wb_ref, woff_ref, pre_ref, pim_ref, y_ref, sloc_ref, sin_ref, m_ref, *, nbatch):
    lc = S5_CHUNK
    w = u_ref.shape[1]

    @pl.when((pl.program_id(0) == 0) & (pl.program_id(1) == 0))
    def _():
        m_ref[...] = jnp.zeros_like(m_ref)

    @pl.when(pl.program_id(1) == 0)
    def _():
        for s in range(lc):
            for l in range(s, lc):
                m_ref[s * w:(s + 1) * w, l * w:(l + 1) * w] = k_ref[0, l - s]

    nc = u_ref.shape[0] // lc
    hw = wb_ref.shape[2] // 2
    nseg = S5_SEGS * nbatch
    seglen = nc // nseg
    u_all = jnp.concatenate([u_ref[pl.ds(s, nc, stride=lc), :] for s in range(lc)], axis=1).astype(BF16)
    nq = hw // w
    sloc = jnp.dot(u_all, wb_ref[0], preferred_element_type=F32)
    for q in range(2 * nq):
        sloc_ref[q] = sloc[:, q * w:(q + 1) * w]
    a_re, a_im = pre_ref[0, 1:2], pim_ref[0, 1:2]

    def put(rows, re, im):
        for q in range(nq):
            sin_ref[q, rows, :] = re[:, q * w:(q + 1) * w]
            sin_ref[nq + q, rows, :] = im[:, q * w:(q + 1) * w]

    def get(ref, rows):
        return (jnp.concatenate([ref[q, rows, :] for q in range(nq)], axis=1),
                jnp.concatenate([ref[nq + q, rows, :] for q in range(nq)], axis=1))

    def seg_step(k, st):
        sre, sim = st
        rows = pl.ds(k, nseg, stride=seglen)
        put(rows, sre, sim)
        lre, lim = get(sloc_ref, rows)
        return a_re * sre - a_im * sim + lre, a_re * sim + a_im * sre + lim

    z = jnp.zeros((nseg, hw), F32)
    ere, eim = lax.fori_loop(0, seglen, seg_step, (z, z))
    g_re, g_im = pre_ref[0, seglen:seglen + 1], pim_ref[0, seglen:seglen + 1]
    cre, cim = jnp.zeros((1, hw), F32), jnp.zeros((1, hw), F32)
    pre, pim = pre_ref[0, 0:seglen], pim_ref[0, 0:seglen]
    for sg in range(1, nseg):
        if sg % S5_SEGS == 0:
            cre, cim = jnp.zeros((1, hw), F32), jnp.zeros((1, hw), F32)
            continue
        cre, cim = (g_re * cre - g_im * cim + ere[sg - 1:sg], g_re * cim + g_im * cre + eim[sg - 1:sg])
        rows = slice(sg * seglen, (sg + 1) * seglen)
        cur_re, cur_im = get(sin_ref, rows)
        put(rows, cur_re + pre * cre - pim * cim, cur_im + pre * cim + pim * cre)
    s_in = jnp.concatenate([sin_ref[q] for q in range(2 * nq)], axis=1).astype(BF16)
    for lb in range(lc // 2):
        c0, c1 = 2 * lb * w, (2 * lb + 2) * w
        y2 = (jnp.dot(u_all[:, :c1], m_ref[:c1, c0:c1], preferred_element_type=F32)
              + jnp.dot(s_in, woff_ref[0, :, c0:c1], preferred_element_type=F32))
        y_ref[pl.ds(2 * lb, nc, stride=lc), :] = y2[:, :w]
        y_ref[pl.ds(2 * lb + 1, nc, stride=lc), :] = y2[:, w:]


def _s5_mixer(u, bsz, seq, params):
    kexp, wb_slab, woff_slab, pq_re, pq_im = params
    nj, km, sw = wb_slab.shape
    w = km // S5_CHUNK
    nbatch = 2 if bsz % 2 == 0 else 1
    rows = nbatch * seq
    nc = rows // S5_CHUNK
    t = u.shape[0]
    const = lambda j, b: (j, 0, 0)
    return pl.pallas_call(
        functools.partial(_s5_kernel, nbatch=nbatch),
        out_shape=jax.ShapeDtypeStruct((t, nj * w), F32), grid=(nj, bsz // nbatch),
        in_specs=[pl.BlockSpec((rows, w), lambda j, b: (b, j)),
                  pl.BlockSpec((1, S5_CHUNK, w, w), lambda j, b: (j, 0, 0, 0)),
                  pl.BlockSpec((1, km, sw), const),
                  pl.BlockSpec((1, sw, km), const),
                  pl.BlockSpec((1,) + pq_re.shape[1:], const),
                  pl.BlockSpec((1,) + pq_im.shape[1:], const)],
        out_specs=pl.BlockSpec((rows, w), lambda j, b: (b, j)),
        scratch_shapes=[pltpu.VMEM((sw // w, nc, w), F32), pltpu.VMEM((sw // w, nc, w), F32),
                        pltpu.VMEM((km, km), BF16)],
        compiler_params=_cparams("arbitrary", "arbitrary"), name="s5_mixer",
    )(u, kexp, wb_slab, woff_slab, pq_re, pq_im)


def _glu_kernel(y_ref, wv_ref, wg_ref, o_ref, a_ref):
    @pl.when(pl.program_id(1) == 0)
    def _():
        a_ref[...] = jax.nn.gelu(y_ref[...].astype(F32), approximate=True).astype(BF16)

    a = a_ref[...]
    val = jnp.dot(a, wv_ref[...].astype(BF16), preferred_element_type=F32)
    gate = jnp.dot(a, wg_ref[...].astype(BF16), preferred_element_type=F32)
    o_ref[...] = (val * jax.nn.sigmoid(gate)).astype(o_ref.dtype)


def _gelu_glu(y, w, tm, tn):
    t, k = y.shape
    n = w.shape[1] // 2
    tm = min(tm, t)
    nb = n // tn
    return pl.pallas_call(
        _glu_kernel, out_shape=jax.ShapeDtypeStruct((t, n), BF16), grid=(t // tm, n // tn),
        in_specs=[pl.BlockSpec((tm, k), lambda i, j: (i, 0)),
                  pl.BlockSpec((k, tn), lambda i, j: (0, j)),
                  pl.BlockSpec((k, tn), lambda i, j: (0, j + nb))],
        out_specs=pl.BlockSpec((tm, tn), lambda i, j: (i, j)),
        scratch_shapes=[pltpu.VMEM((tm, k), BF16)],
        compiler_params=_cparams("arbitrary", "arbitrary"), name="gelu_glu",
    )(y, w, w)


def _rope(x, cos, sin):
    w = x.shape[1]
    half = ROPE_DIM // 2
    d = lax.broadcasted_iota(jnp.int32, x.shape, 1) % HEAD_DIM
    partner = jnp.where(d < half, pltpu.roll(x, w - half, axis=1), pltpu.roll(x, half, axis=1))
    return x * cos + partner * sin


def _attn_kernel(sink_ref, q_ref, kc_ref, kp_ref, vc_ref, vp_ref,
                 cq_ref, sq_ref, cp_ref, sp_ref, o_ref):
    n = pl.program_id(1)
    wq = q_ref.shape[1]
    hq = wq // HEAD_DIM
    grp = hq // KV_HEADS
    cos_c, sin_c = cq_ref[...], sq_ref[...]
    reps = wq // cos_c.shape[1]
    q = _rope(q_ref[...].astype(F32),
              jnp.concatenate([cos_c] * reps, axis=1), jnp.concatenate([sin_c] * reps, axis=1))
    q = (q * (HEAD_DIM ** -0.5)).astype(BF16)
    kc = _rope(kc_ref[...].astype(F32), cos_c, sin_c).astype(BF16)
    kp = _rope(kp_ref[...].astype(F32), cp_ref[...], sp_ref[...]).astype(BF16)
    kk = jnp.concatenate([kp, kc], axis=0)
    vv = jnp.concatenate([vp_ref[...], vc_ref[...]], axis=0)
    qi = lax.broadcasted_iota(jnp.int32, (WINDOW, 2 * WINDOW), 0)
    kj = lax.broadcasted_iota(jnp.int32, (WINDOW, 2 * WINDOW), 1)
    first = jnp.where(n > 0, 0, WINDOW)
    valid = (kj > qi) & (kj <= qi + WINDOW) & (kj >= first)
    outs = []
    for h in range(hq):
        kvh = h // grp
        qh = q[:, h * HEAD_DIM:(h + 1) * HEAD_DIM]
        kh = kk[:, kvh * HEAD_DIM:(kvh + 1) * HEAD_DIM]
        vh = vv[:, kvh * HEAD_DIM:(kvh + 1) * HEAD_DIM]
        s = lax.dot_general(qh, kh, (((1,), (1,)), ((), ())), preferred_element_type=F32)
        s = jnp.where(valid, s, NEG_BIG)
        sink = sink_ref[h]
        mx = jnp.maximum(jnp.max(s, axis=-1, keepdims=True), sink)
        p = jnp.exp(s - mx)
        den = jnp.sum(p, axis=-1, keepdims=True) + jnp.exp(sink - mx)
        o = jnp.dot(p.astype(BF16), vh, preferred_element_type=F32)
        outs.append(o / den)
    o_ref[...] = jnp.concatenate(outs, axis=1).astype(o_ref.dtype)


def _attention(proj, bsz, seq, cos_t, sin_t, sinks, q_col, k_col, v_col):
    t = proj.shape[0]
    nb = seq // WINDOW
    wq = sinks.shape[0] * HEAD_DIM
    wk = KV_HEADS * HEAD_DIM
    cur = lambda b, n, s: (b * nb + n, 0)
    prv = lambda b, n, s: (b * nb + jnp.maximum(n - 1, 0), 0)
    return pl.pallas_call(
        _attn_kernel, out_shape=jax.ShapeDtypeStruct((t, wq), BF16),
        grid_spec=pltpu.PrefetchScalarGridSpec(
            num_scalar_prefetch=1, grid=(bsz, nb),
            in_specs=[pl.BlockSpec((WINDOW, wq), lambda b, n, s: (b * nb + n, q_col)),
                      pl.BlockSpec((WINDOW, wk), lambda b, n, s: (b * nb + n, k_col)),
                      pl.BlockSpec((WINDOW, wk), lambda b, n, s: (b * nb + jnp.maximum(n - 1, 0), k_col)),
                      pl.BlockSpec((WINDOW, wk), lambda b, n, s: (b * nb + n, v_col)),
                      pl.BlockSpec((WINDOW, wk), lambda b, n, s: (b * nb + jnp.maximum(n - 1, 0), v_col)),
                      pl.BlockSpec((WINDOW, wk), cur), pl.BlockSpec((WINDOW, wk), cur),
                      pl.BlockSpec((WINDOW, wk), prv), pl.BlockSpec((WINDOW, wk), prv)],
            out_specs=pl.BlockSpec((WINDOW, wq), lambda b, n, s: (b * nb + n, 0))),
        compiler_params=_cparams("arbitrary", "arbitrary"), name="swa_attention",
    )(sinks.astype(F32), proj, proj, proj, proj, proj, cos_t, sin_t, cos_t, sin_t)


def _rope_tables(positions):
    half = ROPE_DIM // 2
    inv = 1.0 / (ROPE_THETA ** (jnp.arange(0, ROPE_DIM, 2, dtype=F32) / ROPE_DIM))
    ang = positions.astype(F32).reshape(-1, 1) * inv[None, :]
    cos, sin = jnp.cos(ang), jnp.sin(ang)
    t = ang.shape[0]
    ones = jnp.ones((t, HEAD_DIM - ROPE_DIM), F32)
    cos_h = jnp.concatenate([cos, cos, ones], axis=1)
    sin_h = jnp.concatenate([-sin, sin, 0.0 * ones], axis=1)
    return jnp.tile(cos_h, (1, KV_HEADS)), jnp.tile(sin_h, (1, KV_HEADS))


def _pack_bf16_pairs(v):
    half = v.shape[1] // 2
    hi = lax.bitcast_convert_type(v[:, :half].astype(BF16).astype(F32), jnp.uint32)
    lo = lax.bitcast_convert_type(v[:, half:].astype(BF16).astype(F32), jnp.uint32)
    return hi | (lo >> 16)


def _unpack_bf16_pairs(p):
    return jnp.concatenate([lax.bitcast_convert_type(p & jnp.uint32(0xFFFF0000), F32),
                            lax.bitcast_convert_type(p << 16, F32)], axis=1)


def _router_kernel(x_ref, g_ref, sh_ref, sc_ref, rwh_ref, rwl_ref, rb_ref, o_ref, cnt_ref, hp_ref, carry_ref):
    i = pl.program_id(0)
    tm = x_ref.shape[0]
    ne = MOE_EXPERTS

    @pl.when(i == 0)
    def _():
        carry_ref[...] = jnp.zeros_like(carry_ref)

    h = _modnorm(x_ref[...], g_ref[...], sh_ref[0], sc_ref[0])
    hp_ref[...] = _pack_bf16_pairs(h)
    nt_dims = (((1,), (1,)), ((), ()))
    h_hi = h.astype(BF16)
    h_lo = (h - h_hi.astype(F32)).astype(BF16)
    lt = (lax.dot_general(rwh_ref[...], h_hi, nt_dims, preferred_element_type=F32)
          + lax.dot_general(rwh_ref[...], h_lo, nt_dims, preferred_element_type=F32)
          + lax.dot_general(rwl_ref[...], h_hi, nt_dims, preferred_element_type=F32)) + rb_ref[...]
    gl = lt[0:MOE_GROUPS]
    gm = jnp.max(gl, axis=0, keepdims=True)
    gsum = jnp.sum(jnp.exp(gl - gm), axis=0, keepdims=True)
    g_val = 1.0 / gsum
    grow = lax.broadcasted_iota(jnp.int32, gl.shape, 0)
    g_idx = jnp.min(jnp.where(gl == gm, grow, MOE_GROUPS), axis=0, keepdims=True)
    sel = jnp.zeros((MOE_EPG, tm), F32)
    for gi in range(MOE_GROUPS):
        sel = jnp.where(g_idx == gi, lt[8 + gi * MOE_EPG: 8 + (gi + 1) * MOE_EPG], sel)
    erow = lax.broadcasted_iota(jnp.int32, sel.shape, 0)
    m1 = jnp.max(sel, axis=0, keepdims=True)
    i1 = jnp.min(jnp.where(sel == m1, erow, MOE_EPG), axis=0, keepdims=True)
    sel2 = jnp.where(erow == i1, -jnp.inf, sel)
    m2 = jnp.max(sel2, axis=0, keepdims=True)
    i2 = jnp.min(jnp.where(sel2 == m2, erow, MOE_EPG), axis=0, keepdims=True)
    e21 = jnp.exp(m2 - m1)
    w1 = g_val / (1.0 + e21)
    w2 = g_val * e21 / (1.0 + e21)
    e1 = g_idx * MOE_EPG + i1
    e2 = g_idx * MOE_EPG + i2
    xrow = lax.broadcasted_iota(jnp.int32, (ne, tm), 0)
    oh1 = xrow == e1
    oh2 = xrow == e2
    cmat = jnp.where(oh1 | oh2, 1.0, 0.0)
    ti = lax.broadcasted_iota(jnp.int32, (tm, tm), 0)
    tj = lax.broadcasted_iota(jnp.int32, (tm, tm), 1)
    su = jnp.where(ti < tj, 1.0, 0.0).astype(BF16)
    before = jnp.dot(cmat.astype(BF16), su, preferred_element_type=F32) + carry_ref[...]
    r1 = jnp.sum(jnp.where(oh1, before, 0.0), axis=0, keepdims=True)
    r2 = jnp.sum(jnp.where(oh2, before, 0.0), axis=0, keepdims=True)
    carry = carry_ref[...] + jnp.sum(cmat, axis=1, keepdims=True)
    carry_ref[...] = carry
    cnt_ref[...] = jnp.broadcast_to(carry, cnt_ref.shape)
    o_ref[...] = jnp.concatenate([e1.astype(F32), e2.astype(F32), w1, w2, r1, r2,
                                  jnp.zeros((2, tm), F32)], axis=0)


def _invert_kernel(d1_ref, d2_ref, cnt_ref, off_ref, end_ref, slot_ref, *, n_tok, unroll):
    ne = cnt_ref.shape[0]

    def pad_expert(e, p):
        lo = off_ref[e] + cnt_ref[e]

        def pad_row(r, p):
            slot_ref[r] = 2 * n_tok + p
            return p + 1

        return lax.fori_loop(lo, end_ref[e], pad_row, p)

    lax.fori_loop(0, ne, pad_expert, 0)

    def tok_block(i, c):
        for j in range(unroll):
            t = i * unroll + j
            slot_ref[d1_ref[t]] = t
            slot_ref[d2_ref[t]] = n_tok + t
        return c

    lax.fori_loop(0, n_tok // unroll, tok_block, 0)


def _expert_kernel(te_ref, nv_ref, nb_ref, slot_ref, hp_ref, wg_ref, wu_ref, wd_ref, ys_ref,
                   xb0_ref, xb1_ref, yb0_ref, yb1_ref, wgb_ref, wub_ref, wdb_ref, gsem, ssem, *, n_tok):
    i = pl.program_id(0)
    nv = nv_ref[0]
    tile = xb0_ref.shape[0]
    pk = ys_ref.shape[1]
    blk = MOE_DMA_BLOCK

    def for_blocks(tile_idx, fn):
        nb = nb_ref[tile_idx]
        for b in range(tile // blk):
            @pl.when(b < nb)
            def _():
                fn(b)

    def gather(tile_idx, xb_ref, sem):
        def issue(b):
            for r in range(b * blk, (b + 1) * blk):
                s = slot_ref[tile_idx * tile + r]
                tok = jnp.where(s >= 2 * n_tok, 0, jnp.where(s >= n_tok, s - n_tok, s))
                pltpu.make_async_copy(hp_ref.at[pl.ds(tok, 1)], xb_ref.at[pl.ds(r, 1)], sem).start(priority=1)
        for_blocks(tile_idx, issue)

    def wait_gather(tile_idx, xb_ref, sem):
        part = xb_ref.at[pl.ds(0, blk)]
        for_blocks(tile_idx, lambda b: pltpu.make_async_copy(part, part, sem).wait())

    def scatter(tile_idx, yb_ref, sem):
        def issue(b):
            for r in range(b * blk, (b + 1) * blk):
                s = slot_ref[tile_idx * tile + r]
                pltpu.make_async_copy(yb_ref.at[pl.ds(r * pk, pk)], ys_ref.at[s], sem).start(priority=r % 2)
        for_blocks(tile_idx, issue)

    def wait_scatter(tile_idx, yb_ref, sem):
        part = yb_ref.at[pl.ds(0, blk * pk)]
        for_blocks(tile_idx, lambda b: pltpu.make_async_copy(part, part, sem).wait())

    new_expert = (i == 0) | (te_ref[i] != te_ref[jnp.maximum(i - 1, 0)])

    @pl.when(new_expert & (i < nv))
    def _():
        wgb_ref[...] = wg_ref[0, 0].astype(BF16)
        wub_ref[...] = wu_ref[0, 0].astype(BF16)
        wdb_ref[...] = wd_ref[0, 0].astype(BF16)

    @pl.when(i == 0)
    def _():
        xb0_ref[...] = jnp.zeros_like(xb0_ref)
        xb1_ref[...] = jnp.zeros_like(xb1_ref)
        gather(0, xb0_ref, gsem.at[0])

    def step(p, xb_ref, xb_next_ref, yb_ref, yb_other_ref):
        nxt = jnp.minimum(i + 1, nv - 1)
        gather(nxt, xb_next_ref, gsem.at[1 - p])
        wait_gather(i, xb_ref, gsem.at[p])

        @pl.when(i >= 2)
        def _():
            wait_scatter(i - 2, yb_ref, ssem.at[p])

        x = _unpack_bf16_pairs(xb_ref[...]).astype(BF16)
        hg = jnp.dot(x, wgb_ref[...], preferred_element_type=F32)
        hu = jnp.dot(x, wub_ref[...], preferred_element_type=F32)
        hid = (hg * jax.nn.sigmoid(hg) * hu).astype(BF16)
        y = jnp.dot(hid, wdb_ref[...], preferred_element_type=F32)
        yp = _pack_bf16_pairs(y)
        lanes = yb_ref.shape[1]
        for j in range(pk):
            yb_ref[pl.ds(j, tile, stride=pk), :] = yp[:, j * lanes:(j + 1) * lanes]
        scatter(i, yb_ref, ssem.at[p])

        @pl.when(i == nv - 1)
        def _():
            wait_gather(nxt, xb_next_ref, gsem.at[1 - p])
            wait_scatter(i, yb_ref, ssem.at[p])

            @pl.when(i >= 1)
            def _():
                wait_scatter(i - 1, yb_other_ref, ssem.at[1 - p])

    odd = lax.rem(i, 2) == 1

    @pl.when((i < nv) & jnp.logical_not(odd))
    def _():
        step(0, xb0_ref, xb1_ref, yb0_ref, yb1_ref)

    @pl.when((i < nv) & odd)
    def _():
        step(1, xb1_ref, xb0_ref, yb1_ref, yb0_ref)


def _combine_kernel(x_ref, w_ref, gt_ref, fn_ref, ya_ref, yb_ref, o_ref, *, final_norm):
    w = w_ref[...]
    tm = x_ref.shape[0]
    pk = ya_ref.shape[0] // tm

    def rows(ref):
        return _unpack_bf16_pairs(jnp.concatenate([ref[pl.ds(j, tm, stride=pk), :] for j in range(pk)], axis=1))

    y = w[:, 0:1] * rows(ya_ref) + w[:, 1:2] * rows(yb_ref)
    out = x_ref[...] + gt_ref[0] * y
    if final_norm:
        ms = jnp.mean(out * out, axis=-1, keepdims=True)
        out = out * lax.rsqrt(ms + EPS) * fn_ref[...]
    o_ref[...] = out


def _hier_moe(x2d, seq, g, sh, sc, gate, r1_w, r1_b, r2_w, r2_b, wg, wu, wd, layer, final_w):
    t, d = x2d.shape
    ne = MOE_EXPERTS
    rw = jnp.zeros((128, d), F32).at[0:MOE_GROUPS].set(r1_w.T).at[8:8 + ne].set(r2_w.T)
    rb = jnp.zeros((128, 1), F32).at[0:MOE_GROUPS, 0].set(r1_b).at[8:8 + ne, 0].set(r2_b)
    rw_hi = rw.astype(BF16)
    rw_lo = (rw - rw_hi.astype(F32)).astype(BF16)
    tmr = min(512, seq)
    bmap = lambda i: ((i * tmr) // seq, 0, 0)
    route, cnt, hp = pl.pallas_call(
        _router_kernel,
        out_shape=(jax.ShapeDtypeStruct((8, t), F32), jax.ShapeDtypeStruct((ne, 128), F32),
                   jax.ShapeDtypeStruct((t, d // 2), jnp.uint32)),
        grid=(t // tmr,),
        in_specs=[pl.BlockSpec((tmr, d), lambda i: (i, 0)),
                  pl.BlockSpec((1, d), lambda i: (0, 0)),
                  pl.BlockSpec((1, 1, d), bmap), pl.BlockSpec((1, 1, d), bmap),
                  pl.BlockSpec((128, d), lambda i: (0, 0)), pl.BlockSpec((128, d), lambda i: (0, 0)),
                  pl.BlockSpec((128, 1), lambda i: (0, 0))],
        out_specs=(pl.BlockSpec((8, tmr), lambda i: (0, i)),
                   pl.BlockSpec((ne, 128), lambda i: (0, 0)),
                   pl.BlockSpec((tmr, d // 2), lambda i: (i, 0))),
        scratch_shapes=[pltpu.VMEM((ne, 1), F32)],
        compiler_params=_cparams("arbitrary"), name="moe_router",
    )(x2d, g.reshape(1, d), sh[:, None, :], sc[:, None, :], rw_hi, rw_lo, rb)

    e1 = route[0].astype(jnp.int32)
    e2 = route[1].astype(jnp.int32)
    tile = EXPERT_TILE
    counts = cnt[:, 0].astype(jnp.int32)
    padded = ((counts + tile - 1) // tile) * tile
    ends = jnp.cumsum(padded)
    offs = ends - padded
    dest1 = offs[e1] + route[4].astype(jnp.int32)
    dest2 = offs[e2] + route[5].astype(jnp.int32)
    nt = (2 * t) // tile + ne
    n_used = ends[-1] // tile
    tid = jnp.arange(nt, dtype=jnp.int32)
    tsrc = jnp.minimum(tid, n_used - 1)
    texp = jnp.minimum(jnp.sum((tsrc[:, None] * tile >= ends[None, :]).astype(jnp.int32), axis=1), ne - 1)
    wts = jnp.concatenate([route[2:4].T, jnp.zeros((t, 6), F32)], axis=1)
    tvalid = jnp.clip(offs[texp] + counts[texp] - tsrc * tile, 0, tile)
    tblocks = (tvalid + MOE_DMA_BLOCK - 1) // MOE_DMA_BLOCK

    smem = pl.BlockSpec(memory_space=pltpu.SMEM)
    pk = d // 256
    slot = pl.pallas_call(
        functools.partial(_invert_kernel, n_tok=t, unroll=8),
        out_shape=jax.ShapeDtypeStruct((nt * tile,), jnp.int32),
        in_specs=[smem] * 5, out_specs=smem, name="moe_invert",
    )(dest1, dest2, counts, offs, ends)

    ff = wg.shape[3]
    n_trash = ne * tile
    ys = pl.pallas_call(
        functools.partial(_expert_kernel, n_tok=t),
        out_shape=jax.ShapeDtypeStruct((2 * t + n_trash, pk, 128), jnp.uint32),
        grid_spec=pltpu.PrefetchScalarGridSpec(
            num_scalar_prefetch=4, grid=(nt,),
            in_specs=[pl.BlockSpec(memory_space=pl.ANY),
                      pl.BlockSpec((1, 1, d, ff), lambda i, te, nv, nb, sl: (layer, te[i], 0, 0)),
                      pl.BlockSpec((1, 1, d, ff), lambda i, te, nv, nb, sl: (layer, te[i], 0, 0)),
                      pl.BlockSpec((1, 1, ff, d), lambda i, te, nv, nb, sl: (layer, te[i], 0, 0))],
            out_specs=pl.BlockSpec(memory_space=pl.ANY),
            scratch_shapes=[pltpu.VMEM((tile, d // 2), jnp.uint32), pltpu.VMEM((tile, d // 2), jnp.uint32),
                            pltpu.VMEM((tile * pk, 128), jnp.uint32), pltpu.VMEM((tile * pk, 128), jnp.uint32),
                            pltpu.VMEM((d, ff), BF16), pltpu.VMEM((d, ff), BF16), pltpu.VMEM((ff, d), BF16),
                            pltpu.SemaphoreType.DMA((2,)), pltpu.SemaphoreType.DMA((2,))]),
        compiler_params=_cparams("arbitrary"), name="moe_experts",
    )(texp, n_used.reshape(1).astype(jnp.int32), tblocks.astype(jnp.int32), slot, hp, wg, wu, wd)

    ys2 = ys.reshape(-1, 128)
    fin = final_w is not None
    fw = (final_w if fin else jnp.ones((d,), F32)).reshape(1, d)
    tmc = min(512, seq)
    nbc = t // tmc
    return pl.pallas_call(
        functools.partial(_combine_kernel, final_norm=fin),
        out_shape=jax.ShapeDtypeStruct((t, d), F32), grid=(nbc,),
        in_specs=[pl.BlockSpec((tmc, d), lambda i: (i, 0)),
                  pl.BlockSpec((tmc, 8), lambda i: (i, 0)),
                  pl.BlockSpec((1, 1, d), lambda i: ((i * tmc) // seq, 0, 0)),
                  pl.BlockSpec((1, d), lambda i: (0, 0)),
                  pl.BlockSpec((tmc * pk, 128), lambda i: (i, 0)),
                  pl.BlockSpec((tmc * pk, 128), lambda i: (i + nbc, 0))],
        out_specs=pl.BlockSpec((tmc, d), lambda i: (i, 0)),
        compiler_params=_cparams("arbitrary"), name="moe_combine",
    )(x2d, wts, gate[:, None, :], fw, ys2, ys2)


def _ssd_kernel(z_ref, x_ref, b_ref, c_ref, dt_ref, cwx_ref, cwb_ref, cwc_ref, cbx_ref, cbb_ref, cbc_ref,
                dtb_ref, alog_ref, dsk_ref, nw_ref, o_ref,
                xe_ref, be_ref, ce_ref, xs_ref, bs_ref, cs_ref, st_ref, acs_ref, dtg_ref):
    c = pl.program_id(1)
    lq = SSD_CHUNK
    ng = SSD_GROUPS
    gw = x_ref.shape[1] // ng
    hpg = gw // SSD_HEAD_DIM
    halo = xe_ref.shape[0] - lq

    @pl.when(c == 0)
    def _():
        xe_ref[0:halo] = jnp.zeros((halo, xe_ref.shape[1]), BF16)
        be_ref[0:halo] = jnp.zeros((halo, be_ref.shape[1]), BF16)
        ce_ref[0:halo] = jnp.zeros((halo, ce_ref.shape[1]), BF16)
        st_ref[...] = jnp.zeros_like(st_ref)

    xe_ref[halo:halo + lq] = x_ref[...]
    be_ref[halo:halo + lq] = b_ref[...]
    ce_ref[halo:halo + lq] = c_ref[...]

    sr = lax.broadcasted_iota(jnp.int32, (SSD_CONV * lq, halo + lq), 0)
    sm = lax.broadcasted_iota(jnp.int32, (SSD_CONV * lq, halo + lq), 1)
    tap = sr // lq
    shift_mat = jnp.where(sm == (sr - tap * lq) + halo - (SSD_CONV - 1) + tap, 1.0, 0.0).astype(BF16)

    def conv_silu(e_ref, w_ref, bias_ref, o_ref, width):
        for c0 in range(0, e_ref.shape[1], width):
            cols = slice(c0, c0 + width)
            e = e_ref[:, cols]
            acc = bias_ref[:, cols] + e[halo:].astype(F32) * w_ref[SSD_CONV - 1:SSD_CONV, cols]
            for k in range(SSD_CONV - 1):
                sh = jnp.dot(shift_mat[k * lq:(k + 1) * lq], e, preferred_element_type=F32)
                acc = acc + sh * w_ref[k:k + 1, cols]
            o_ref[:, cols] = acc + acc * jnp.tanh(acc)

    conv_silu(xe_ref, cwx_ref, cbx_ref, xs_ref, gw)
    conv_silu(be_ref, cwb_ref, cbb_ref, bs_ref, gw)
    conv_silu(ce_ref, cwc_ref, cbc_ref, cs_ref, gw)

    dt = jax.nn.softplus(dt_ref[...] + dtb_ref[...])
    adt = dt * (-jnp.exp(alog_ref[...]))
    li = lax.broadcasted_iota(jnp.int32, (lq, lq), 0)
    si = lax.broadcasted_iota(jnp.int32, (lq, lq), 1)
    tri = jnp.where(si <= li, 1.0, 0.0)
    acs = jnp.dot(tri, adt, precision=HIGHEST, preferred_element_type=F32)
    lane = lax.broadcasted_iota(jnp.int32, (lq, 128), 1)
    for gi in range(ng):
        sh = (128 - gi * hpg) % 128
        acs_ref[gi] = jnp.where(lane < hpg, pltpu.roll(acs, sh, axis=1) if sh else acs, 0.0)
        dtg_ref[gi] = jnp.where(lane < hpg, pltpu.roll(dt, sh, axis=1) if sh else dt, 0.0)
    causal = si <= li
    lane_lo = lane < SSD_HEAD_DIM

    def group_body(gi, carry):
        xc0 = pl.multiple_of(gi * gw, gw)
        nc0 = pl.multiple_of(gi * SSD_STATE, SSD_STATE)
        xg = xs_ref[:, pl.ds(xc0, gw)]
        bg = bs_ref[:, pl.ds(nc0, SSD_STATE)]
        cg = cs_ref[:, pl.ds(nc0, SSD_STATE)]
        bt = bg.T
        cgb = cg.astype(BF16)
        cb = jnp.dot(cgb, bt.astype(BF16), preferred_element_type=F32)
        acs_g = acs_ref[gi]
        acs_t = acs_g.T
        dt_t = dtg_ref[gi].T
        xgb = xg.astype(BF16)
        ys = []
        for jp in range(hpg // 2):
            lhs_parts, w_parts, ecols, dlast = [], [], [], []
            for j in (2 * jp, 2 * jp + 1):
                e_col = jnp.broadcast_to(acs_g[:, j:j + 1], (lq, lq))
                dt_row = dt_t[j:j + 1, :]
                r_row = jnp.broadcast_to(acs_t[j:j + 1, :] - jnp.log(dt_row), (lq, lq))
                lmat = jnp.where(causal, jnp.exp(e_col - r_row), 0.0) * cb
                lhs_parts.append(lmat.astype(BF16))
                a_last = acs_t[j:j + 1, lq - 1:lq]
                wrow = dt_row * jnp.exp(a_last - acs_t[j:j + 1, :])
                w_parts.append((bt * wrow).astype(BF16))
                ecols.append(e_col)
                dlast.append(jnp.exp(a_last))
            lhs = jnp.concatenate([jnp.concatenate(lhs_parts, axis=1),
                                   jnp.concatenate(w_parts, axis=1)], axis=0)
            xp = xgb[:, jp * 128:(jp + 1) * 128]
            zero = jnp.zeros_like(xp)
            rhs = jnp.concatenate([jnp.where(lane_lo, xp, zero), jnp.where(lane_lo, zero, xp)], axis=0)
            res = jnp.dot(lhs, rhs, preferred_element_type=F32)
            st = st_ref[gi, :, jp * 128:(jp + 1) * 128]
            yoff = jnp.dot(cgb, st.astype(BF16), preferred_element_type=F32)
            escale = jnp.exp(jnp.where(lane_lo, ecols[0], ecols[1]))
            ys.append(res[:lq] + yoff * escale)
            dl = jnp.where(lane_lo[:1], dlast[0], dlast[1])
            st_ref[gi, :, jp * 128:(jp + 1) * 128] = st * dl + res[lq:]
        y = jnp.concatenate(ys, axis=1) + dsk_ref[:, pl.ds(xc0, gw)] * xg
        zg = z_ref[:, pl.ds(xc0, gw)].astype(F32)
        y = y * (zg * jax.nn.sigmoid(zg))
        ms = jnp.mean(y * y, axis=-1, keepdims=True)
        y = y * lax.rsqrt(ms + EPS) * nw_ref[:, pl.ds(xc0, gw)]
        o_ref[:, pl.ds(xc0, gw)] = y.astype(o_ref.dtype)
        return carry

    lax.fori_loop(0, ng, group_body, 0)
    xe_ref[0:halo] = xe_ref[lq:lq + halo]
    be_ref[0:halo] = be_ref[lq:lq + halo]
    ce_ref[0:halo] = ce_ref[lq:lq + halo]


def _ssd_mixer(zx, dt_raw, bsz, seq, conv_w, conv_b, dt_bias, a_log, d_skip, norm_w):
    t = zx.shape[0]
    heads = a_log.shape[0]
    inner = heads * SSD_HEAD_DIM
    bcw = SSD_GROUPS * SSD_STATE
    lq = SSD_CHUNK
    nc = seq // lq
    pad = 128 - heads
    cw = 0.5 * conv_w.astype(F32)
    cwp = jnp.zeros((8, cw.shape[1]), F32).at[:SSD_CONV].set(cw)
    cb = 0.5 * conv_b.astype(F32).reshape(1, -1)
    row = lambda b, c: (b * nc + c, 0)
    const = lambda b, c: (0, 0)
    xb = inner // inner
    bb = (2 * inner) // bcw
    return pl.pallas_call(
        _ssd_kernel, out_shape=jax.ShapeDtypeStruct((t, inner), BF16), grid=(bsz, nc),
        in_specs=[pl.BlockSpec((lq, inner), lambda b, c: (b * nc + c, 0)),
                  pl.BlockSpec((lq, inner), lambda b, c: (b * nc + c, xb)),
                  pl.BlockSpec((lq, bcw), lambda b, c: (b * nc + c, bb)),
                  pl.BlockSpec((lq, bcw), lambda b, c: (b * nc + c, bb + 1)),
                  pl.BlockSpec((lq, 128), row),
                  pl.BlockSpec((8, inner), const), pl.BlockSpec((8, bcw), const), pl.BlockSpec((8, bcw), const),
                  pl.BlockSpec((1, inner), const), pl.BlockSpec((1, bcw), const), pl.BlockSpec((1, bcw), const),
                  pl.BlockSpec((1, 128), const), pl.BlockSpec((1, 128), const),
                  pl.BlockSpec((1, inner), const), pl.BlockSpec((1, inner), const)],
        out_specs=pl.BlockSpec((lq, inner), row),
        scratch_shapes=[pltpu.VMEM((lq + 16, inner), BF16), pltpu.VMEM((lq + 16, bcw), BF16),
                        pltpu.VMEM((lq + 16, bcw), BF16),
                        pltpu.VMEM((lq, inner), F32), pltpu.VMEM((lq, bcw), F32), pltpu.VMEM((lq, bcw), F32),
                        pltpu.VMEM((SSD_GROUPS, SSD_STATE, inner // SSD_GROUPS), F32),
                        pltpu.VMEM((SSD_GROUPS, lq, 128), F32), pltpu.VMEM((SSD_GROUPS, lq, 128), F32)],
        compiler_params=_cparams("arbitrary", "arbitrary"), name="ssd_mixer",
    )(zx, zx, zx, zx, dt_raw,
      cwp[:, :inner], cwp[:, inner:inner + bcw], cwp[:, inner + bcw:],
      cb[:, :inner], cb[:, inner:inner + bcw], cb[:, inner + bcw:],
      jnp.pad(dt_bias.astype(F32), (0, pad)).reshape(1, 128),
      jnp.pad(a_log.astype(F32), (0, pad)).reshape(1, 128),
      jnp.repeat(d_skip.astype(F32), SSD_HEAD_DIM).reshape(1, inner),
      norm_w.astype(F32).reshape(1, inner))


def kernel(x, c, positions, ada_w, ada_b, norm_mix, norm_ffn, mix_a_w_in, s5_lam_re, s5_lam_im, s5_log_dt, s5_b_re, s5_b_im, s5_c_re, s5_c_im, s5_d, s5_w_glu, attn_sinks, mix_a_w_out, ssd_w_in, ssd_conv_w, ssd_conv_b, ssd_dt_bias, ssd_a_log, ssd_d, ssd_norm_w, ssd_w_out, moe_r1_w, moe_r1_b, moe_r2_w, moe_r2_b, moe_w_gate, moe_w_up, moe_w_down, final_norm):
    bsz, seq, d = x.shape
    depth = ada_w.shape[0]
    t = bsz * seq
    xf = x.reshape(t, d).astype(F32)
    mod = _adaln(c.astype(F32), ada_w, ada_b)
    cos_t, sin_t = _rope_tables(positions)

    for i in range(depth):
        sh1, sc1, g1, sh2, sc2, g2 = jnp.split(mod[i], 6, axis=-1)
        j = i // 2
        if i % 2 == 0:
            s5w = s5_lam_re.shape[1] * S5_GROUP
            w_in = mix_a_w_in[j]
            u, qkv = _inproj0(xf, seq, norm_mix[i], sh1, sc1, w_in[:, :s5w], w_in[:, s5w:], tm=512)
            params = _s5_params(s5_lam_re[j], s5_lam_im[j], s5_log_dt[j], s5_b_re[j], s5_b_im[j],
                                s5_c_re[j], s5_c_im[j], s5_d[j], seq)
            y_s5 = _s5_mixer(u, bsz, seq, params)
            y_glu = _gelu_glu(y_s5, s5_w_glu[j], tm=2048, tn=512)
            wq = attn_sinks.shape[1] * HEAD_DIM
            wk = KV_HEADS * HEAD_DIM
            y_att = _attention(qkv, bsz, seq, cos_t, sin_t, attn_sinks[j],
                               q_col=0, k_col=wq // wk, v_col=wq // wk + 1)
            xf = _matmul_residual([y_glu, y_att], mix_a_w_out[j], xf, seq, g1, tm=1024, tn=1024)
        else:
            heads = ssd_a_log.shape[1]
            inner = heads * SSD_HEAD_DIM
            main = inner + inner + 2 * SSD_GROUPS * SSD_STATE
            w_in = ssd_w_in[j]
            w_dt = jnp.pad(w_in[:, main:], ((0, 0), (0, 128 - heads))).astype(BF16)
            zx, dt_raw = _inproj1(xf, seq, norm_mix[i], sh1, sc1, w_in, main, w_dt, tm=1024, tn=1024)
            y = _ssd_mixer(zx, dt_raw, bsz, seq, ssd_conv_w[j], ssd_conv_b[j], ssd_dt_bias[j],
                           ssd_a_log[j], ssd_d[j], ssd_norm_w[j])
            xf = _matmul_residual([y], ssd_w_out[j], xf, seq, g1, tm=1024, tn=512)
        xf = _hier_moe(xf, seq, norm_ffn[i], sh2, sc2, g2, moe_r1_w[i], moe_r1_b[i], moe_r2_w[i], moe_r2_b[i],
                       moe_w_gate, moe_w_up, moe_w_down, i,
                       final_norm if i == depth - 1 else None)
    return xf.reshape(bsz, seq, d).astype(x.dtype)
```
